```python
import math
import jax, jax.numpy as jnp
from jax import lax
import numpy as np

D_MODEL = 2048
BATCH = 2
SEQ = 8192
DEPTH = 2

HEAD_DIM = 128
RET_HEADS = 6
RET_W = RET_HEADS * HEAD_DIM
RET_CHUNK = 128
DIFF_HEADS = 4
DIFF_QK = HEAD_DIM // 2
DIFF_W = DIFF_HEADS * HEAD_DIM
Q_BLOCK = 128
LRU_W = 768
LRU_BLOCKS = 8
LRU_BW = LRU_W // LRU_BLOCKS
CONV_W = 4
LRU_C = 8.0
D_MIX = RET_W + DIFF_W + LRU_W
SPLIT_SIZES = (RET_W, RET_W, RET_W, RET_W,
               DIFF_W, DIFF_W, DIFF_W, DIFF_W,
               LRU_W, LRU_W)
D_IN = sum(SPLIT_SIZES)
ROPE_THETA = 10000.0
EPS = 1e-6

kernel_name = "hybrid_retention_diffattn_rglru_block"


def rms_norm(x, g=None):
    x32 = x.astype(jnp.float32)
    y = x32 * lax.rsqrt(jnp.mean(x32 * x32, axis=-1, keepdims=True) + EPS)
    if g is not None:
        y = y * g.astype(jnp.float32)
    return y.astype(x.dtype)


def rope(x, positions, inv_freq):
    ang = positions.astype(jnp.float32)[..., None] * inv_freq
    cos = jnp.cos(ang)[:, :, None, :].astype(x.dtype)
    sin = jnp.sin(ang)[:, :, None, :].astype(x.dtype)
    x1, x2 = jnp.split(x, 2, axis=-1)
    return jnp.concatenate([x1 * cos - x2 * sin, x2 * cos + x1 * sin], axis=-1)


def retention(q, k, v, positions):
    B, S, H, D = q.shape
    dt = q.dtype
    inv = 1.0 / (ROPE_THETA ** jnp.linspace(0.0, 1.0, D // 2, dtype=jnp.float32))
    q = rope(q, positions, inv)
    k = rope(k, positions, inv) * (D ** -0.5)
    log_g = jnp.log1p(-jnp.power(2.0, -5.0 - jnp.arange(H, dtype=jnp.float32)))
    C = RET_CHUNK
    N = S // C
    idx = jnp.arange(C, dtype=jnp.float32)
    rel = idx[:, None] - idx[None, :]
    intra_decay = jnp.where(rel[None] >= 0,
                            jnp.exp(jnp.maximum(rel, 0.0)[None] * log_g[:, None, None]),
                            0.0).astype(dt)
    qc = q.reshape(B, N, C, H, D)
    kc = k.reshape(B, N, C, H, D)
    vc = v.reshape(B, N, C, H, D)
    scores = jnp.einsum('bnihd,bnjhd->bnhij', qc, kc) * intra_decay
    o_intra = jnp.einsum('bnhij,bnjhe->bnihe', scores, vc)
    k_w = jnp.exp((C - 1.0 - idx)[:, None] * log_g[None, :]).astype(dt)
    kv = jnp.einsum('bnjhd,bnjhe->nbhde', kc * k_w[:, :, None], vc)
    chunk_decay = jnp.exp(C * log_g)[None, :, None, None].astype(kv.dtype)

    def step(state, kv_n):
        return state * chunk_decay + kv_n, state

    _, s_prev = lax.scan(step, jnp.zeros_like(kv[0]), kv)
    q_w = jnp.exp((idx + 1.0)[:, None] * log_g[None, :]).astype(dt)
    o_cross = jnp.einsum('bnihd,nbhde->bnihe', qc * q_w[:, :, None], s_prev)
    return (o_intra + o_cross).reshape(B, S, H, D)


def diff_attention(q, k, v, positions, lam):
    B, S, H, _, DQ = q.shape
    inv = 1.0 / (ROPE_THETA ** (jnp.arange(0, DQ, 2, dtype=jnp.float32) / DQ))
    q = rope(q.reshape(B, S, H * 2, DQ), positions, inv).reshape(B, S, H, 2, DQ) * (DQ ** -0.5)
    k = rope(k.reshape(B, S, H * 2, DQ), positions, inv).reshape(B, S, H, 2, DQ)
    kpos = jnp.arange(S)

    def block(i):
        qb = lax.dynamic_slice_in_dim(q, i * Q_BLOCK, Q_BLOCK, axis=1)
        s = jnp.einsum('bqhcd,bkhcd->bhcqk', qb, k).astype(jnp.float32)
        qpos = i * Q_BLOCK + jnp.arange(Q_BLOCK)
        s = jnp.where(kpos[None, :] <= qpos[:, None], s, -jnp.inf)
        p = jax.nn.softmax(s, axis=-1)
        w = p[:, :, 0] - lam * p[:, :, 1]
        return jnp.einsum('bhqk,bkhe->bqhe', w.astype(v.dtype), v)

    o = lax.map(block, jnp.arange(S // Q_BLOCK))
    return jnp.transpose(o, (1, 0, 2, 3, 4)).reshape(B, S, H, v.shape[-1])


def rg_lru(x, conv_w, conv_b, wa, ba, wx, bx, lam):
    B, S, W = x.shape
    xp = jnp.pad(x, ((0, 0), (CONV_W - 1, 0), (0, 0)))
    xc = conv_b + xp[:, 0:S] * conv_w[0]
    for j in range(1, CONV_W):
        xc = xc + xp[:, j:j + S] * conv_w[j]
    xb = xc.reshape(B, S, LRU_BLOCKS, LRU_BW)
    r = jax.nn.sigmoid(jnp.einsum('bsnk,nkj->bsnj', xb, wa).reshape(B, S, W) + ba)
    i = jax.nn.sigmoid(jnp.einsum('bsnk,nkj->bsnj', xb, wx).reshape(B, S, W) + bx)
    log_a = (LRU_C * r.astype(jnp.float32)) * jax.nn.log_sigmoid(lam.astype(jnp.float32))
    a = jnp.exp(log_a)
    mult = jnp.sqrt(-jnp.expm1(2.0 * log_a))
    b = mult * (i * xc).astype(jnp.float32)

    def combine(left, right):
        a1, b1 = left
        a2, b2 = right
        return a1 * a2, a2 * b1 + b2

    _, h = lax.associative_scan(combine, (a, b), axis=1)
    return h.astype(x.dtype)


def setup_inputs(seed: int = 0) -> dict:
    key = jax.random.key(seed)
    ks = jax.random.split(key, 20)
    f32 = jnp.float32
    x = jax.random.normal(ks[0], (BATCH, SEQ, D_MODEL), f32)
    positions = jnp.broadcast_to(jnp.arange(SEQ, dtype=jnp.int32), (BATCH, SEQ))
    pre_norm_g = 1.0 + 0.05 * jax.random.normal(ks[1], (DEPTH, D_MODEL), f32)
    w_in = jax.random.normal(ks[2], (DEPTH, D_MODEL, D_IN), f32) * (D_MODEL ** -0.5)
    diff_lambda_q1 = 0.1 * jax.random.normal(ks[3], (DEPTH, DIFF_QK), f32)
    diff_lambda_k1 = 0.1 * jax.random.normal(ks[4], (DEPTH, DIFF_QK), f32)
    diff_lambda_q2 = 0.1 * jax.random.normal(ks[5], (DEPTH, DIFF_QK), f32)
    diff_lambda_k2 = 0.1 * jax.random.normal(ks[6], (DEPTH, DIFF_QK), f32)
    diff_subln_g = 1.0 + 0.05 * jax.random.normal(ks[7], (DEPTH, HEAD_DIM), f32)
    lru_conv_w = jax.random.normal(ks[8], (DEPTH, CONV_W, LRU_W), f32) * (CONV_W ** -0.5)
    lru_conv_b = 0.01 * jax.random.normal(ks[9], (DEPTH, LRU_W), f32)
    lru_wa = jax.random.normal(ks[10], (DEPTH, LRU_BLOCKS, LRU_BW, LRU_BW), f32) * (LRU_BW ** -0.5)
    lru_ba = 0.01 * jax.random.normal(ks[11], (DEPTH, LRU_W), f32)
    lru_wx = jax.random.normal(ks[12], (DEPTH, LRU_BLOCKS, LRU_BW, LRU_BW), f32) * (LRU_BW ** -0.5)
    lru_bx = 0.01 * jax.random.normal(ks[13], (DEPTH, LRU_W), f32)
    u = jax.random.uniform(ks[14], (DEPTH, LRU_W), f32, minval=0.9, maxval=0.999)
    s = u ** (1.0 / LRU_C)
    lru_lambda = jnp.log(s) - jnp.log1p(-s)
    w_out = jax.random.normal(ks[15], (DEPTH, D_MIX, D_MODEL), f32) * (D_MIX ** -0.5)
    post_norm_g = 1.0 + 0.05 * jax.random.normal(ks[16], (DEPTH, D_MODEL), f32)
    return {"x": x, "positions": positions, "pre_norm_g": pre_norm_g, "w_in": w_in,
            "diff_lambda_q1": diff_lambda_q1, "diff_lambda_k1": diff_lambda_k1,
            "diff_lambda_q2": diff_lambda_q2, "diff_lambda_k2": diff_lambda_k2,
            "diff_subln_g": diff_subln_g, "lru_conv_w": lru_conv_w, "lru_conv_b": lru_conv_b,
            "lru_wa": lru_wa, "lru_ba": lru_ba, "lru_wx": lru_wx, "lru_bx": lru_bx,
            "lru_lambda": lru_lambda, "w_out": w_out, "post_norm_g": post_norm_g}


def reference(x, positions, pre_norm_g, w_in, diff_lambda_q1, diff_lambda_k1,
              diff_lambda_q2, diff_lambda_k2, diff_subln_g, lru_conv_w, lru_conv_b,
              lru_wa, lru_ba, lru_wx, lru_bx, lru_lambda, w_out, post_norm_g):
    B, S, _ = x.shape
    split_idx = [int(v) for v in np.cumsum(SPLIT_SIZES)[:-1]]
    for l in range(DEPTH):
        h = rms_norm(x, pre_norm_g[l])
        proj = jnp.einsum('bsd,de->bse', h, w_in[l])
        (rq, rk, rv, rg, dq, dk, dv, dg, lx, lg) = jnp.split(proj, split_idx, axis=-1)

        ro = retention(rq.reshape(B, S, RET_HEADS, HEAD_DIM),
                       rk.reshape(B, S, RET_HEADS, HEAD_DIM),
                       rv.reshape(B, S, RET_HEADS, HEAD_DIM), positions)
        ret_out = rms_norm(ro).reshape(B, S, RET_W) * jax.nn.silu(rg)

        lam_init = 0.8 - 0.6 * math.exp(-0.3 * l)
        lam = (jnp.exp(jnp.sum(diff_lambda_q1[l] * diff_lambda_k1[l]).astype(jnp.float32))
               - jnp.exp(jnp.sum(diff_lambda_q2[l] * diff_lambda_k2[l]).astype(jnp.float32))
               + lam_init)
        do = diff_attention(dq.reshape(B, S, DIFF_HEADS, 2, DIFF_QK),
                            dk.reshape(B, S, DIFF_HEADS, 2, DIFF_QK),
                            dv.reshape(B, S, DIFF_HEADS, HEAD_DIM), positions, lam)
        diff_out = (rms_norm(do, diff_subln_g[l]) * (1.0 - lam_init)).reshape(B, S, DIFF_W) * jax.nn.silu(dg)

        lru_out = rg_lru(lx, lru_conv_w[l], lru_conv_b[l], lru_wa[l], lru_ba[l],
                         lru_wx[l], lru_bx[l], lru_lambda[l]) * jax.nn.silu(lg)

        mixed = jnp.concatenate([ret_out, diff_out, lru_out], axis=-1)
        y = jnp.einsum('bse,ed->bsd', mixed, w_out[l])
        x = x + rms_norm(y, post_norm_g[l])
    return x
```

```python
import functools
import math

import jax
import jax.numpy as jnp
from jax import lax
from jax.experimental import pallas as pl
from jax.experimental.pallas import tpu as pltpu

F32 = jnp.float32
BF16 = jnp.bfloat16

D_MODEL = 2048
HEAD_DIM = 128
RET_HEADS = 6
RET_W = RET_HEADS * HEAD_DIM
RET_CHUNK = 128
DIFF_HEADS = 4
DIFF_QK = HEAD_DIM // 2
DIFF_W = DIFF_HEADS * HEAD_DIM
LRU_W = 768
LRU_BLOCKS = 8
LRU_BW = LRU_W // LRU_BLOCKS
CONV_W = 4
LRU_C = 8.0
D_MIX = RET_W + DIFF_W + LRU_W
RET_COLS = 4 * RET_W
DIFF_COLS = 4 * DIFF_W
LRU_COLS = 2 * LRU_W
D_IN = RET_COLS + DIFF_COLS + LRU_COLS
ROPE_THETA = 10000.0
EPS = 1e-6

LANES = 128
SUBLANES = 8
VMEM_LIMIT = 52 * 1024 * 1024

ROPE_TM = 1024
INPROJ_TM = 256
INPROJ_TN = 512
RET_ROWS = 512
ATT_T = 512
LRU_ROWS = 512
OUT_TM = 512


def _silu(x):
    return x * jax.nn.sigmoid(x)


def _rope_table_kernel(pos_ref, c_ref, cos_r, sin_r, cos_d, sin_d):
    pos = pos_ref[...].astype(F32)
    ang_r = pos * c_ref[0:1, :]
    ang_d = pos * c_ref[1:2, :]
    cos_r[...] = jnp.cos(ang_r)
    sin_r[...] = jnp.sin(ang_r) * c_ref[2:3, :]
    cos_d[...] = jnp.cos(ang_d)
    sin_d[...] = jnp.sin(ang_d) * c_ref[3:4, :]


def _rope_tables(positions):
    t = positions.size
    half = HEAD_DIM // 2
    inv_r = 1.0 / (ROPE_THETA ** jnp.linspace(0.0, 1.0, half, dtype=F32))
    inv_d = 1.0 / (ROPE_THETA ** (jnp.arange(0, DIFF_QK, 2, dtype=F32) / DIFF_QK))
    one = jnp.ones((DIFF_QK // 2,), F32)
    consts = jnp.stack([
        jnp.concatenate([inv_r, inv_r]),
        jnp.concatenate([inv_d] * 4),
        jnp.concatenate([-jnp.ones((half,), F32), jnp.ones((half,), F32)]),
        jnp.concatenate([-one, one, -one, one]),
    ])
    consts = jnp.concatenate([consts, jnp.zeros((SUBLANES - 4, LANES), F32)])
    tab = jax.ShapeDtypeStruct((t, LANES), F32)
    row_spec = pl.BlockSpec((ROPE_TM, LANES), lambda i: (i, 0))
    return pl.pallas_call(
        _rope_table_kernel,
        grid=(t // ROPE_TM,),
        in_specs=[pl.BlockSpec((ROPE_TM, 1), lambda i: (i, 0)),
                  pl.BlockSpec((SUBLANES, LANES), lambda i: (0, 0))],
        out_specs=[row_spec] * 4,
        out_shape=[tab] * 4,
        name="rope_tables",
    )(positions.reshape(t, 1), consts)


def _rope_r(x, cos, sin):
    return x * cos + pltpu.roll(x, HEAD_DIM // 2, 1) * sin


def _rope_d(x, cos, sin, first_half):
    q = DIFF_QK // 2
    partner = jnp.where(first_half, pltpu.roll(x, LANES - q, 1), pltpu.roll(x, q, 1))
    return x * cos + partner * sin


def _inproj_kernel(x_ref, g_ref, w_ref, cr_ref, sr_ref, cd_ref, sd_ref,
                   ret_ref, diff_ref, lru_ref, h_scr):
    x = x_ref[...]
    ms = jnp.mean(x * x, axis=-1, keepdims=True)
    h_scr[...] = ((x * lax.rsqrt(ms + EPS)) * g_ref[...]).astype(BF16)
    lane = lax.broadcasted_iota(jnp.int32, (x.shape[0], LANES), 1)
    first_half = (lane % DIFF_QK) < (DIFF_QK // 2)
    for blk in range(D_IN // INPROJ_TN):
        c0 = blk * INPROJ_TN
        acc = jnp.dot(h_scr[...], w_ref[:, c0:c0 + INPROJ_TN], preferred_element_type=F32)
        for s in range(INPROJ_TN // LANES):
            c = c0 + s * LANES
            y = acc[:, s * LANES:(s + 1) * LANES]
            if c < 2 * RET_W:
                y = _rope_r(y, cr_ref[...], sr_ref[...])
                if c >= RET_W:
                    y = y * (HEAD_DIM ** -0.5)
            elif RET_COLS <= c < RET_COLS + 2 * DIFF_W:
                y = _rope_d(y, cd_ref[...], sd_ref[...], first_half)
                if c < RET_COLS + DIFF_W:
                    y = y * (DIFF_QK ** -0.5)
            y = y.astype(BF16)
            if c < RET_COLS:
                ret_ref[:, c:c + LANES] = y
            elif c < RET_COLS + DIFF_COLS:
                diff_ref[:, c - RET_COLS:c - RET_COLS + LANES] = y
            else:
                o = c - RET_COLS - DIFF_COLS
                lru_ref[:, o:o + LANES] = y


def _inproj(x2d, g, w_bf16, tabs):
    t = x2d.shape[0]
    tm = INPROJ_TM
    tab_spec = pl.BlockSpec((tm, LANES), lambda i: (i, 0))
    return pl.pallas_call(
        _inproj_kernel,
        grid=(t // tm,),
        in_specs=[pl.BlockSpec((tm, D_MODEL), lambda i: (i, 0)),
                  pl.BlockSpec((1, D_MODEL), lambda i: (0, 0)),
                  pl.BlockSpec((D_MODEL, D_IN), lambda i: (0, 0),
                               pipeline_mode=pl.Buffered(1)),
                  tab_spec, tab_spec, tab_spec, tab_spec],
        out_specs=[pl.BlockSpec((tm, RET_COLS), lambda i: (i, 0)),
                   pl.BlockSpec((tm, DIFF_COLS), lambda i: (i, 0)),
                   pl.BlockSpec((tm, LRU_COLS), lambda i: (i, 0))],
        out_shape=[jax.ShapeDtypeStruct((t, RET_COLS), BF16),
                   jax.ShapeDtypeStruct((t, DIFF_COLS), BF16),
                   jax.ShapeDtypeStruct((t, LRU_COLS), BF16)],
        scratch_shapes=[pltpu.VMEM((tm, D_MODEL), BF16)],
        compiler_params=pltpu.CompilerParams(
            dimension_semantics=("arbitrary",), vmem_limit_bytes=VMEM_LIMIT),
        name="inproj",
    )(x2d, g.reshape(1, D_MODEL), w_bf16, *tabs)


def _retention_tables():
    c = RET_CHUNK
    log_g = jnp.log1p(-jnp.power(2.0, -5.0 - jnp.arange(RET_HEADS, dtype=F32)))
    idx = jnp.arange(c, dtype=F32)
    rel = idx[:, None] - idx[None, :]
    intra = jnp.where(rel[None] >= 0,
                      jnp.exp(jnp.maximum(rel, 0.0)[None] * log_g[:, None, None]), 0.0)
    k_w = jnp.exp((c - 1.0 - idx)[None, :] * log_g[:, None])
    q_w = jnp.exp((idx + 1.0)[None, :] * log_g[:, None])
    cdec = jnp.exp(c * log_g)
    full = (RET_HEADS, c, HEAD_DIM)
    return (intra.astype(F32),
            jnp.broadcast_to(q_w[:, :, None], full),
            jnp.broadcast_to(k_w[:, :, None], full),
            jnp.broadcast_to(cdec[:, None, None], full))


def _retention_kernel(q_ref, k_ref, v_ref, g_ref, dec_ref, qw_ref, kw_ref, cd_ref,
                      o_ref, state_ref):
    @pl.when(pl.program_id(1) == 0)
    def _():
        state_ref[...] = jnp.zeros_like(state_ref)

    nt = (((1,), (1,)), ((), ()))
    tn = (((0,), (0,)), ((), ()))
    for c in range(RET_ROWS // RET_CHUNK):
        rows = slice(c * RET_CHUNK, (c + 1) * RET_CHUNK)
        for h in range(RET_HEADS):
            cols = slice(h * HEAD_DIM, (h + 1) * HEAD_DIM)
            q = q_ref[rows, cols]
            k = k_ref[rows, cols]
            v = v_ref[rows, cols]
            s = lax.dot_general(q, k, nt, preferred_element_type=F32) * dec_ref[h]
            o = jnp.dot(s.astype(BF16), v, preferred_element_type=F32)
            st = state_ref[h]
            qs = (q.astype(F32) * qw_ref[h]).astype(BF16)
            o = o + jnp.dot(qs, st.astype(BF16), preferred_element_type=F32)
            ks = (k.astype(F32) * kw_ref[h]).astype(BF16)
            kv = lax.dot_general(ks, v, tn, preferred_element_type=F32)
            state_ref[h] = st * cd_ref[h] + kv
            ms = jnp.mean(o * o, axis=-1, keepdims=True)
            y = o * lax.rsqrt(ms + EPS)
            o_ref[rows, cols] = (y * _silu(g_ref[rows, cols].astype(F32))).astype(BF16)


def _retention(ret_p, tables, batch, seq):
    t = batch * seq
    steps = seq // RET_ROWS

    def col(j):
        return pl.BlockSpec((RET_ROWS, RET_W), lambda b, i, j=j: (b * steps + i, j))

    tab_spec = pl.BlockSpec((RET_HEADS, RET_CHUNK, HEAD_DIM), lambda b, i: (0, 0, 0))
    return pl.pallas_call(
        _retention_kernel,
        grid=(batch, steps),
        in_specs=[col(0), col(1), col(2), col(3), tab_spec, tab_spec, tab_spec, tab_spec],
        out_specs=pl.BlockSpec((RET_ROWS, RET_W), lambda b, i: (b * steps + i, 0)),
        out_shape=jax.ShapeDtypeStruct((t, RET_W), BF16),
        scratch_shapes=[pltpu.VMEM((RET_HEADS, HEAD_DIM, HEAD_DIM), F32)],
        compiler_params=pltpu.CompilerParams(
            dimension_semantics=("arbitrary", "arbitrary"), vmem_limit_bytes=VMEM_LIMIT),
        name="retention",
    )(ret_p, ret_p, ret_p, ret_p, *tables)


def _diff_kernel(lam_ref, q_ref, k_ref, v_ref, g_ref, subg_ref, o_ref,
                 qm_scr, m_scr, l_scr, acc_scr, *, lam_init):
    qi = pl.program_id(2)
    q = q_ref[...]
    lane = lax.broadcasted_iota(jnp.int32, q.shape, 1)
    qm_scr[0] = jnp.where(lane < DIFF_QK, q, jnp.zeros_like(q))
    qm_scr[1] = jnp.where(lane >= DIFF_QK, q, jnp.zeros_like(q))
    m_scr[...] = jnp.full(m_scr.shape, -jnp.inf, F32)
    l_scr[...] = jnp.zeros_like(l_scr)
    acc_scr[...] = jnp.zeros_like(acc_scr)
    nt = (((1,), (1,)), ((), ()))

    def block(ki, masked):
        r0 = pl.multiple_of(ki * ATT_T, ATT_T)
        k = k_ref[pl.ds(r0, ATT_T), :]
        v = v_ref[pl.ds(r0, ATT_T), :]
        for c in range(2):
            s = lax.dot_general(qm_scr[c], k, nt, preferred_element_type=F32)
            if masked:
                row = lax.broadcasted_iota(jnp.int32, s.shape, 0)
                col = lax.broadcasted_iota(jnp.int32, s.shape, 1)
                s = jnp.where(col <= row, s, -jnp.inf)
            m_prev = m_scr[c]
            m_new = jnp.maximum(m_prev, jnp.max(s, axis=-1, keepdims=True))
            alpha = jnp.exp(m_prev - m_new)
            p = jnp.exp(s - m_new[:, 0:1])
            l_scr[c] = alpha * l_scr[c] + jnp.sum(p, axis=-1, keepdims=True)
            acc_scr[c] = alpha * acc_scr[c] + jnp.dot(
                p.astype(BF16), v, preferred_element_type=F32)
            m_scr[c] = m_new

    def body(ki, carry):
        block(ki, False)
        return carry

    lax.fori_loop(0, qi, body, 0)
    block(qi, True)

    lv = lam_ref[...]
    lam = (jnp.exp(jnp.sum(lv[0:1] * lv[1:2], axis=-1, keepdims=True))
           - jnp.exp(jnp.sum(lv[2:3] * lv[3:4], axis=-1, keepdims=True)) + lam_init)
    o = acc_scr[0] / l_scr[0] - lam * (acc_scr[1] / l_scr[1])
    ms = jnp.mean(o * o, axis=-1, keepdims=True)
    y = ((o * lax.rsqrt(ms + EPS)) * subg_ref[...]) * (1.0 - lam_init)
    o_ref[...] = (y * _silu(g_ref[...].astype(F32))).astype(BF16)


def _diff_attention(diff_p, lam_vecs, subg, lam_init, batch, seq):
    t = batch * seq
    nq = seq // ATT_T
    return pl.pallas_call(
        functools.partial(_diff_kernel, lam_init=lam_init),
        grid=(batch, DIFF_HEADS, nq),
        in_specs=[
            pl.BlockSpec((4, DIFF_QK), lambda b, h, i: (0, 0)),
            pl.BlockSpec((ATT_T, HEAD_DIM), lambda b, h, i: (b * nq + i, h)),
            pl.BlockSpec((seq, HEAD_DIM), lambda b, h, i: (b, DIFF_HEADS + h)),
            pl.BlockSpec((seq, HEAD_DIM), lambda b, h, i: (b, 2 * DIFF_HEADS + h)),
            pl.BlockSpec((ATT_T, HEAD_DIM), lambda b, h, i: (b * nq + i, 3 * DIFF_HEADS + h)),
            pl.BlockSpec((1, HEAD_DIM), lambda b, h, i: (0, 0)),
        ],
        out_specs=pl.BlockSpec((ATT_T, HEAD_DIM), lambda b, h, i: (b * nq + i, h)),
        out_shape=jax.ShapeDtypeStruct((t, DIFF_W), BF16),
        scratch_shapes=[pltpu.VMEM((2, ATT_T, HEAD_DIM), BF16),
                        pltpu.VMEM((2, ATT_T, LANES), F32),
                        pltpu.VMEM((2, ATT_T, LANES), F32),
                        pltpu.VMEM((2, ATT_T, HEAD_DIM), F32)],
        compiler_params=pltpu.CompilerParams(
            dimension_semantics=("arbitrary", "arbitrary", "arbitrary"),
            vmem_limit_bytes=VMEM_LIMIT),
        name="diff_attention",
    )(lam_vecs, diff_p, diff_p, diff_p, diff_p, subg.reshape(1, HEAD_DIM))


def _lru_kernel(x_ref, g_ref, cw_ref, cb_ref, w_ref, bias_ref, lam_ref, o_ref,
                xs_scr, a_scr, b_scr, h_scr):
    rows = LRU_ROWS
    pad = SUBLANES

    @pl.when(pl.program_id(1) == 0)
    def _():
        xs_scr[0:pad, :] = jnp.zeros((pad, LRU_W), F32)
        h_scr[...] = jnp.zeros_like(h_scr)

    @pl.when(pl.program_id(1) > 0)
    def _():
        xs_scr[0:pad, :] = xs_scr[rows:rows + pad, :]

    xs_scr[pad:pad + rows, :] = x_ref[...].astype(F32)
    xc = cb_ref[...] + xs_scr[pad - 3:pad - 3 + rows, :] * cw_ref[0:1, :]
    for j in range(1, CONV_W):
        xc = xc + xs_scr[pad - 3 + j:pad - 3 + j + rows, :] * cw_ref[j:j + 1, :]

    gates = jnp.dot(xc.astype(BF16), w_ref[...], preferred_element_type=F32) + bias_ref[...]
    r = jax.nn.sigmoid(gates[:, :LRU_W])
    i = jax.nn.sigmoid(gates[:, LRU_W:])
    lam = lam_ref[...]
    log_sig = jnp.minimum(lam, 0.0) - jnp.log1p(jnp.exp(-jnp.abs(lam)))
    log_a = (LRU_C * r) * log_sig
    a = jnp.exp(log_a)
    mult = jnp.sqrt(-jnp.tanh(log_a) * (a * a + 1.0))
    a_scr[...] = a
    b_scr[...] = mult * (i * xc)

    row = lax.broadcasted_iota(jnp.int32, (SUBLANES, LRU_W), 0)

    def body(t, carry):
        r0 = pl.multiple_of(t * SUBLANES, SUBLANES)
        av = a_scr[pl.ds(r0, SUBLANES), :]
        bv = b_scr[pl.ds(r0, SUBLANES), :]
        for d in (1, 2, 4):
            a_prev = jnp.where(row >= d, pltpu.roll(av, d, 0), 1.0)
            b_prev = jnp.where(row >= d, pltpu.roll(bv, d, 0), 0.0)
            bv = av * b_prev + bv
            av = av * a_prev
        hv = av * carry + bv
        b_scr[pl.ds(r0, SUBLANES), :] = hv
        return jnp.broadcast_to(hv[SUBLANES - 1:SUBLANES, :], (SUBLANES, LRU_W))

    h_scr[...] = lax.fori_loop(0, rows // SUBLANES, body, h_scr[...], unroll=4)
    o_ref[...] = (b_scr[...] * _silu(g_ref[...].astype(F32))).astype(BF16)


def _block_diag(w):
    n, k, _ = w.shape
    eye = jnp.eye(n, dtype=w.dtype)
    return (w[:, :, None, :] * eye[:, None, :, None]).reshape(n * k, n * k)


def _rg_lru(lru_p, conv_w, conv_b, wa, ba, wx, bx, lam, batch, seq):
    t = batch * seq
    steps = seq // LRU_ROWS
    w = jnp.concatenate([_block_diag(wa), _block_diag(wx)], axis=1).astype(BF16)
    bias = jnp.concatenate([ba, bx]).reshape(1, 2 * LRU_W)

    def const(shape):
        return pl.BlockSpec(shape, lambda b, i: (0, 0))

    return pl.pallas_call(
        _lru_kernel,
        grid=(batch, steps),
        in_specs=[pl.BlockSpec((LRU_ROWS, LRU_W), lambda b, i: (b * steps + i, 0)),
                  pl.BlockSpec((LRU_ROWS, LRU_W), lambda b, i: (b * steps + i, 1)),
                  const((CONV_W, LRU_W)), const((1, LRU_W)),
                  const((LRU_W, 2 * LRU_W)), const((1, 2 * LRU_W)), const((1, LRU_W))],
        out_specs=pl.BlockSpec((LRU_ROWS, LRU_W), lambda b, i: (b * steps + i, 0)),
        out_shape=jax.ShapeDtypeStruct((t, LRU_W), BF16),
        scratch_shapes=[pltpu.VMEM((LRU_ROWS + 2 * SUBLANES, LRU_W), F32),
                        pltpu.VMEM((LRU_ROWS, LRU_W), F32),
                        pltpu.VMEM((LRU_ROWS, LRU_W), F32),
                        pltpu.VMEM((SUBLANES, LRU_W), F32)],
        compiler_params=pltpu.CompilerParams(
            dimension_semantics=("arbitrary", "arbitrary"), vmem_limit_bytes=VMEM_LIMIT),
        name="rg_lru",
    )(lru_p, lru_p, conv_w, conv_b.reshape(1, LRU_W), w, bias, lam.reshape(1, LRU_W))


def _outproj_kernel(ret_ref, diff_ref, lru_ref, w_ref, g_ref, x_ref, o_ref):
    y = jnp.dot(ret_ref[...], w_ref[0:RET_W, :], preferred_element_type=F32)
    y = y + jnp.dot(diff_ref[...], w_ref[RET_W:RET_W + DIFF_W, :], preferred_element_type=F32)
    y = y + jnp.dot(lru_ref[...], w_ref[RET_W + DIFF_W:D_MIX, :], preferred_element_type=F32)
    ms = jnp.mean(y * y, axis=-1, keepdims=True)
    o_ref[...] = x_ref[...] + (y * lax.rsqrt(ms + EPS)) * g_ref[...]


def _outproj(ret_o, diff_o, lru_o, w_bf16, g, x2d):
    t = x2d.shape[0]
    tm = OUT_TM
    return pl.pallas_call(
        _outproj_kernel,
        grid=(t // tm,),
        in_specs=[pl.BlockSpec((tm, RET_W), lambda i: (i, 0)),
                  pl.BlockSpec((tm, DIFF_W), lambda i: (i, 0)),
                  pl.BlockSpec((tm, LRU_W), lambda i: (i, 0)),
                  pl.BlockSpec((D_MIX, D_MODEL), lambda i: (0, 0),
                               pipeline_mode=pl.Buffered(1)),
                  pl.BlockSpec((1, D_MODEL), lambda i: (0, 0)),
                  pl.BlockSpec((tm, D_MODEL), lambda i: (i, 0))],
        out_specs=pl.BlockSpec((tm, D_MODEL), lambda i: (i, 0)),
        out_shape=jax.ShapeDtypeStruct((t, D_MODEL), F32),
        compiler_params=pltpu.CompilerParams(
            dimension_semantics=("arbitrary",), vmem_limit_bytes=VMEM_LIMIT),
        name="outproj",
    )(ret_o, diff_o, lru_o, w_bf16, g.reshape(1, D_MODEL), x2d)


def kernel(x, positions, pre_norm_g, w_in, diff_lambda_q1, diff_lambda_k1, diff_lambda_q2,
           diff_lambda_k2, diff_subln_g, lru_conv_w, lru_conv_b, lru_wa, lru_ba, lru_wx,
           lru_bx, lru_lambda, w_out, post_norm_g):
    batch, seq, _ = x.shape
    depth = w_in.shape[0]
    assert seq % ATT_T == 0 and seq % RET_ROWS == 0 and seq % LRU_ROWS == 0
    x2d = x.reshape(batch * seq, D_MODEL)
    tabs = _rope_tables(positions)
    ret_tabs = _retention_tables()
    w_in_b = w_in.astype(BF16)
    w_out_b = w_out.astype(BF16)
    for l in range(depth):
        ret_p, diff_p, lru_p = _inproj(x2d, pre_norm_g[l], w_in_b[l], tabs)
        ret_o = _retention(ret_p, ret_tabs, batch, seq)
        lam_vecs = jnp.stack([diff_lambda_q1[l], diff_lambda_k1[l],
                              diff_lambda_q2[l], diff_lambda_k2[l]])
        lam_init = 0.8 - 0.6 * math.exp(-0.3 * l)
        diff_o = _diff_attention(diff_p, lam_vecs, diff_subln_g[l], lam_init, batch, seq)
        lru_o = _rg_lru(lru_p, lru_conv_w[l], lru_conv_b[l], lru_wa[l], lru_ba[l],
                        lru_wx[l], lru_bx[l], lru_lambda[l], batch, seq)
        x2d = _outproj(ret_o, diff_o, lru_o, w_out_b[l], post_norm_g[l], x2d)
    return x2d.reshape(batch, seq, D_MODEL)
```

```python
import functools
import math

import jax
import jax.numpy as jnp
from jax import lax
from jax.experimental import pallas as pl
from jax.experimental.pallas import tpu as pltpu

F32 = jnp.float32
BF16 = jnp.bfloat16

D_MODEL = 2048
HEAD_DIM = 128
RET_HEADS = 6
RET_W = RET_HEADS * HEAD_DIM
RET_CHUNK = 128
DIFF_HEADS = 4
DIFF_QK = HEAD_DIM // 2
DIFF_W = DIFF_HEADS * HEAD_DIM
LRU_W = 768
LRU_BLOCKS = 8
LRU_BW = LRU_W // LRU_BLOCKS
CONV_W = 4
LRU_C = 8.0
D_MIX = RET_W + DIFF_W + LRU_W
RET_COLS = 4 * RET_W
DIFF_COLS = 4 * DIFF_W
LRU_COLS = 2 * LRU_W
D_IN = RET_COLS + DIFF_COLS + LRU_COLS
ROPE_THETA = 10000.0
EPS = 1e-6
LOG2E = math.log2(math.e)

LANES = 128
SUBLANES = 8
VMEM_LIMIT = 52 * 1024 * 1024

ROPE_TM = 1024
INPROJ_TM = 256
INPROJ_TN = 512
RET_ROWS = 512
ATT_T = 512
ATT_TK = 2048
ATT_ROWS = 256
LRU_ROWS = 512
OUT_TM = 512


def _silu(x):
    return x * jax.nn.sigmoid(x)


def _rope_table_kernel(pos_ref, c_ref, cos_r, sin_r, cos_d, sin_d):
    pos = pos_ref[...].astype(F32)
    ang_r = pos * c_ref[0:1, :]
    ang_d = pos * c_ref[1:2, :]
    cos_r[...] = jnp.cos(ang_r)
    sin_r[...] = jnp.sin(ang_r) * c_ref[2:3, :]
    cos_d[...] = jnp.cos(ang_d)
    sin_d[...] = jnp.sin(ang_d) * c_ref[3:4, :]


def _rope_tables(positions):
    t = positions.size
    half = HEAD_DIM // 2
    inv_r = 1.0 / (ROPE_THETA ** jnp.linspace(0.0, 1.0, half, dtype=F32))
    inv_d = 1.0 / (ROPE_THETA ** (jnp.arange(0, DIFF_QK, 2, dtype=F32) / DIFF_QK))
    one = jnp.ones((DIFF_QK // 2,), F32)
    consts = jnp.stack([
        jnp.concatenate([inv_r, inv_r]),
        jnp.concatenate([inv_d] * 4),
        jnp.concatenate([-jnp.ones((half,), F32), jnp.ones((half,), F32)]),
        jnp.concatenate([-one, one, -one, one]),
    ])
    consts = jnp.concatenate([consts, jnp.zeros((SUBLANES - 4, LANES), F32)])
    tab = jax.ShapeDtypeStruct((t, LANES), F32)
    row_spec = pl.BlockSpec((ROPE_TM, LANES), lambda i: (i, 0))
    return pl.pallas_call(
        _rope_table_kernel,
        grid=(t // ROPE_TM,),
        in_specs=[pl.BlockSpec((ROPE_TM, 1), lambda i: (i, 0)),
                  pl.BlockSpec((SUBLANES, LANES), lambda i: (0, 0))],
        out_specs=[row_spec] * 4,
        out_shape=[tab] * 4,
        name="rope_tables",
    )(positions.reshape(t, 1), consts)


def _rope_r(x, cos, sin):
    return x * cos + pltpu.roll(x, HEAD_DIM // 2, 1) * sin


def _rope_d(x, cos, sin, first_half):
    q = DIFF_QK // 2
    partner = jnp.where(first_half, pltpu.roll(x, LANES - q, 1), pltpu.roll(x, q, 1))
    return x * cos + partner * sin


def _inproj_kernel(x_ref, g_ref, w_ref, cr_ref, sr_ref, cd_ref, sd_ref,
                   ret_ref, diff_ref, lru_ref, h_scr):
    x = x_ref[...]
    ms = jnp.mean(x * x, axis=-1, keepdims=True)
    h_scr[...] = ((x * lax.rsqrt(ms + EPS)) * g_ref[...]).astype(BF16)
    lane = lax.broadcasted_iota(jnp.int32, (x.shape[0], LANES), 1)
    first_half = (lane % DIFF_QK) < (DIFF_QK // 2)
    for blk in range(D_IN // INPROJ_TN):
        c0 = blk * INPROJ_TN
        acc = jnp.dot(h_scr[...], w_ref[:, c0:c0 + INPROJ_TN], preferred_element_type=F32)
        for s in range(INPROJ_TN // LANES):
            c = c0 + s * LANES
            y = acc[:, s * LANES:(s + 1) * LANES]
            if c < 2 * RET_W:
                y = _rope_r(y, cr_ref[...], sr_ref[...])
                if c >= RET_W:
                    y = y * (HEAD_DIM ** -0.5)
            elif RET_COLS <= c < RET_COLS + 2 * DIFF_W:
                y = _rope_d(y, cd_ref[...], sd_ref[...], first_half)
                if c < RET_COLS + DIFF_W:
                    y = y * (DIFF_QK ** -0.5 * LOG2E)
            y = y.astype(BF16)
            if c < RET_COLS:
                ret_ref[:, c:c + LANES] = y
            elif c < RET_COLS + DIFF_COLS:
                diff_ref[:, c - RET_COLS:c - RET_COLS + LANES] = y
            else:
                o = c - RET_COLS - DIFF_COLS
                lru_ref[:, o:o + LANES] = y


def _inproj(x2d, g, w_bf16, tabs):
    t = x2d.shape[0]
    tm = INPROJ_TM
    tab_spec = pl.BlockSpec((tm, LANES), lambda i: (i, 0))
    return pl.pallas_call(
        _inproj_kernel,
        grid=(t // tm,),
        in_specs=[pl.BlockSpec((tm, D_MODEL), lambda i: (i, 0)),
                  pl.BlockSpec((1, D_MODEL), lambda i: (0, 0)),
                  pl.BlockSpec((D_MODEL, D_IN), lambda i: (0, 0),
                               pipeline_mode=pl.Buffered(1)),
                  tab_spec, tab_spec, tab_spec, tab_spec],
        out_specs=[pl.BlockSpec((tm, RET_COLS), lambda i: (i, 0)),
                   pl.BlockSpec((tm, DIFF_COLS), lambda i: (i, 0)),
                   pl.BlockSpec((tm, LRU_COLS), lambda i: (i, 0))],
        out_shape=[jax.ShapeDtypeStruct((t, RET_COLS), BF16),
                   jax.ShapeDtypeStruct((t, DIFF_COLS), BF16),
                   jax.ShapeDtypeStruct((t, LRU_COLS), BF16)],
        scratch_shapes=[pltpu.VMEM((tm, D_MODEL), BF16)],
        compiler_params=pltpu.CompilerParams(
            dimension_semantics=("arbitrary",), vmem_limit_bytes=VMEM_LIMIT),
        name="inproj",
    )(x2d, g.reshape(1, D_MODEL), w_bf16, *tabs)


def _retention_tables():
    c = RET_CHUNK
    log_g = jnp.log1p(-jnp.power(2.0, -5.0 - jnp.arange(RET_HEADS, dtype=F32)))
    idx = jnp.arange(c, dtype=F32)
    rel = idx[:, None] - idx[None, :]
    intra = jnp.where(rel[None] >= 0,
                      jnp.exp(jnp.maximum(rel, 0.0)[None] * log_g[:, None, None]), 0.0)
    k_w = jnp.exp((c - 1.0 - idx)[None, :] * log_g[:, None])
    q_w = jnp.exp((idx + 1.0)[None, :] * log_g[:, None])
    cdec = jnp.exp(c * log_g)
    full = (RET_HEADS, c, HEAD_DIM)
    return (intra.astype(F32),
            jnp.broadcast_to(q_w[:, :, None], full),
            jnp.broadcast_to(k_w[:, :, None], full),
            jnp.broadcast_to(cdec[:, None, None], full))


def _retention_kernel(q_ref, k_ref, v_ref, g_ref, dec_ref, qw_ref, kw_ref, cd_ref,
                      o_ref, state_ref):
    @pl.when(pl.program_id(1) == 0)
    def _():
        state_ref[...] = jnp.zeros_like(state_ref)

    nt = (((1,), (1,)), ((), ()))
    tn = (((0,), (0,)), ((), ()))
    for c in range(RET_ROWS // RET_CHUNK):
        rows = slice(c * RET_CHUNK, (c + 1) * RET_CHUNK)
        for h in range(RET_HEADS):
            cols = slice(h * HEAD_DIM, (h + 1) * HEAD_DIM)
            q = q_ref[rows, cols]
            k = k_ref[rows, cols]
            v = v_ref[rows, cols]
            s = lax.dot_general(q, k, nt, preferred_element_type=F32) * dec_ref[h]
            o = jnp.dot(s.astype(BF16), v, preferred_element_type=F32)
            st = state_ref[h]
            qs = (q.astype(F32) * qw_ref[h]).astype(BF16)
            o = o + jnp.dot(qs, st.astype(BF16), preferred_element_type=F32)
            ks = (k.astype(F32) * kw_ref[h]).astype(BF16)
            kv = lax.dot_general(ks, v, tn, preferred_element_type=F32)
            state_ref[h] = st * cd_ref[h] + kv
            ms = jnp.mean(o * o, axis=-1, keepdims=True)
            y = o * lax.rsqrt(ms + EPS)
            o_ref[rows, cols] = (y * _silu(g_ref[rows, cols].astype(F32))).astype(BF16)


def _retention(ret_p, tables, batch, seq):
    t = batch * seq
    steps = seq // RET_ROWS

    def col(j):
        return pl.BlockSpec((RET_ROWS, RET_W), lambda b, i, j=j: (b * steps + i, j))

    tab_spec = pl.BlockSpec((RET_HEADS, RET_CHUNK, HEAD_DIM), lambda b, i: (0, 0, 0))
    return pl.pallas_call(
        _retention_kernel,
        grid=(batch, steps),
        in_specs=[col(0), col(1), col(2), col(3), tab_spec, tab_spec, tab_spec, tab_spec],
        out_specs=pl.BlockSpec((RET_ROWS, RET_W), lambda b, i: (b * steps + i, 0)),
        out_shape=jax.ShapeDtypeStruct((t, RET_W), BF16),
        scratch_shapes=[pltpu.VMEM((RET_HEADS, HEAD_DIM, HEAD_DIM), F32)],
        compiler_params=pltpu.CompilerParams(
            dimension_semantics=("arbitrary", "arbitrary"), vmem_limit_bytes=VMEM_LIMIT),
        name="retention",
    )(ret_p, ret_p, ret_p, ret_p, *tables)


def _diff_kernel(lam_ref, q_ref, k_ref, v_ref, g_ref, subg_ref, o_ref,
                 qm_scr, m_scr, l_scr, acc_scr, *, lam_init):
    qi = pl.program_id(2)
    q = q_ref[...]
    lane = lax.broadcasted_iota(jnp.int32, q.shape, 1)
    qm_scr[0:ATT_T, :] = jnp.where(lane < DIFF_QK, q, jnp.zeros_like(q))
    qm_scr[ATT_T:2 * ATT_T, :] = jnp.where(lane >= DIFF_QK, q, jnp.zeros_like(q))
    m_scr[...] = jnp.full(m_scr.shape, -jnp.inf, F32)
    l_scr[...] = jnp.zeros_like(l_scr)
    acc_scr[...] = jnp.zeros_like(acc_scr)
    nt = (((1,), (1,)), ((), ()))

    def lane_fold(x, op):
        parts = [x[:, i * LANES:(i + 1) * LANES] for i in range(x.shape[1] // LANES)]
        while len(parts) > 1:
            parts = [op(parts[i], parts[i + 1]) for i in range(0, len(parts), 2)]
        return parts[0]

    def block(r0, size, masked):
        k = k_ref[pl.ds(r0, size), :]
        v = v_ref[pl.ds(r0, size), :]
        chunks = [slice(i * ATT_ROWS, (i + 1) * ATT_ROWS) for i in range(2 * ATT_T // ATT_ROWS)]
        scores = [lax.dot_general(qm_scr[rows, :], k, nt, preferred_element_type=F32)
                  for rows in chunks]
        for rows, s in zip(chunks, scores):
            if masked:
                row = (lax.broadcasted_iota(jnp.int32, s.shape, 0) + rows.start) % ATT_T
                col = lax.broadcasted_iota(jnp.int32, s.shape, 1)
                s = jnp.where(col <= row, s, -jnp.inf)
            m_prev = m_scr[rows, :]
            m_cur = jnp.max(lane_fold(s, jnp.maximum), axis=-1, keepdims=True)
            m_new = jnp.maximum(m_prev, m_cur)
            alpha = jnp.exp2(m_prev - m_new)
            p = jnp.exp2(s - pltpu.repeat(m_new, size // LANES, 1))
            l_scr[rows, :] = alpha * l_scr[rows, :] + lane_fold(p, jnp.add)
            acc_scr[rows, :] = alpha * acc_scr[rows, :] + jnp.dot(
                p.astype(BF16), v, preferred_element_type=F32)
            m_scr[rows, :] = m_new

    def body(kk, carry):
        block(pl.multiple_of(kk * ATT_TK, ATT_TK), ATT_TK, False)
        return carry

    below = qi * ATT_T
    n_big = below // ATT_TK
    lax.fori_loop(0, n_big, body, 0)
    done = n_big * ATT_TK
    size = ATT_TK // 2
    while size >= ATT_T:
        take = below - done >= size

        @pl.when(take)
        def _(done=done, size=size):
            block(pl.multiple_of(done, ATT_T), size, False)

        done = done + jnp.where(take, size, 0)
        size //= 2
    block(pl.multiple_of(below, ATT_T), ATT_T, True)

    lv = lam_ref[...]
    lam = (jnp.exp(jnp.sum(lv[0:1] * lv[1:2], axis=-1, keepdims=True))
           - jnp.exp(jnp.sum(lv[2:3] * lv[3:4], axis=-1, keepdims=True)) + lam_init)
    l0 = jnp.sum(l_scr[0:ATT_T, :], axis=-1, keepdims=True)
    l1 = jnp.sum(l_scr[ATT_T:2 * ATT_T, :], axis=-1, keepdims=True)
    o = acc_scr[0:ATT_T, :] / l0 - lam * (acc_scr[ATT_T:2 * ATT_T, :] / l1)
    ms = jnp.mean(o * o, axis=-1, keepdims=True)
    y = ((o * lax.rsqrt(ms + EPS)) * subg_ref[...]) * (1.0 - lam_init)
    o_ref[...] = (y * _silu(g_ref[...].astype(F32))).astype(BF16)


def _diff_attention(diff_p, lam_vecs, subg, lam_init, batch, seq):
    t = batch * seq
    nq = seq // ATT_T
    return pl.pallas_call(
        functools.partial(_diff_kernel, lam_init=lam_init),
        grid=(batch, DIFF_HEADS, nq),
        in_specs=[
            pl.BlockSpec((4, DIFF_QK), lambda b, h, i: (0, 0)),
            pl.BlockSpec((ATT_T, HEAD_DIM), lambda b, h, i: (b * nq + i, h)),
            pl.BlockSpec((seq, HEAD_DIM), lambda b, h, i: (b, DIFF_HEADS + h)),
            pl.BlockSpec((seq, HEAD_DIM), lambda b, h, i: (b, 2 * DIFF_HEADS + h)),
            pl.BlockSpec((ATT_T, HEAD_DIM), lambda b, h, i: (b * nq + i, 3 * DIFF_HEADS + h)),
            pl.BlockSpec((1, HEAD_DIM), lambda b, h, i: (0, 0)),
        ],
        out_specs=pl.BlockSpec((ATT_T, HEAD_DIM), lambda b, h, i: (b * nq + i, h)),
        out_shape=jax.ShapeDtypeStruct((t, DIFF_W), BF16),
        scratch_shapes=[pltpu.VMEM((2 * ATT_T, HEAD_DIM), BF16),
                        pltpu.VMEM((2 * ATT_T, LANES), F32),
                        pltpu.VMEM((2 * ATT_T, LANES), F32),
                        pltpu.VMEM((2 * ATT_T, HEAD_DIM), F32)],
        compiler_params=pltpu.CompilerParams(
            dimension_semantics=("arbitrary", "arbitrary", "arbitrary"),
            vmem_limit_bytes=VMEM_LIMIT),
        name="diff_attention",
    )(lam_vecs, diff_p, diff_p, diff_p, diff_p, subg.reshape(1, HEAD_DIM))


def _lru_kernel(x_ref, g_ref, cw_ref, cb_ref, w_ref, bias_ref, lam_ref, o_ref,
                xs_scr, a_scr, b_scr, h_scr):
    rows = LRU_ROWS
    pad = SUBLANES

    @pl.when(pl.program_id(1) == 0)
    def _():
        xs_scr[0:pad, :] = jnp.zeros((pad, LRU_W), F32)
        h_scr[...] = jnp.zeros_like(h_scr)

    @pl.when(pl.program_id(1) > 0)
    def _():
        xs_scr[0:pad, :] = xs_scr[rows:rows + pad, :]

    xs_scr[pad:pad + rows, :] = x_ref[...].astype(F32)
    xc = cb_ref[...] + xs_scr[pad - 3:pad - 3 + rows, :] * cw_ref[0:1, :]
    for j in range(1, CONV_W):
        xc = xc + xs_scr[pad - 3 + j:pad - 3 + j + rows, :] * cw_ref[j:j + 1, :]

    gates = jnp.dot(xc.astype(BF16), w_ref[...], preferred_element_type=F32) + bias_ref[...]
    r = jax.nn.sigmoid(gates[:, :LRU_W])
    i = jax.nn.sigmoid(gates[:, LRU_W:])
    lam = lam_ref[...]
    log_sig = jnp.minimum(lam, 0.0) - jnp.log1p(jnp.exp(-jnp.abs(lam)))
    log_a = (LRU_C * r) * log_sig
    a = jnp.exp(log_a)
    mult = jnp.sqrt(-jnp.tanh(log_a) * (a * a + 1.0))
    a_scr[...] = a
    b_scr[...] = mult * (i * xc)

    row = lax.broadcasted_iota(jnp.int32, (SUBLANES, LRU_W), 0)

    def body(t, carry):
        r0 = pl.multiple_of(t * SUBLANES, SUBLANES)
        av = a_scr[pl.ds(r0, SUBLANES), :]
        bv = b_scr[pl.ds(r0, SUBLANES), :]
        for d in (1, 2, 4):
            a_prev = jnp.where(row >= d, pltpu.roll(av, d, 0), 1.0)
            b_prev = jnp.where(row >= d, pltpu.roll(bv, d, 0), 0.0)
            bv = av * b_prev + bv
            av = av * a_prev
        hv = av * carry + bv
        b_scr[pl.ds(r0, SUBLANES), :] = hv
        return jnp.broadcast_to(hv[SUBLANES - 1:SUBLANES, :], (SUBLANES, LRU_W))

    h_scr[...] = lax.fori_loop(0, rows // SUBLANES, body, h_scr[...], unroll=4)
    o_ref[...] = (b_scr[...] * _silu(g_ref[...].astype(F32))).astype(BF16)


def _block_diag(w):
    n, k, _ = w.shape
    eye = jnp.eye(n, dtype=w.dtype)
    return (w[:, :, None, :] * eye[:, None, :, None]).reshape(n * k, n * k)


def _rg_lru(lru_p, conv_w, conv_b, wa, ba, wx, bx, lam, batch, seq):
    t = batch * seq
    steps = seq // LRU_ROWS
    w = jnp.concatenate([_block_diag(wa), _block_diag(wx)], axis=1).astype(BF16)
    bias = jnp.concatenate([ba, bx]).reshape(1, 2 * LRU_W)

    def const(shape):
        return pl.BlockSpec(shape, lambda b, i: (0, 0))

    return pl.pallas_call(
        _lru_kernel,
        grid=(batch, steps),
        in_specs=[pl.BlockSpec((LRU_ROWS, LRU_W), lambda b, i: (b * steps + i, 0)),
                  pl.BlockSpec((LRU_ROWS, LRU_W), lambda b, i: (b * steps + i, 1)),
                  const((CONV_W, LRU_W)), const((1, LRU_W)),
                  const((LRU_W, 2 * LRU_W)), const((1, 2 * LRU_W)), const((1, LRU_W))],
        out_specs=pl.BlockSpec((LRU_ROWS, LRU_W), lambda b, i: (b * steps + i, 0)),
        out_shape=jax.ShapeDtypeStruct((t, LRU_W), BF16),
        scratch_shapes=[pltpu.VMEM((LRU_ROWS + 2 * SUBLANES, LRU_W), F32),
                        pltpu.VMEM((LRU_ROWS, LRU_W), F32),
                        pltpu.VMEM((LRU_ROWS, LRU_W), F32),
                        pltpu.VMEM((SUBLANES, LRU_W), F32)],
        compiler_params=pltpu.CompilerParams(
            dimension_semantics=("arbitrary", "arbitrary"), vmem_limit_bytes=VMEM_LIMIT),
        name="rg_lru",
    )(lru_p, lru_p, conv_w, conv_b.reshape(1, LRU_W), w, bias, lam.reshape(1, LRU_W))


def _outproj_kernel(ret_ref, diff_ref, lru_ref, w_ref, g_ref, x_ref, o_ref):
    y = jnp.dot(ret_ref[...], w_ref[0:RET_W, :], preferred_element_type=F32)
    y = y + jnp.dot(diff_ref[...], w_ref[RET_W:RET_W + DIFF_W, :], preferred_element_type=F32)
    y = y + jnp.dot(lru_ref[...], w_ref[RET_W + DIFF_W:D_MIX, :], preferred_element_type=F32)
    ms = jnp.mean(y * y, axis=-1, keepdims=True)
    o_ref[...] = x_ref[...] + (y * lax.rsqrt(ms + EPS)) * g_ref[...]


def _outproj(ret_o, diff_o, lru_o, w_bf16, g, x2d):
    t = x2d.shape[0]
    tm = OUT_TM
    return pl.pallas_call(
        _outproj_kernel,
        grid=(t // tm,),
        in_specs=[pl.BlockSpec((tm, RET_W), lambda i: (i, 0)),
                  pl.BlockSpec((tm, DIFF_W), lambda i: (i, 0)),
                  pl.BlockSpec((tm, LRU_W), lambda i: (i, 0)),
                  pl.BlockSpec((D_MIX, D_MODEL), lambda i: (0, 0),
                               pipeline_mode=pl.Buffered(1)),
                  pl.BlockSpec((1, D_MODEL), lambda i: (0, 0)),
                  pl.BlockSpec((tm, D_MODEL), lambda i: (i, 0))],
        out_specs=pl.BlockSpec((tm, D_MODEL), lambda i: (i, 0)),
        out_shape=jax.ShapeDtypeStruct((t, D_MODEL), F32),
        compiler_params=pltpu.CompilerParams(
            dimension_semantics=("arbitrary",), vmem_limit_bytes=VMEM_LIMIT),
        name="outproj",
    )(ret_o, diff_o, lru_o, w_bf16, g.reshape(1, D_MODEL), x2d)


def kernel(x, positions, pre_norm_g, w_in, diff_lambda_q1, diff_lambda_k1, diff_lambda_q2,
           diff_lambda_k2, diff_subln_g, lru_conv_w, lru_conv_b, lru_wa, lru_ba, lru_wx,
           lru_bx, lru_lambda, w_out, post_norm_g):
    batch, seq, _ = x.shape
    depth = w_in.shape[0]
    assert seq % ATT_T == 0 and seq % RET_ROWS == 0 and seq % LRU_ROWS == 0
    x2d = x.reshape(batch * seq, D_MODEL)
    tabs = _rope_tables(positions)
    ret_tabs = _retention_tables()
    w_in_b = w_in.astype(BF16)
    w_out_b = w_out.astype(BF16)
    for l in range(depth):
        ret_p, diff_p, lru_p = _inproj(x2d, pre_norm_g[l], w_in_b[l], tabs)
        ret_o = _retention(ret_p, ret_tabs, batch, seq)
        lam_vecs = jnp.stack([diff_lambda_q1[l], diff_lambda_k1[l],
                              diff_lambda_q2[l], diff_lambda_k2[l]])
        lam_init = 0.8 - 0.6 * math.exp(-0.3 * l)
        diff_o = _diff_attention(diff_p, lam_vecs, diff_subln_g[l], lam_init, batch, seq)
        lru_o = _rg_lru(lru_p, lru_conv_w[l], lru_conv_b[l], lru_wa[l], lru_ba[l],
                        lru_wx[l], lru_bx[l], lru_lambda[l], batch, seq)
        x2d = _outproj(ret_o, diff_o, lru_o, w_out_b[l], post_norm_g[l], x2d)
    return x2d.reshape(batch, seq, D_MODEL)
```

```python
import functools
import math

import jax
import jax.numpy as jnp
from jax import lax
from jax.experimental import pallas as pl
from jax.experimental.pallas import tpu as pltpu

F32 = jnp.float32
BF16 = jnp.bfloat16

D_MODEL = 2048
HEAD_DIM = 128
RET_HEADS = 6
RET_W = RET_HEADS * HEAD_DIM
RET_CHUNK = 128
DIFF_HEADS = 4
DIFF_QK = HEAD_DIM // 2
DIFF_W = DIFF_HEADS * HEAD_DIM
LRU_W = 768
LRU_BLOCKS = 8
LRU_BW = LRU_W // LRU_BLOCKS
CONV_W = 4
LRU_C = 8.0
D_MIX = RET_W + DIFF_W + LRU_W
RET_COLS = 4 * RET_W
DIFF_COLS = 4 * DIFF_W
LRU_COLS = 2 * LRU_W
D_IN = RET_COLS + DIFF_COLS + LRU_COLS
ROPE_THETA = 10000.0
EPS = 1e-6
LOG2E = math.log2(math.e)

LANES = 128
SUBLANES = 8
VMEM_LIMIT = 52 * 1024 * 1024

ROPE_TM = 1024
INPROJ_TM = 256
INPROJ_TN = 512
RET_ROWS = 512
ATT_T = 1024
ATT_TK = 2048
ATT_ROWS = 256
LRU_ROWS = 512
OUT_TM = 512


def _silu(x):
    return x * jax.nn.sigmoid(x)


def _rope_table_kernel(pos_ref, c_ref, cos_r, sin_r, cos_d, sin_d):
    pos = pos_ref[...].astype(F32)
    ang_r = pos * c_ref[0:1, :]
    ang_d = pos * c_ref[1:2, :]
    cos_r[...] = jnp.cos(ang_r)
    sin_r[...] = jnp.sin(ang_r) * c_ref[2:3, :]
    cos_d[...] = jnp.cos(ang_d)
    sin_d[...] = jnp.sin(ang_d) * c_ref[3:4, :]


def _rope_tables(positions):
    t = positions.size
    half = HEAD_DIM // 2
    inv_r = 1.0 / (ROPE_THETA ** jnp.linspace(0.0, 1.0, half, dtype=F32))
    inv_d = 1.0 / (ROPE_THETA ** (jnp.arange(0, DIFF_QK, 2, dtype=F32) / DIFF_QK))
    one = jnp.ones((DIFF_QK // 2,), F32)
    consts = jnp.stack([
        jnp.concatenate([inv_r, inv_r]),
        jnp.concatenate([inv_d] * 4),
        jnp.concatenate([-jnp.ones((half,), F32), jnp.ones((half,), F32)]),
        jnp.concatenate([-one, one, -one, one]),
    ])
    consts = jnp.concatenate([consts, jnp.zeros((SUBLANES - 4, LANES), F32)])
    tab = jax.ShapeDtypeStruct((t, LANES), F32)
    row_spec = pl.BlockSpec((ROPE_TM, LANES), lambda i: (i, 0))
    return pl.pallas_call(
        _rope_table_kernel,
        grid=(t // ROPE_TM,),
        in_specs=[pl.BlockSpec((ROPE_TM, 1), lambda i: (i, 0)),
                  pl.BlockSpec((SUBLANES, LANES), lambda i: (0, 0))],
        out_specs=[row_spec] * 4,
        out_shape=[tab] * 4,
        name="rope_tables",
    )(positions.reshape(t, 1), consts)


def _rope_r(x, cos, sin):
    return x * cos + pltpu.roll(x, HEAD_DIM // 2, 1) * sin


def _rope_d(x, cos, sin, first_half):
    q = DIFF_QK // 2
    partner = jnp.where(first_half, pltpu.roll(x, LANES - q, 1), pltpu.roll(x, q, 1))
    return x * cos + partner * sin


def _inproj_kernel(x_ref, g_ref, w_ref, cr_ref, sr_ref, cd_ref, sd_ref,
                   ret_ref, diff_ref, lru_ref, h_scr):
    x = x_ref[...]
    ms = jnp.mean(x * x, axis=-1, keepdims=True)
    h_scr[...] = ((x * lax.rsqrt(ms + EPS)) * g_ref[...]).astype(BF16)
    lane = lax.broadcasted_iota(jnp.int32, (x.shape[0], LANES), 1)
    first_half = (lane % DIFF_QK) < (DIFF_QK // 2)
    for blk in range(D_IN // INPROJ_TN):
        c0 = blk * INPROJ_TN
        acc = jnp.dot(h_scr[...], w_ref[:, c0:c0 + INPROJ_TN], preferred_element_type=F32)
        for s in range(INPROJ_TN // LANES):
            c = c0 + s * LANES
            y = acc[:, s * LANES:(s + 1) * LANES]
            if c < 2 * RET_W:
                y = _rope_r(y, cr_ref[...], sr_ref[...])
                if c >= RET_W:
                    y = y * (HEAD_DIM ** -0.5)
            elif RET_COLS <= c < RET_COLS + 2 * DIFF_W:
                y = _rope_d(y, cd_ref[...], sd_ref[...], first_half)
                if c < RET_COLS + DIFF_W:
                    y = y * (DIFF_QK ** -0.5 * LOG2E)
            y = y.astype(BF16)
            if c < RET_COLS:
                ret_ref[:, c:c + LANES] = y
            elif c < RET_COLS + DIFF_COLS:
                diff_ref[:, c - RET_COLS:c - RET_COLS + LANES] = y
            else:
                o = c - RET_COLS - DIFF_COLS
                lru_ref[:, o:o + LANES] = y


def _inproj(x2d, g, w_bf16, layer, tabs):
    t = x2d.shape[0]
    tm = INPROJ_TM
    tab_spec = pl.BlockSpec((tm, LANES), lambda i: (i, 0))
    return pl.pallas_call(
        _inproj_kernel,
        grid=(t // tm,),
        in_specs=[pl.BlockSpec((tm, D_MODEL), lambda i: (i, 0)),
                  pl.BlockSpec((1, D_MODEL), lambda i: (0, 0)),
                  pl.BlockSpec((None, D_MODEL, D_IN), lambda i: (layer, 0, 0),
                               pipeline_mode=pl.Buffered(1)),
                  tab_spec, tab_spec, tab_spec, tab_spec],
        out_specs=[pl.BlockSpec((tm, RET_COLS), lambda i: (i, 0)),
                   pl.BlockSpec((tm, DIFF_COLS), lambda i: (i, 0)),
                   pl.BlockSpec((tm, LRU_COLS), lambda i: (i, 0))],
        out_shape=[jax.ShapeDtypeStruct((t, RET_COLS), BF16),
                   jax.ShapeDtypeStruct((t, DIFF_COLS), BF16),
                   jax.ShapeDtypeStruct((t, LRU_COLS), BF16)],
        scratch_shapes=[pltpu.VMEM((tm, D_MODEL), BF16)],
        compiler_params=pltpu.CompilerParams(
            dimension_semantics=("arbitrary",), vmem_limit_bytes=VMEM_LIMIT),
        name="inproj",
    )(x2d, g.reshape(1, D_MODEL), w_bf16, *tabs)


def _retention_tables():
    c = RET_CHUNK
    log_g = jnp.log1p(-jnp.power(2.0, -5.0 - jnp.arange(RET_HEADS, dtype=F32)))
    idx = jnp.arange(c, dtype=F32)
    rel = idx[:, None] - idx[None, :]
    intra = jnp.where(rel[None] >= 0,
                      jnp.exp(jnp.maximum(rel, 0.0)[None] * log_g[:, None, None]), 0.0)
    k_w = jnp.exp((c - 1.0 - idx)[None, :] * log_g[:, None])
    q_w = jnp.exp((idx + 1.0)[None, :] * log_g[:, None])
    cdec = jnp.exp(c * log_g)
    full = (RET_HEADS, c, HEAD_DIM)
    return (intra.astype(F32),
            jnp.broadcast_to(q_w[:, :, None], full),
            jnp.broadcast_to(k_w[:, :, None], full),
            jnp.broadcast_to(cdec[:, None, None], full))


def _retention_kernel(q_ref, k_ref, v_ref, g_ref, dec_ref, qw_ref, kw_ref, cd_ref,
                      o_ref, state_ref):
    @pl.when(pl.program_id(1) == 0)
    def _():
        state_ref[...] = jnp.zeros_like(state_ref)

    nt = (((1,), (1,)), ((), ()))
    tn = (((0,), (0,)), ((), ()))
    for c in range(RET_ROWS // RET_CHUNK):
        rows = slice(c * RET_CHUNK, (c + 1) * RET_CHUNK)
        for h in range(RET_HEADS):
            cols = slice(h * HEAD_DIM, (h + 1) * HEAD_DIM)
            q = q_ref[rows, cols]
            k = k_ref[rows, cols]
            v = v_ref[rows, cols]
            s = lax.dot_general(q, k, nt, preferred_element_type=F32) * dec_ref[h]
            o = jnp.dot(s.astype(BF16), v, preferred_element_type=F32)
            st = state_ref[h]
            qs = (q.astype(F32) * qw_ref[h]).astype(BF16)
            o = o + jnp.dot(qs, st.astype(BF16), preferred_element_type=F32)
            ks = (k.astype(F32) * kw_ref[h]).astype(BF16)
            kv = lax.dot_general(ks, v, tn, preferred_element_type=F32)
            state_ref[h] = st * cd_ref[h] + kv
            ms = jnp.mean(o * o, axis=-1, keepdims=True)
            y = o * lax.rsqrt(ms + EPS)
            o_ref[rows, cols] = (y * _silu(g_ref[rows, cols].astype(F32))).astype(BF16)


def _retention(ret_p, tables, batch, seq):
    t = batch * seq
    steps = seq // RET_ROWS

    def col(j):
        return pl.BlockSpec((RET_ROWS, RET_W), lambda b, i, j=j: (b * steps + i, j))

    tab_spec = pl.BlockSpec((RET_HEADS, RET_CHUNK, HEAD_DIM), lambda b, i: (0, 0, 0))
    return pl.pallas_call(
        _retention_kernel,
        grid=(batch, steps),
        in_specs=[col(0), col(1), col(2), col(3), tab_spec, tab_spec, tab_spec, tab_spec],
        out_specs=pl.BlockSpec((RET_ROWS, RET_W), lambda b, i: (b * steps + i, 0)),
        out_shape=jax.ShapeDtypeStruct((t, RET_W), BF16),
        scratch_shapes=[pltpu.VMEM((RET_HEADS, HEAD_DIM, HEAD_DIM), F32)],
        compiler_params=pltpu.CompilerParams(
            dimension_semantics=("arbitrary", "arbitrary"), vmem_limit_bytes=VMEM_LIMIT),
        name="retention",
    )(ret_p, ret_p, ret_p, ret_p, *tables)


def _diff_kernel(lam_ref, q_ref, k_ref, v_ref, g_ref, subg_ref, o_ref,
                 qm_scr, m_scr, l_scr, acc_scr, *, lam_init):
    qi = pl.program_id(2)
    q = q_ref[...]
    lane = lax.broadcasted_iota(jnp.int32, q.shape, 1)
    qm_scr[0:ATT_T, :] = jnp.where(lane < DIFF_QK, q, jnp.zeros_like(q))
    qm_scr[ATT_T:2 * ATT_T, :] = jnp.where(lane >= DIFF_QK, q, jnp.zeros_like(q))
    m_scr[...] = jnp.full(m_scr.shape, -jnp.inf, F32)
    l_scr[...] = jnp.zeros_like(l_scr)
    acc_scr[...] = jnp.zeros_like(acc_scr)
    nt = (((1,), (1,)), ((), ()))

    def lane_fold(x, op):
        parts = [x[:, i * LANES:(i + 1) * LANES] for i in range(x.shape[1] // LANES)]
        while len(parts) > 1:
            nxt = [op(parts[i], parts[i + 1]) for i in range(0, len(parts) - 1, 2)]
            parts = nxt + parts[len(parts) - len(parts) % 2:]
        return parts[0]

    def block(r0, size, diagonal):
        k = k_ref[pl.ds(r0, size), :]
        v = v_ref[pl.ds(r0, size), :]
        chunks = []
        for i in range(2 * ATT_T // ATT_ROWS):
            rows = slice(i * ATT_ROWS, (i + 1) * ATT_ROWS)
            keys = (i * ATT_ROWS) % ATT_T + ATT_ROWS if diagonal else size
            s = lax.dot_general(qm_scr[rows, :], k[:keys, :], nt, preferred_element_type=F32)
            chunks.append((rows, keys, s))
        for rows, keys, s in chunks:
            if diagonal:
                tail = s[:, keys - ATT_ROWS:]
                row = lax.broadcasted_iota(jnp.int32, tail.shape, 0)
                col = lax.broadcasted_iota(jnp.int32, tail.shape, 1)
                tail = jnp.where(col <= row, tail, -jnp.inf)
                s = tail if keys == ATT_ROWS else jnp.concatenate(
                    [s[:, :keys - ATT_ROWS], tail], axis=1)
            m_prev = m_scr[rows, :]
            m_cur = jnp.max(lane_fold(s, jnp.maximum), axis=-1, keepdims=True)
            m_new = jnp.maximum(m_prev, m_cur)
            alpha = jnp.exp2(m_prev - m_new)
            p = jnp.exp2(s - jnp.tile(m_new, (1, keys // LANES)))
            l_scr[rows, :] = alpha * l_scr[rows, :] + lane_fold(p, jnp.add)
            acc_scr[rows, :] = alpha * acc_scr[rows, :] + jnp.dot(
                p.astype(BF16), v[:keys, :], preferred_element_type=F32)
            m_scr[rows, :] = m_new

    def body(kk, carry):
        block(pl.multiple_of(kk * ATT_TK, ATT_TK), ATT_TK, False)
        return carry

    below = qi * ATT_T
    n_big = below // ATT_TK
    lax.fori_loop(0, n_big, body, 0)
    done = n_big * ATT_TK
    size = ATT_TK // 2
    while size >= ATT_T:
        take = below - done >= size

        @pl.when(take)
        def _(done=done, size=size):
            block(pl.multiple_of(done, ATT_T), size, False)

        done = done + jnp.where(take, size, 0)
        size //= 2
    block(pl.multiple_of(below, ATT_T), ATT_T, True)

    lv = lam_ref[...]
    lam = (jnp.exp(jnp.sum(lv[0:1] * lv[1:2], axis=-1, keepdims=True))
           - jnp.exp(jnp.sum(lv[2:3] * lv[3:4], axis=-1, keepdims=True)) + lam_init)
    l0 = jnp.sum(l_scr[0:ATT_T, :], axis=-1, keepdims=True)
    l1 = jnp.sum(l_scr[ATT_T:2 * ATT_T, :], axis=-1, keepdims=True)
    o = acc_scr[0:ATT_T, :] / l0 - lam * (acc_scr[ATT_T:2 * ATT_T, :] / l1)
    ms = jnp.mean(o * o, axis=-1, keepdims=True)
    y = ((o * lax.rsqrt(ms + EPS)) * subg_ref[...]) * (1.0 - lam_init)
    o_ref[...] = (y * _silu(g_ref[...].astype(F32))).astype(BF16)


def _diff_attention(diff_p, lam_vecs, subg, lam_init, batch, seq):
    t = batch * seq
    nq = seq // ATT_T
    return pl.pallas_call(
        functools.partial(_diff_kernel, lam_init=lam_init),
        grid=(batch, DIFF_HEADS, nq),
        in_specs=[
            pl.BlockSpec((4, DIFF_QK), lambda b, h, i: (0, 0)),
            pl.BlockSpec((ATT_T, HEAD_DIM), lambda b, h, i: (b * nq + i, h)),
            pl.BlockSpec((seq, HEAD_DIM), lambda b, h, i: (b, DIFF_HEADS + h)),
            pl.BlockSpec((seq, HEAD_DIM), lambda b, h, i: (b, 2 * DIFF_HEADS + h)),
            pl.BlockSpec((ATT_T, HEAD_DIM), lambda b, h, i: (b * nq + i, 3 * DIFF_HEADS + h)),
            pl.BlockSpec((1, HEAD_DIM), lambda b, h, i: (0, 0)),
        ],
        out_specs=pl.BlockSpec((ATT_T, HEAD_DIM), lambda b, h, i: (b * nq + i, h)),
        out_shape=jax.ShapeDtypeStruct((t, DIFF_W), BF16),
        scratch_shapes=[pltpu.VMEM((2 * ATT_T, HEAD_DIM), BF16),
                        pltpu.VMEM((2 * ATT_T, LANES), F32),
                        pltpu.VMEM((2 * ATT_T, LANES), F32),
                        pltpu.VMEM((2 * ATT_T, HEAD_DIM), F32)],
        compiler_params=pltpu.CompilerParams(
            dimension_semantics=("arbitrary", "arbitrary", "arbitrary"),
            vmem_limit_bytes=VMEM_LIMIT),
        name="diff_attention",
    )(lam_vecs, diff_p, diff_p, diff_p, diff_p, subg.reshape(1, HEAD_DIM))


def _lru_kernel(x_ref, g_ref, cw_ref, cb_ref, w_ref, bias_ref, lam_ref, o_ref,
                xs_scr, a_scr, b_scr, h_scr):
    rows = LRU_ROWS
    pad = SUBLANES

    @pl.when(pl.program_id(1) == 0)
    def _():
        xs_scr[0:pad, :] = jnp.zeros((pad, LRU_W), F32)
        h_scr[...] = jnp.zeros_like(h_scr)

    @pl.when(pl.program_id(1) > 0)
    def _():
        xs_scr[0:pad, :] = xs_scr[rows:rows + pad, :]

    xs_scr[pad:pad + rows, :] = x_ref[...].astype(F32)
    xc = cb_ref[...] + xs_scr[pad - 3:pad - 3 + rows, :] * cw_ref[0:1, :]
    for j in range(1, CONV_W):
        xc = xc + xs_scr[pad - 3 + j:pad - 3 + j + rows, :] * cw_ref[j:j + 1, :]

    gates = jnp.dot(xc.astype(BF16), w_ref[...], preferred_element_type=F32) + bias_ref[...]
    r = jax.nn.sigmoid(gates[:, :LRU_W])
    i = jax.nn.sigmoid(gates[:, LRU_W:])
    lam = lam_ref[...]
    log_sig = jnp.minimum(lam, 0.0) - jnp.log1p(jnp.exp(-jnp.abs(lam)))
    log_a = (LRU_C * r) * log_sig
    a = jnp.exp(log_a)
    mult = jnp.sqrt(-jnp.tanh(log_a) * (a * a + 1.0))
    a_scr[...] = a
    b_scr[...] = mult * (i * xc)

    row = lax.broadcasted_iota(jnp.int32, (SUBLANES, LRU_W), 0)

    def body(t, carry):
        r0 = pl.multiple_of(t * SUBLANES, SUBLANES)
        av = a_scr[pl.ds(r0, SUBLANES), :]
        bv = b_scr[pl.ds(r0, SUBLANES), :]
        for d in (1, 2, 4):
            a_prev = jnp.where(row >= d, pltpu.roll(av, d, 0), 1.0)
            b_prev = jnp.where(row >= d, pltpu.roll(bv, d, 0), 0.0)
            bv = av * b_prev + bv
            av = av * a_prev
        hv = av * carry + bv
        b_scr[pl.ds(r0, SUBLANES), :] = hv
        return jnp.broadcast_to(hv[SUBLANES - 1:SUBLANES, :], (SUBLANES, LRU_W))

    h_scr[...] = lax.fori_loop(0, rows // SUBLANES, body, h_scr[...], unroll=4)
    o_ref[...] = (b_scr[...] * _silu(g_ref[...].astype(F32))).astype(BF16)


def _block_diag(w):
    n, k, _ = w.shape
    eye = jnp.eye(n, dtype=w.dtype)
    return (w[:, :, None, :] * eye[:, None, :, None]).reshape(n * k, n * k)


def _rg_lru(lru_p, conv_w, conv_b, wa, ba, wx, bx, lam, batch, seq):
    t = batch * seq
    steps = seq // LRU_ROWS
    w = jnp.concatenate([_block_diag(wa), _block_diag(wx)], axis=1).astype(BF16)
    bias = jnp.concatenate([ba, bx]).reshape(1, 2 * LRU_W)

    def const(shape):
        return pl.BlockSpec(shape, lambda b, i: (0, 0))

    return pl.pallas_call(
        _lru_kernel,
        grid=(batch, steps),
        in_specs=[pl.BlockSpec((LRU_ROWS, LRU_W), lambda b, i: (b * steps + i, 0)),
                  pl.BlockSpec((LRU_ROWS, LRU_W), lambda b, i: (b * steps + i, 1)),
                  const((CONV_W, LRU_W)), const((1, LRU_W)),
                  const((LRU_W, 2 * LRU_W)), const((1, 2 * LRU_W)), const((1, LRU_W))],
        out_specs=pl.BlockSpec((LRU_ROWS, LRU_W), lambda b, i: (b * steps + i, 0)),
        out_shape=jax.ShapeDtypeStruct((t, LRU_W), BF16),
        scratch_shapes=[pltpu.VMEM((LRU_ROWS + 2 * SUBLANES, LRU_W), F32),
                        pltpu.VMEM((LRU_ROWS, LRU_W), F32),
                        pltpu.VMEM((LRU_ROWS, LRU_W), F32),
                        pltpu.VMEM((SUBLANES, LRU_W), F32)],
        compiler_params=pltpu.CompilerParams(
            dimension_semantics=("arbitrary", "arbitrary"), vmem_limit_bytes=VMEM_LIMIT),
        name="rg_lru",
    )(lru_p, lru_p, conv_w, conv_b.reshape(1, LRU_W), w, bias, lam.reshape(1, LRU_W))


def _outproj_kernel(ret_ref, diff_ref, lru_ref, w_ref, g_ref, x_ref, o_ref):
    y = jnp.dot(ret_ref[...], w_ref[0:RET_W, :], preferred_element_type=F32)
    y = y + jnp.dot(diff_ref[...], w_ref[RET_W:RET_W + DIFF_W, :], preferred_element_type=F32)
    y = y + jnp.dot(lru_ref[...], w_ref[RET_W + DIFF_W:D_MIX, :], preferred_element_type=F32)
    ms = jnp.mean(y * y, axis=-1, keepdims=True)
    o_ref[...] = x_ref[...] + (y * lax.rsqrt(ms + EPS)) * g_ref[...]


def _outproj(ret_o, diff_o, lru_o, w_bf16, layer, g, x2d):
    t = x2d.shape[0]
    tm = OUT_TM
    return pl.pallas_call(
        _outproj_kernel,
        grid=(t // tm,),
        in_specs=[pl.BlockSpec((tm, RET_W), lambda i: (i, 0)),
                  pl.BlockSpec((tm, DIFF_W), lambda i: (i, 0)),
                  pl.BlockSpec((tm, LRU_W), lambda i: (i, 0)),
                  pl.BlockSpec((None, D_MIX, D_MODEL), lambda i: (layer, 0, 0),
                               pipeline_mode=pl.Buffered(1)),
                  pl.BlockSpec((1, D_MODEL), lambda i: (0, 0)),
                  pl.BlockSpec((tm, D_MODEL), lambda i: (i, 0))],
        out_specs=pl.BlockSpec((tm, D_MODEL), lambda i: (i, 0)),
        out_shape=jax.ShapeDtypeStruct((t, D_MODEL), F32),
        compiler_params=pltpu.CompilerParams(
            dimension_semantics=("arbitrary",), vmem_limit_bytes=VMEM_LIMIT),
        name="outproj",
    )(ret_o, diff_o, lru_o, w_bf16, g.reshape(1, D_MODEL), x2d)


def kernel(x, positions, pre_norm_g, w_in, diff_lambda_q1, diff_lambda_k1, diff_lambda_q2,
           diff_lambda_k2, diff_subln_g, lru_conv_w, lru_conv_b, lru_wa, lru_ba, lru_wx,
           lru_bx, lru_lambda, w_out, post_norm_g):
    batch, seq, _ = x.shape
    depth = w_in.shape[0]
    assert seq % ATT_T == 0 and seq % RET_ROWS == 0 and seq % LRU_ROWS == 0
    x2d = x.reshape(batch * seq, D_MODEL)
    tabs = _rope_tables(positions)
    ret_tabs = _retention_tables()
    w_in_b = w_in.astype(BF16)
    w_out_b = w_out.astype(BF16)
    for l in range(depth):
        ret_p, diff_p, lru_p = _inproj(x2d, pre_norm_g[l], w_in_b, l, tabs)
        ret_o = _retention(ret_p, ret_tabs, batch, seq)
        lam_vecs = jnp.stack([diff_lambda_q1[l], diff_lambda_k1[l],
                              diff_lambda_q2[l], diff_lambda_k2[l]])
        lam_init = 0.8 - 0.6 * math.exp(-0.3 * l)
        diff_o = _diff_attention(diff_p, lam_vecs, diff_subln_g[l], lam_init, batch, seq)
        lru_o = _rg_lru(lru_p, lru_conv_w[l], lru_conv_b[l], lru_wa[l], lru_ba[l],
                        lru_wx[l], lru_bx[l], lru_lambda[l], batch, seq)
        x2d = _outproj(ret_o, diff_o, lru_o, w_out_b, l, post_norm_g[l], x2d)
    return x2d.reshape(batch, seq, D_MODEL)
```

```python
import functools
import math

import jax
import jax.numpy as jnp
from jax import lax
from jax.experimental import pallas as pl
from jax.experimental.pallas import tpu as pltpu

F32 = jnp.float32
BF16 = jnp.bfloat16

D_MODEL = 2048
HEAD_DIM = 128
RET_HEADS = 6
RET_W = RET_HEADS * HEAD_DIM
RET_CHUNK = 128
DIFF_HEADS = 4
DIFF_QK = HEAD_DIM // 2
DIFF_W = DIFF_HEADS * HEAD_DIM
LRU_W = 768
LRU_BLOCKS = 8
LRU_BW = LRU_W // LRU_BLOCKS
CONV_W = 4
LRU_C = 8.0
D_MIX = RET_W + DIFF_W + LRU_W
RET_COLS = 4 * RET_W
DIFF_COLS = 4 * DIFF_W
LRU_COLS = 2 * LRU_W
D_IN = RET_COLS + DIFF_COLS + LRU_COLS
ROPE_THETA = 10000.0
EPS = 1e-6
LOG2E = math.log2(math.e)

LANES = 128
SUBLANES = 8
VMEM_LIMIT = 52 * 1024 * 1024

ROPE_TM = 1024
INPROJ_TM = 256
INPROJ_TN = 512
RET_ROWS = 512
ATT_T = 1024
ATT_TK = 2048
ATT_ROWS = 256
LRU_GROUPS = 4
OUT_TM = 512


def _silu(x):
    return x * jax.nn.sigmoid(x)


def _rope_table_kernel(pos_ref, c_ref, cos_r, sin_r, cos_d, sin_d):
    pos = pos_ref[...].astype(F32)
    ang_r = pos * c_ref[0:1, :]
    ang_d = pos * c_ref[1:2, :]
    cos_r[...] = jnp.cos(ang_r)
    sin_r[...] = jnp.sin(ang_r) * c_ref[2:3, :]
    cos_d[...] = jnp.cos(ang_d)
    sin_d[...] = jnp.sin(ang_d) * c_ref[3:4, :]


def _rope_tables(positions):
    t = positions.size
    half = HEAD_DIM // 2
    inv_r = 1.0 / (ROPE_THETA ** jnp.linspace(0.0, 1.0, half, dtype=F32))
    inv_d = 1.0 / (ROPE_THETA ** (jnp.arange(0, DIFF_QK, 2, dtype=F32) / DIFF_QK))
    one = jnp.ones((DIFF_QK // 2,), F32)
    consts = jnp.stack([
        jnp.concatenate([inv_r, inv_r]),
        jnp.concatenate([inv_d] * 4),
        jnp.concatenate([-jnp.ones((half,), F32), jnp.ones((half,), F32)]),
        jnp.concatenate([-one, one, -one, one]),
    ])
    consts = jnp.concatenate([consts, jnp.zeros((SUBLANES - 4, LANES), F32)])
    tab = jax.ShapeDtypeStruct((t, LANES), F32)
    row_spec = pl.BlockSpec((ROPE_TM, LANES), lambda i: (i, 0))
    return pl.pallas_call(
        _rope_table_kernel,
        grid=(t // ROPE_TM,),
        in_specs=[pl.BlockSpec((ROPE_TM, 1), lambda i: (i, 0)),
                  pl.BlockSpec((SUBLANES, LANES), lambda i: (0, 0))],
        out_specs=[row_spec] * 4,
        out_shape=[tab] * 4,
        name="rope_tables",
    )(positions.reshape(t, 1), consts)


def _rope_r(x, cos, sin):
    return x * cos + pltpu.roll(x, HEAD_DIM // 2, 1) * sin


def _rope_d(x, cos, sin, first_half):
    q = DIFF_QK // 2
    partner = jnp.where(first_half, pltpu.roll(x, LANES - q, 1), pltpu.roll(x, q, 1))
    return x * cos + partner * sin


def _inproj_kernel(x_ref, g_ref, w_ref, cr_ref, sr_ref, cd_ref, sd_ref,
                   cw_ref, cb_ref, gw_ref, gb_ref, lam_ref,
                   ret_ref, diff_ref, lru_ref,
                   xn_scr, xs_scr, hc_scr, hs_scr, *, steps_per_seq):
    rows = x_ref.shape[0]
    pad = SUBLANES
    seq_start = pl.program_id(0) % steps_per_seq == 0

    @pl.when(seq_start)
    def _():
        xs_scr[0:pad, :] = jnp.zeros((pad, LRU_W), F32)
        hc_scr[...] = jnp.zeros_like(hc_scr)

    @pl.when(jnp.logical_not(seq_start))
    def _():
        xs_scr[0:pad, :] = xs_scr[rows:rows + pad, :]

    x = x_ref[...]
    ms = jnp.mean(x * x, axis=-1, keepdims=True)
    xn_scr[...] = ((x * lax.rsqrt(ms + EPS)) * g_ref[...]).astype(BF16)
    lane = lax.broadcasted_iota(jnp.int32, (rows, LANES), 1)
    first_half = (lane % DIFF_QK) < (DIFF_QK // 2)

    def proj(c0, width):
        return jnp.dot(xn_scr[...], w_ref[:, c0:c0 + width], preferred_element_type=F32)

    def emit(blk):
        c0 = blk * INPROJ_TN
        acc = proj(c0, INPROJ_TN)
        for s in range(INPROJ_TN // LANES):
            c = c0 + s * LANES
            y = acc[:, s * LANES:(s + 1) * LANES]
            if c < 2 * RET_W:
                y = _rope_r(y, cr_ref[...], sr_ref[...])
                if c >= RET_W:
                    y = y * (HEAD_DIM ** -0.5)
            elif RET_COLS <= c < RET_COLS + 2 * DIFF_W:
                y = _rope_d(y, cd_ref[...], sd_ref[...], first_half)
                if c < RET_COLS + DIFF_W:
                    y = y * (DIFF_QK ** -0.5 * LOG2E)
            y = y.astype(BF16)
            if c < RET_COLS:
                ret_ref[:, c:c + LANES] = y
            else:
                diff_ref[:, c - RET_COLS:c - RET_COLS + LANES] = y

    lru0 = RET_COLS + DIFF_COLS
    xs_scr[pad:pad + rows, :] = proj(lru0, LRU_W)
    lg = proj(lru0 + LRU_W, LRU_W)
    xc = cb_ref[...] + xs_scr[pad - 3:pad - 3 + rows, :] * cw_ref[0:1, :]
    for j in range(1, CONV_W):
        xc = xc + xs_scr[pad - 3 + j:pad - 3 + j + rows, :] * cw_ref[j:j + 1, :]

    emit(0)
    emit(1)

    gates = jnp.dot(xc.astype(BF16), gw_ref[...], preferred_element_type=F32) + gb_ref[...]
    lam = lam_ref[...]
    log_sig = jnp.minimum(lam, 0.0) - jnp.log1p(jnp.exp(-jnp.abs(lam)))
    row = lax.broadcasted_iota(jnp.int32, (SUBLANES, LRU_W), 0)

    def lru_rows(rs, carry):
        r = jax.nn.sigmoid(gates[rs, :LRU_W])
        i = jax.nn.sigmoid(gates[rs, LRU_W:])
        log_a = (LRU_C * r) * log_sig
        a = jnp.exp(log_a)
        mult = jnp.sqrt(-jnp.tanh(log_a) * (a * a + 1.0))
        b = mult * (i * xc[rs, :])
        for t in range((rs.stop - rs.start) // SUBLANES):
            av = a[t * SUBLANES:(t + 1) * SUBLANES, :]
            bv = b[t * SUBLANES:(t + 1) * SUBLANES, :]
            for d in (1, 2, 4):
                a_prev = jnp.where(row >= d, pltpu.roll(av, d, 0), 1.0)
                b_prev = jnp.where(row >= d, pltpu.roll(bv, d, 0), 0.0)
                bv = av * b_prev + bv
                av = av * a_prev
            hv = av * carry + bv
            r0 = rs.start + t * SUBLANES
            hs_scr[r0:r0 + SUBLANES, :] = hv
            carry = jnp.broadcast_to(hv[SUBLANES - 1:SUBLANES, :], (SUBLANES, LRU_W))
        lru_ref[rs, :] = (hs_scr[rs, :] * _silu(lg[rs, :])).astype(BF16)
        return carry

    carry = hc_scr[...]
    blk = 2
    for gi in range(LRU_GROUPS):
        emit(blk)
        blk += 1
        carry = lru_rows(slice(gi * rows // LRU_GROUPS, (gi + 1) * rows // LRU_GROUPS), carry)
    hc_scr[...] = carry
    while blk < lru0 // INPROJ_TN:
        emit(blk)
        blk += 1


def _block_diag(w):
    n, k, _ = w.shape
    eye = jnp.eye(n, dtype=w.dtype)
    return (w[:, :, None, :] * eye[:, None, :, None]).reshape(n * k, n * k)


def _inproj(x2d, g, w_bf16, layer, tabs, conv_w, conv_b, wa, ba, wx, bx, lam, seq):
    t = x2d.shape[0]
    tm = INPROJ_TM
    gate_w = jnp.concatenate([_block_diag(wa), _block_diag(wx)], axis=1).astype(BF16)
    gate_b = jnp.concatenate([ba, bx]).reshape(1, 2 * LRU_W)
    tab_spec = pl.BlockSpec((tm, LANES), lambda i: (i, 0))

    def const(shape):
        return pl.BlockSpec(shape, lambda i: (0, 0))

    return pl.pallas_call(
        functools.partial(_inproj_kernel, steps_per_seq=seq // tm),
        grid=(t // tm,),
        in_specs=[pl.BlockSpec((tm, D_MODEL), lambda i: (i, 0)),
                  const((1, D_MODEL)),
                  pl.BlockSpec((None, D_MODEL, D_IN), lambda i: (layer, 0, 0),
                               pipeline_mode=pl.Buffered(1)),
                  tab_spec, tab_spec, tab_spec, tab_spec,
                  const((CONV_W, LRU_W)), const((1, LRU_W)),
                  pl.BlockSpec((LRU_W, 2 * LRU_W), lambda i: (0, 0),
                               pipeline_mode=pl.Buffered(1)),
                  const((1, 2 * LRU_W)), const((1, LRU_W))],
        out_specs=[pl.BlockSpec((tm, RET_COLS), lambda i: (i, 0)),
                   pl.BlockSpec((tm, DIFF_COLS), lambda i: (i, 0)),
                   pl.BlockSpec((tm, LRU_W), lambda i: (i, 0))],
        out_shape=[jax.ShapeDtypeStruct((t, RET_COLS), BF16),
                   jax.ShapeDtypeStruct((t, DIFF_COLS), BF16),
                   jax.ShapeDtypeStruct((t, LRU_W), BF16)],
        scratch_shapes=[pltpu.VMEM((tm, D_MODEL), BF16),
                        pltpu.VMEM((tm + SUBLANES, LRU_W), F32),
                        pltpu.VMEM((SUBLANES, LRU_W), F32),
                        pltpu.VMEM((tm, LRU_W), F32)],
        compiler_params=pltpu.CompilerParams(
            dimension_semantics=("arbitrary",), vmem_limit_bytes=VMEM_LIMIT),
        name="inproj_lru",
    )(x2d, g.reshape(1, D_MODEL), w_bf16, *tabs, conv_w, conv_b.reshape(1, LRU_W),
      gate_w, gate_b, lam.reshape(1, LRU_W))


def _retention_tables():
    c = RET_CHUNK
    log_g = jnp.log1p(-jnp.power(2.0, -5.0 - jnp.arange(RET_HEADS, dtype=F32)))
    idx = jnp.arange(c, dtype=F32)
    rel = idx[:, None] - idx[None, :]
    intra = jnp.where(rel[None] >= 0,
                      jnp.exp(jnp.maximum(rel, 0.0)[None] * log_g[:, None, None]), 0.0)
    k_w = jnp.exp((c - 1.0 - idx)[None, :] * log_g[:, None])
    q_w = jnp.exp((idx + 1.0)[None, :] * log_g[:, None])
    cdec = jnp.exp(c * log_g)
    full = (RET_HEADS, c, HEAD_DIM)
    return (intra.astype(F32),
            jnp.broadcast_to(q_w[:, :, None], full),
            jnp.broadcast_to(k_w[:, :, None], full),
            jnp.broadcast_to(cdec[:, None, None], full))


def _retention_kernel(q_ref, k_ref, v_ref, g_ref, dec_ref, qw_ref, kw_ref, cd_ref,
                      o_ref, state_ref):
    @pl.when(pl.program_id(1) == 0)
    def _():
        state_ref[...] = jnp.zeros_like(state_ref)

    nt = (((1,), (1,)), ((), ()))
    tn = (((0,), (0,)), ((), ()))
    for c in range(RET_ROWS // RET_CHUNK):
        rows = slice(c * RET_CHUNK, (c + 1) * RET_CHUNK)
        for h in range(RET_HEADS):
            cols = slice(h * HEAD_DIM, (h + 1) * HEAD_DIM)
            q = q_ref[rows, cols]
            k = k_ref[rows, cols]
            v = v_ref[rows, cols]
            s = lax.dot_general(q, k, nt, preferred_element_type=F32) * dec_ref[h]
            o = jnp.dot(s.astype(BF16), v, preferred_element_type=F32)
            st = state_ref[h]
            qs = (q.astype(F32) * qw_ref[h]).astype(BF16)
            o = o + jnp.dot(qs, st.astype(BF16), preferred_element_type=F32)
            ks = (k.astype(F32) * kw_ref[h]).astype(BF16)
            kv = lax.dot_general(ks, v, tn, preferred_element_type=F32)
            state_ref[h] = st * cd_ref[h] + kv
            ms = jnp.mean(o * o, axis=-1, keepdims=True)
            y = o * lax.rsqrt(ms + EPS)
            o_ref[rows, cols] = (y * _silu(g_ref[rows, cols].astype(F32))).astype(BF16)


def _retention(ret_p, tables, batch, seq):
    t = batch * seq
    steps = seq // RET_ROWS

    def col(j):
        return pl.BlockSpec((RET_ROWS, RET_W), lambda b, i, j=j: (b * steps + i, j))

    tab_spec = pl.BlockSpec((RET_HEADS, RET_CHUNK, HEAD_DIM), lambda b, i: (0, 0, 0))
    return pl.pallas_call(
        _retention_kernel,
        grid=(batch, steps),
        in_specs=[col(0), col(1), col(2), col(3), tab_spec, tab_spec, tab_spec, tab_spec],
        out_specs=pl.BlockSpec((RET_ROWS, RET_W), lambda b, i: (b * steps + i, 0)),
        out_shape=jax.ShapeDtypeStruct((t, RET_W), BF16),
        scratch_shapes=[pltpu.VMEM((RET_HEADS, HEAD_DIM, HEAD_DIM), F32)],
        compiler_params=pltpu.CompilerParams(
            dimension_semantics=("arbitrary", "arbitrary"), vmem_limit_bytes=VMEM_LIMIT),
        name="retention",
    )(ret_p, ret_p, ret_p, ret_p, *tables)


def _diff_kernel(lam_ref, q_ref, k_ref, v_ref, g_ref, subg_ref, o_ref,
                 qm_scr, m_scr, l_scr, acc_scr, *, lam_init):
    qi = pl.program_id(2)
    q = q_ref[...]
    lane = lax.broadcasted_iota(jnp.int32, q.shape, 1)
    qm_scr[0:ATT_T, :] = jnp.where(lane < DIFF_QK, q, jnp.zeros_like(q))
    qm_scr[ATT_T:2 * ATT_T, :] = jnp.where(lane >= DIFF_QK, q, jnp.zeros_like(q))
    m_scr[...] = jnp.full(m_scr.shape, -jnp.inf, F32)
    l_scr[...] = jnp.zeros_like(l_scr)
    acc_scr[...] = jnp.zeros_like(acc_scr)
    nt = (((1,), (1,)), ((), ()))

    def lane_fold(x, op):
        parts = [x[:, i * LANES:(i + 1) * LANES] for i in range(x.shape[1] // LANES)]
        while len(parts) > 1:
            nxt = [op(parts[i], parts[i + 1]) for i in range(0, len(parts) - 1, 2)]
            parts = nxt + parts[len(parts) - len(parts) % 2:]
        return parts[0]

    def block(r0, size, diagonal):
        k = k_ref[pl.ds(r0, size), :]
        v = v_ref[pl.ds(r0, size), :]
        chunks = []
        for i in range(2 * ATT_T // ATT_ROWS):
            rows = slice(i * ATT_ROWS, (i + 1) * ATT_ROWS)
            keys = (i * ATT_ROWS) % ATT_T + ATT_ROWS if diagonal else size
            s = lax.dot_general(qm_scr[rows, :], k[:keys, :], nt, preferred_element_type=F32)
            chunks.append((rows, keys, s))
        for rows, keys, s in chunks:
            if diagonal:
                tail = s[:, keys - ATT_ROWS:]
                row = lax.broadcasted_iota(jnp.int32, tail.shape, 0)
                col = lax.broadcasted_iota(jnp.int32, tail.shape, 1)
                tail = jnp.where(col <= row, tail, -jnp.inf)
                s = tail if keys == ATT_ROWS else jnp.concatenate(
                    [s[:, :keys - ATT_ROWS], tail], axis=1)
            m_prev = m_scr[rows, :]
            m_cur = jnp.max(lane_fold(s, jnp.maximum), axis=-1, keepdims=True)
            m_new = jnp.maximum(m_prev, m_cur)
            alpha = jnp.exp2(m_prev - m_new)
            p = jnp.exp2(s - jnp.tile(m_new, (1, keys // LANES)))
            l_scr[rows, :] = alpha * l_scr[rows, :] + lane_fold(p, jnp.add)
            acc_scr[rows, :] = alpha * acc_scr[rows, :] + jnp.dot(
                p.astype(BF16), v[:keys, :], preferred_element_type=F32)
            m_scr[rows, :] = m_new

    def body(kk, carry):
        block(pl.multiple_of(kk * ATT_TK, ATT_TK), ATT_TK, False)
        return carry

    below = qi * ATT_T
    n_big = below // ATT_TK
    lax.fori_loop(0, n_big, body, 0)
    done = n_big * ATT_TK
    size = ATT_TK // 2
    while size >= ATT_T:
        take = below - done >= size

        @pl.when(take)
        def _(done=done, size=size):
            block(pl.multiple_of(done, ATT_T), size, False)

        done = done + jnp.where(take, size, 0)
        size //= 2
    block(pl.multiple_of(below, ATT_T), ATT_T, True)

    lv = lam_ref[...]
    lam = (jnp.exp(jnp.sum(lv[0:1] * lv[1:2], axis=-1, keepdims=True))
           - jnp.exp(jnp.sum(lv[2:3] * lv[3:4], axis=-1, keepdims=True)) + lam_init)
    l0 = jnp.sum(l_scr[0:ATT_T, :], axis=-1, keepdims=True)
    l1 = jnp.sum(l_scr[ATT_T:2 * ATT_T, :], axis=-1, keepdims=True)
    o = acc_scr[0:ATT_T, :] / l0 - lam * (acc_scr[ATT_T:2 * ATT_T, :] / l1)
    ms = jnp.mean(o * o, axis=-1, keepdims=True)
    y = ((o * lax.rsqrt(ms + EPS)) * subg_ref[...]) * (1.0 - lam_init)
    o_ref[...] = (y * _silu(g_ref[...].astype(F32))).astype(BF16)


def _diff_attention(diff_p, lam_vecs, subg, lam_init, batch, seq):
    t = batch * seq
    nq = seq // ATT_T
    return pl.pallas_call(
        functools.partial(_diff_kernel, lam_init=lam_init),
        grid=(batch, DIFF_HEADS, nq),
        in_specs=[
            pl.BlockSpec((4, DIFF_QK), lambda b, h, i: (0, 0)),
            pl.BlockSpec((ATT_T, HEAD_DIM), lambda b, h, i: (b * nq + i, h)),
            pl.BlockSpec((seq, HEAD_DIM), lambda b, h, i: (b, DIFF_HEADS + h)),
            pl.BlockSpec((seq, HEAD_DIM), lambda b, h, i: (b, 2 * DIFF_HEADS + h)),
            pl.BlockSpec((ATT_T, HEAD_DIM), lambda b, h, i: (b * nq + i, 3 * DIFF_HEADS + h)),
            pl.BlockSpec((1, HEAD_DIM), lambda b, h, i: (0, 0)),
        ],
        out_specs=pl.BlockSpec((ATT_T, HEAD_DIM), lambda b, h, i: (b * nq + i, h)),
        out_shape=jax.ShapeDtypeStruct((t, DIFF_W), BF16),
        scratch_shapes=[pltpu.VMEM((2 * ATT_T, HEAD_DIM), BF16),
                        pltpu.VMEM((2 * ATT_T, LANES), F32),
                        pltpu.VMEM((2 * ATT_T, LANES), F32),
                        pltpu.VMEM((2 * ATT_T, HEAD_DIM), F32)],
        compiler_params=pltpu.CompilerParams(
            dimension_semantics=("arbitrary", "arbitrary", "arbitrary"),
            vmem_limit_bytes=VMEM_LIMIT),
        name="diff_attention",
    )(lam_vecs, diff_p, diff_p, diff_p, diff_p, subg.reshape(1, HEAD_DIM))


def _outproj_kernel(ret_ref, diff_ref, lru_ref, w_ref, g_ref, x_ref, o_ref):
    y = jnp.dot(ret_ref[...], w_ref[0:RET_W, :], preferred_element_type=F32)
    y = y + jnp.dot(diff_ref[...], w_ref[RET_W:RET_W + DIFF_W, :], preferred_element_type=F32)
    y = y + jnp.dot(lru_ref[...], w_ref[RET_W + DIFF_W:D_MIX, :], preferred_element_type=F32)
    ms = jnp.mean(y * y, axis=-1, keepdims=True)
    o_ref[...] = x_ref[...] + (y * lax.rsqrt(ms + EPS)) * g_ref[...]


def _outproj(ret_o, diff_o, lru_o, w_bf16, layer, g, x2d):
    t = x2d.shape[0]
    tm = OUT_TM
    return pl.pallas_call(
        _outproj_kernel,
        grid=(t // tm,),
        in_specs=[pl.BlockSpec((tm, RET_W), lambda i: (i, 0)),
                  pl.BlockSpec((tm, DIFF_W), lambda i: (i, 0)),
                  pl.BlockSpec((tm, LRU_W), lambda i: (i, 0)),
                  pl.BlockSpec((None, D_MIX, D_MODEL), lambda i: (layer, 0, 0),
                               pipeline_mode=pl.Buffered(1)),
                  pl.BlockSpec((1, D_MODEL), lambda i: (0, 0)),
                  pl.BlockSpec((tm, D_MODEL), lambda i: (i, 0))],
        out_specs=pl.BlockSpec((tm, D_MODEL), lambda i: (i, 0)),
        out_shape=jax.ShapeDtypeStruct((t, D_MODEL), F32),
        compiler_params=pltpu.CompilerParams(
            dimension_semantics=("arbitrary",), vmem_limit_bytes=VMEM_LIMIT),
        name="outproj",
    )(ret_o, diff_o, lru_o, w_bf16, g.reshape(1, D_MODEL), x2d)


def kernel(x, positions, pre_norm_g, w_in, diff_lambda_q1, diff_lambda_k1, diff_lambda_q2,
           diff_lambda_k2, diff_subln_g, lru_conv_w, lru_conv_b, lru_wa, lru_ba, lru_wx,
           lru_bx, lru_lambda, w_out, post_norm_g):
    batch, seq, _ = x.shape
    depth = w_in.shape[0]
    assert seq % ATT_T == 0 and seq % RET_ROWS == 0 and seq % INPROJ_TM == 0
    x2d = x.reshape(batch * seq, D_MODEL)
    tabs = _rope_tables(positions)
    ret_tabs = _retention_tables()
    w_in_b = w_in.astype(BF16)
    w_out_b = w_out.astype(BF16)
    for l in range(depth):
        ret_p, diff_p, lru_o = _inproj(
            x2d, pre_norm_g[l], w_in_b, l, tabs, lru_conv_w[l], lru_conv_b[l], lru_wa[l],
            lru_ba[l], lru_wx[l], lru_bx[l], lru_lambda[l], seq)
        ret_o = _retention(ret_p, ret_tabs, batch, seq)
        lam_vecs = jnp.stack([diff_lambda_q1[l], diff_lambda_k1[l],
                              diff_lambda_q2[l], diff_lambda_k2[l]])
        lam_init = 0.8 - 0.6 * math.exp(-0.3 * l)
        diff_o = _diff_attention(diff_p, lam_vecs, diff_subln_g[l], lam_init, batch, seq)
        x2d = _outproj(ret_o, diff_o, lru_o, w_out_b, l, post_norm_g[l], x2d)
    return x2d.reshape(batch, seq, D_MODEL)
```

```python
import functools
import math

import jax
import jax.numpy as jnp
from jax import lax
from jax.experimental import pallas as pl
from jax.experimental.pallas import tpu as pltpu

F32 = jnp.float32
BF16 = jnp.bfloat16

D_MODEL = 2048
HEAD_DIM = 128
RET_HEADS = 6
RET_W = RET_HEADS * HEAD_DIM
RET_CHUNK = 128
DIFF_HEADS = 4
DIFF_QK = HEAD_DIM // 2
DIFF_W = DIFF_HEADS * HEAD_DIM
LRU_W = 768
LRU_BLOCKS = 8
LRU_BW = LRU_W // LRU_BLOCKS
CONV_W = 4
LRU_C = 8.0
D_MIX = RET_W + DIFF_W + LRU_W
RET_COLS = 4 * RET_W
DIFF_COLS = 4 * DIFF_W
LRU_COLS = 2 * LRU_W
D_IN = RET_COLS + DIFF_COLS + LRU_COLS
ROPE_THETA = 10000.0
EPS = 1e-6
LOG2E = math.log2(math.e)

LANES = 128
SUBLANES = 8
VMEM_LIMIT = 52 * 1024 * 1024

ROPE_TM = 1024
INPROJ_TM = 256
INPROJ_TN = 512
RET_ROWS = 512
ATT_T = 1024
ATT_TK = 2048
ATT_ROWS = 256
LRU_GROUPS = 4
OUT_TM = 512
OUT_GROUPS = 2


def _silu(x):
    return x * jax.nn.sigmoid(x)


def _rope_table_kernel(pos_ref, c_ref, cos_r, sin_r, cos_d, sin_d):
    pos = pos_ref[...].astype(F32)
    ang = pos * c_ref[0:1, :]
    c = jnp.cos(ang)
    s = jnp.sin(ang)
    lane = lax.broadcasted_iota(jnp.int32, ang.shape, 1)
    half = HEAD_DIM // 2
    q = DIFF_QK // 2
    low = lane < half
    cos_r[...] = jnp.where(low, c, pltpu.roll(c, half, 1))
    sin_r[...] = jnp.where(low, s, pltpu.roll(s, half, 1)) * c_ref[1:2, :]

    def spread(x):
        return jnp.where(lane < q, pltpu.roll(x, half, 1),
                         jnp.where(lane < 2 * q, pltpu.roll(x, half + q, 1),
                                   jnp.where(lane < 3 * q, x, pltpu.roll(x, q, 1))))

    cos_d[...] = spread(c)
    sin_d[...] = spread(s) * c_ref[2:3, :]


def _rope_tables(positions):
    t = positions.size
    half = HEAD_DIM // 2
    inv_r = 1.0 / (ROPE_THETA ** jnp.linspace(0.0, 1.0, half, dtype=F32))
    inv_d = 1.0 / (ROPE_THETA ** (jnp.arange(0, DIFF_QK, 2, dtype=F32) / DIFF_QK))
    one = jnp.ones((DIFF_QK // 2,), F32)
    consts = jnp.stack([
        jnp.concatenate([inv_r, inv_d, jnp.zeros((LANES - half - DIFF_QK // 2,), F32)]),
        jnp.concatenate([-jnp.ones((half,), F32), jnp.ones((half,), F32)]),
        jnp.concatenate([-one, one, -one, one]),
    ])
    consts = jnp.concatenate([consts, jnp.zeros((SUBLANES - 3, LANES), F32)])
    tab = jax.ShapeDtypeStruct((t, LANES), F32)
    row_spec = pl.BlockSpec((ROPE_TM, LANES), lambda i: (i, 0))
    return pl.pallas_call(
        _rope_table_kernel,
        grid=(t // ROPE_TM,),
        in_specs=[pl.BlockSpec((ROPE_TM, 1), lambda i: (i, 0)),
                  pl.BlockSpec((SUBLANES, LANES), lambda i: (0, 0))],
        out_specs=[row_spec] * 4,
        out_shape=[tab] * 4,
        name="rope_tables",
    )(positions.reshape(t, 1), consts)


def _rope_r(x, cos, sin):
    return x * cos + pltpu.roll(x, HEAD_DIM // 2, 1) * sin


def _rope_d(x, cos, sin, first_half):
    q = DIFF_QK // 2
    partner = jnp.where(first_half, pltpu.roll(x, LANES - q, 1), pltpu.roll(x, q, 1))
    return x * cos + partner * sin


def _inproj_kernel(x_ref, g_ref, w_ref, cr_ref, sr_ref, cd_ref, sd_ref,
                   cw_ref, cb_ref, gw_ref, gb_ref, lam_ref,
                   ret_ref, diff_ref, lru_ref,
                   xn_scr, xs_scr, hc_scr, hs_scr, *, steps_per_seq):
    rows = x_ref.shape[0]
    pad = SUBLANES
    seq_start = pl.program_id(0) % steps_per_seq == 0

    @pl.when(seq_start)
    def _():
        xs_scr[0:pad, :] = jnp.zeros((pad, LRU_W), F32)
        hc_scr[...] = jnp.zeros_like(hc_scr)

    @pl.when(jnp.logical_not(seq_start))
    def _():
        xs_scr[0:pad, :] = xs_scr[rows:rows + pad, :]

    x = x_ref[...]
    ms = jnp.mean(x * x, axis=-1, keepdims=True)
    xn_scr[...] = ((x * lax.rsqrt(ms + EPS)) * g_ref[...]).astype(BF16)
    lane = lax.broadcasted_iota(jnp.int32, (rows, LANES), 1)
    first_half = (lane % DIFF_QK) < (DIFF_QK // 2)

    def proj(c0, width):
        return jnp.dot(xn_scr[...], w_ref[:, c0:c0 + width], preferred_element_type=F32)

    def emit(blk):
        c0 = blk * INPROJ_TN
        acc = proj(c0, INPROJ_TN)
        for s in range(INPROJ_TN // LANES):
            c = c0 + s * LANES
            y = acc[:, s * LANES:(s + 1) * LANES]
            if c < 2 * RET_W:
                y = _rope_r(y, cr_ref[...], sr_ref[...])
                if c >= RET_W:
                    y = y * (HEAD_DIM ** -0.5)
            elif RET_COLS <= c < RET_COLS + 2 * DIFF_W:
                y = _rope_d(y, cd_ref[...], sd_ref[...], first_half)
                if c < RET_COLS + DIFF_W:
                    y = y * (DIFF_QK ** -0.5 * LOG2E)
            y = y.astype(BF16)
            if c < RET_COLS:
                ret_ref[:, c:c + LANES] = y
            else:
                diff_ref[:, c - RET_COLS:c - RET_COLS + LANES] = y

    lru0 = RET_COLS + DIFF_COLS
    xs_scr[pad:pad + rows, :] = proj(lru0, LRU_W)
    lg = proj(lru0 + LRU_W, LRU_W)
    xc = cb_ref[...] + xs_scr[pad - 3:pad - 3 + rows, :] * cw_ref[0:1, :]
    for j in range(1, CONV_W):
        xc = xc + xs_scr[pad - 3 + j:pad - 3 + j + rows, :] * cw_ref[j:j + 1, :]

    emit(0)
    emit(1)

    gates = jnp.dot(xc.astype(BF16), gw_ref[...], preferred_element_type=F32) + gb_ref[...]
    lam = lam_ref[...]
    log_sig = jnp.minimum(lam, 0.0) - jnp.log1p(jnp.exp(-jnp.abs(lam)))
    row = lax.broadcasted_iota(jnp.int32, (SUBLANES, LRU_W), 0)

    def lru_rows(rs, carry):
        r = jax.nn.sigmoid(gates[rs, :LRU_W])
        i = jax.nn.sigmoid(gates[rs, LRU_W:])
        log_a = (LRU_C * r) * log_sig
        a = jnp.exp(log_a)
        mult = jnp.sqrt(-jnp.tanh(log_a) * (a * a + 1.0))
        b = mult * (i * xc[rs, :])
        for t in range((rs.stop - rs.start) // SUBLANES):
            av = a[t * SUBLANES:(t + 1) * SUBLANES, :]
            bv = b[t * SUBLANES:(t + 1) * SUBLANES, :]
            for d in (1, 2, 4):
                a_prev = jnp.where(row >= d, pltpu.roll(av, d, 0), 1.0)
                b_prev = jnp.where(row >= d, pltpu.roll(bv, d, 0), 0.0)
                bv = av * b_prev + bv
                av = av * a_prev
            hv = av * carry + bv
            r0 = rs.start + t * SUBLANES
            hs_scr[r0:r0 + SUBLANES, :] = hv
            carry = jnp.broadcast_to(hv[SUBLANES - 1:SUBLANES, :], (SUBLANES, LRU_W))
        lru_ref[rs, :] = (hs_scr[rs, :] * _silu(lg[rs, :])).astype(BF16)
        return carry

    carry = hc_scr[...]
    blk = 2
    for gi in range(LRU_GROUPS):
        emit(blk)
        blk += 1
        carry = lru_rows(slice(gi * rows // LRU_GROUPS, (gi + 1) * rows // LRU_GROUPS), carry)
    hc_scr[...] = carry
    while blk < lru0 // INPROJ_TN:
        emit(blk)
        blk += 1


def _block_diag(w):
    n, k, _ = w.shape
    eye = jnp.eye(n, dtype=w.dtype)
    return (w[:, :, None, :] * eye[:, None, :, None]).reshape(n * k, n * k)


def _inproj(x2d, g, w_bf16, layer, tabs, conv_w, conv_b, wa, ba, wx, bx, lam, seq):
    t = x2d.shape[0]
    tm = INPROJ_TM
    gate_w = jnp.concatenate([_block_diag(wa), _block_diag(wx)], axis=1).astype(BF16)
    gate_b = jnp.concatenate([ba, bx]).reshape(1, 2 * LRU_W)
    tab_spec = pl.BlockSpec((tm, LANES), lambda i: (i, 0))

    def const(shape):
        return pl.BlockSpec(shape, lambda i: (0, 0))

    return pl.pallas_call(
        functools.partial(_inproj_kernel, steps_per_seq=seq // tm),
        grid=(t // tm,),
        in_specs=[pl.BlockSpec((tm, D_MODEL), lambda i: (i, 0)),
                  const((1, D_MODEL)),
                  pl.BlockSpec((None, D_MODEL, D_IN), lambda i: (layer, 0, 0),
                               pipeline_mode=pl.Buffered(1)),
                  tab_spec, tab_spec, tab_spec, tab_spec,
                  const((CONV_W, LRU_W)), const((1, LRU_W)),
                  pl.BlockSpec((LRU_W, 2 * LRU_W), lambda i: (0, 0),
                               pipeline_mode=pl.Buffered(1)),
                  const((1, 2 * LRU_W)), const((1, LRU_W))],
        out_specs=[pl.BlockSpec((tm, RET_COLS), lambda i: (i, 0)),
                   pl.BlockSpec((tm, DIFF_COLS), lambda i: (i, 0)),
                   pl.BlockSpec((tm, LRU_W), lambda i: (i, 0))],
        out_shape=[jax.ShapeDtypeStruct((t, RET_COLS), BF16),
                   jax.ShapeDtypeStruct((t, DIFF_COLS), BF16),
                   jax.ShapeDtypeStruct((t, LRU_W), BF16)],
        scratch_shapes=[pltpu.VMEM((tm, D_MODEL), BF16),
                        pltpu.VMEM((tm + SUBLANES, LRU_W), F32),
                        pltpu.VMEM((SUBLANES, LRU_W), F32),
                        pltpu.VMEM((tm, LRU_W), F32)],
        compiler_params=pltpu.CompilerParams(
            dimension_semantics=("arbitrary",), vmem_limit_bytes=VMEM_LIMIT),
        name="inproj_lru",
    )(x2d, g.reshape(1, D_MODEL), w_bf16, *tabs, conv_w, conv_b.reshape(1, LRU_W),
      gate_w, gate_b, lam.reshape(1, LRU_W))


def _retention_tables():
    c = RET_CHUNK
    log_g = jnp.log1p(-jnp.power(2.0, -5.0 - jnp.arange(RET_HEADS, dtype=F32)))
    idx = jnp.arange(c, dtype=F32)
    rel = idx[:, None] - idx[None, :]
    intra = jnp.where(rel[None] >= 0,
                      jnp.exp(jnp.maximum(rel, 0.0)[None] * log_g[:, None, None]), 0.0)
    k_w = jnp.exp((c - 1.0 - idx)[None, :] * log_g[:, None])
    q_w = jnp.exp((idx + 1.0)[None, :] * log_g[:, None])
    cdec = jnp.exp(c * log_g)
    full = (RET_HEADS, c, HEAD_DIM)
    return (intra.astype(F32),
            jnp.broadcast_to(q_w[:, :, None], full),
            jnp.broadcast_to(k_w[:, :, None], full),
            jnp.broadcast_to(cdec[:, None, None], full))


def _retention_kernel(q_ref, k_ref, v_ref, g_ref, dec_ref, qw_ref, kw_ref, cd_ref,
                      o_ref, state_ref):
    @pl.when(pl.program_id(1) == 0)
    def _():
        state_ref[...] = jnp.zeros_like(state_ref)

    nt = (((1,), (1,)), ((), ()))
    tn = (((0,), (0,)), ((), ()))
    for c in range(RET_ROWS // RET_CHUNK):
        rows = slice(c * RET_CHUNK, (c + 1) * RET_CHUNK)
        for h in range(RET_HEADS):
            cols = slice(h * HEAD_DIM, (h + 1) * HEAD_DIM)
            q = q_ref[rows, cols]
            k = k_ref[rows, cols]
            v = v_ref[rows, cols]
            st = state_ref[h]
            rhs = jnp.concatenate([k, st.astype(BF16)], axis=0)
            r = lax.dot_general(q, rhs, nt, preferred_element_type=F32)
            s = r[:, :RET_CHUNK] * dec_ref[h]
            o = (jnp.dot(s.astype(BF16), v, preferred_element_type=F32)
                 + r[:, RET_CHUNK:] * qw_ref[h])
            ks = (k.astype(F32) * kw_ref[h]).astype(BF16)
            kv_t = lax.dot_general(v, ks, tn, preferred_element_type=F32)
            state_ref[h] = st * cd_ref[h] + kv_t
            ms = jnp.mean(o * o, axis=-1, keepdims=True)
            y = o * lax.rsqrt(ms + EPS)
            o_ref[rows, cols] = (y * _silu(g_ref[rows, cols].astype(F32))).astype(BF16)


def _retention(ret_p, tables, batch, seq):
    t = batch * seq
    steps = seq // RET_ROWS

    def col(j):
        return pl.BlockSpec((RET_ROWS, RET_W), lambda b, i, j=j: (b * steps + i, j))

    tab_spec = pl.BlockSpec((RET_HEADS, RET_CHUNK, HEAD_DIM), lambda b, i: (0, 0, 0))
    return pl.pallas_call(
        _retention_kernel,
        grid=(batch, steps),
        in_specs=[col(0), col(1), col(2), col(3), tab_spec, tab_spec, tab_spec, tab_spec],
        out_specs=pl.BlockSpec((RET_ROWS, RET_W), lambda b, i: (b * steps + i, 0)),
        out_shape=jax.ShapeDtypeStruct((t, RET_W), BF16),
        scratch_shapes=[pltpu.VMEM((RET_HEADS, HEAD_DIM, HEAD_DIM), F32)],
        compiler_params=pltpu.CompilerParams(
            dimension_semantics=("arbitrary", "arbitrary"), vmem_limit_bytes=VMEM_LIMIT),
        name="retention",
    )(ret_p, ret_p, ret_p, ret_p, *tables)


def _diff_kernel(lam_ref, q_ref, k_ref, v_ref, g_ref, subg_ref, o_ref,
                 qm_scr, m_scr, l_scr, acc_scr, *, lam_init):
    qi = pl.program_id(2)
    q = q_ref[...]
    lane = lax.broadcasted_iota(jnp.int32, q.shape, 1)
    qm_scr[0:ATT_T, :] = jnp.where(lane < DIFF_QK, q, jnp.zeros_like(q))
    qm_scr[ATT_T:2 * ATT_T, :] = jnp.where(lane >= DIFF_QK, q, jnp.zeros_like(q))
    m_scr[...] = jnp.full(m_scr.shape, -jnp.inf, F32)
    l_scr[...] = jnp.zeros_like(l_scr)
    acc_scr[...] = jnp.zeros_like(acc_scr)
    nt = (((1,), (1,)), ((), ()))

    def lane_fold(x, op):
        parts = [x[:, i * LANES:(i + 1) * LANES] for i in range(x.shape[1] // LANES)]
        while len(parts) > 1:
            nxt = [op(parts[i], parts[i + 1]) for i in range(0, len(parts) - 1, 2)]
            parts = nxt + parts[len(parts) - len(parts) % 2:]
        return parts[0]

    def block(r0, size, diagonal):
        k = k_ref[pl.ds(r0, size), :]
        v = v_ref[pl.ds(r0, size), :]
        chunks = []
        for i in range(2 * ATT_T // ATT_ROWS):
            rows = slice(i * ATT_ROWS, (i + 1) * ATT_ROWS)
            keys = (i * ATT_ROWS) % ATT_T + ATT_ROWS if diagonal else size
            s = lax.dot_general(qm_scr[rows, :], k[:keys, :], nt, preferred_element_type=F32)
            chunks.append((rows, keys, s))
        for rows, keys, s in chunks:
            if diagonal:
                tail = s[:, keys - ATT_ROWS:]
                row = lax.broadcasted_iota(jnp.int32, tail.shape, 0)
                col = lax.broadcasted_iota(jnp.int32, tail.shape, 1)
                tail = jnp.where(col <= row, tail, -jnp.inf)
                s = tail if keys == ATT_ROWS else jnp.concatenate(
                    [s[:, :keys - ATT_ROWS], tail], axis=1)
            m_prev = m_scr[rows, :]
            m_cur = jnp.max(lane_fold(s, jnp.maximum), axis=-1, keepdims=True)
            m_new = jnp.maximum(m_prev, m_cur)
            alpha = jnp.exp2(m_prev - m_new)
            p = jnp.exp2(s - jnp.tile(m_new, (1, keys // LANES)))
            l_scr[rows, :] = alpha * l_scr[rows, :] + lane_fold(p, jnp.add)
            acc_scr[rows, :] = alpha * acc_scr[rows, :] + jnp.dot(
                p.astype(BF16), v[:keys, :], preferred_element_type=F32)
            m_scr[rows, :] = m_new

    def body(kk, carry):
        block(pl.multiple_of(kk * ATT_TK, ATT_TK), ATT_TK, False)
        return carry

    below = qi * ATT_T
    n_big = below // ATT_TK
    lax.fori_loop(0, n_big, body, 0)
    done = n_big * ATT_TK
    size = ATT_TK // 2
    while size >= ATT_T:
        take = below - done >= size

        @pl.when(take)
        def _(done=done, size=size):
            block(pl.multiple_of(done, ATT_T), size, False)

        done = done + jnp.where(take, size, 0)
        size //= 2
    block(pl.multiple_of(below, ATT_T), ATT_T, True)

    lv = lam_ref[...]
    lam = (jnp.exp(jnp.sum(lv[0:1] * lv[1:2], axis=-1, keepdims=True))
           - jnp.exp(jnp.sum(lv[2:3] * lv[3:4], axis=-1, keepdims=True)) + lam_init)
    l0 = jnp.sum(l_scr[0:ATT_T, :], axis=-1, keepdims=True)
    l1 = jnp.sum(l_scr[ATT_T:2 * ATT_T, :], axis=-1, keepdims=True)
    o = acc_scr[0:ATT_T, :] / l0 - lam * (acc_scr[ATT_T:2 * ATT_T, :] / l1)
    ms = jnp.mean(o * o, axis=-1, keepdims=True)
    y = ((o * lax.rsqrt(ms + EPS)) * subg_ref[...]) * (1.0 - lam_init)
    o_ref[...] = (y * _silu(g_ref[...].astype(F32))).astype(BF16)


def _diff_attention(diff_p, lam_vecs, subg, lam_init, batch, seq):
    t = batch * seq
    nq = seq // ATT_T
    return pl.pallas_call(
        functools.partial(_diff_kernel, lam_init=lam_init),
        grid=(batch, DIFF_HEADS, nq),
        in_specs=[
            pl.BlockSpec((4, DIFF_QK), lambda b, h, i: (0, 0)),
            pl.BlockSpec((ATT_T, HEAD_DIM), lambda b, h, i: (b * nq + i, h)),
            pl.BlockSpec((seq, HEAD_DIM), lambda b, h, i: (b, DIFF_HEADS + h)),
            pl.BlockSpec((seq, HEAD_DIM), lambda b, h, i: (b, 2 * DIFF_HEADS + h)),
            pl.BlockSpec((ATT_T, HEAD_DIM), lambda b, h, i: (b * nq + i, 3 * DIFF_HEADS + h)),
            pl.BlockSpec((1, HEAD_DIM), lambda b, h, i: (0, 0)),
        ],
        out_specs=pl.BlockSpec((ATT_T, HEAD_DIM), lambda b, h, i: (b * nq + i, h)),
        out_shape=jax.ShapeDtypeStruct((t, DIFF_W), BF16),
        scratch_shapes=[pltpu.VMEM((2 * ATT_T, HEAD_DIM), BF16),
                        pltpu.VMEM((2 * ATT_T, LANES), F32),
                        pltpu.VMEM((2 * ATT_T, LANES), F32),
                        pltpu.VMEM((2 * ATT_T, HEAD_DIM), F32)],
        compiler_params=pltpu.CompilerParams(
            dimension_semantics=("arbitrary", "arbitrary", "arbitrary"),
            vmem_limit_bytes=VMEM_LIMIT),
        name="diff_attention",
    )(lam_vecs, diff_p, diff_p, diff_p, diff_p, subg.reshape(1, HEAD_DIM))


def _outproj_kernel(ret_ref, diff_ref, lru_ref, w_ref, g_ref, x_ref, o_ref):
    rows = x_ref.shape[0] // OUT_GROUPS
    for i in range(OUT_GROUPS):
        rs = slice(i * rows, (i + 1) * rows)
        y = jnp.dot(ret_ref[rs, :], w_ref[0:RET_W, :], preferred_element_type=F32)
        y = y + jnp.dot(diff_ref[rs, :], w_ref[RET_W:RET_W + DIFF_W, :],
                        preferred_element_type=F32)
        y = y + jnp.dot(lru_ref[rs, :], w_ref[RET_W + DIFF_W:D_MIX, :],
                        preferred_element_type=F32)
        ms = jnp.mean(y * y, axis=-1, keepdims=True)
        o_ref[rs, :] = x_ref[rs, :] + (y * lax.rsqrt(ms + EPS)) * g_ref[...]


def _outproj(ret_o, diff_o, lru_o, w_bf16, layer, g, x2d):
    t = x2d.shape[0]
    tm = OUT_TM
    return pl.pallas_call(
        _outproj_kernel,
        grid=(t // tm,),
        in_specs=[pl.BlockSpec((tm, RET_W), lambda i: (i, 0)),
                  pl.BlockSpec((tm, DIFF_W), lambda i: (i, 0)),
                  pl.BlockSpec((tm, LRU_W), lambda i: (i, 0)),
                  pl.BlockSpec((None, D_MIX, D_MODEL), lambda i: (layer, 0, 0),
                               pipeline_mode=pl.Buffered(1)),
                  pl.BlockSpec((1, D_MODEL), lambda i: (0, 0)),
                  pl.BlockSpec((tm, D_MODEL), lambda i: (i, 0))],
        out_specs=pl.BlockSpec((tm, D_MODEL), lambda i: (i, 0)),
        out_shape=jax.ShapeDtypeStruct((t, D_MODEL), F32),
        compiler_params=pltpu.CompilerParams(
            dimension_semantics=("arbitrary",), vmem_limit_bytes=VMEM_LIMIT),
        name="outproj",
    )(ret_o, diff_o, lru_o, w_bf16, g.reshape(1, D_MODEL), x2d)


def kernel(x, positions, pre_norm_g, w_in, diff_lambda_q1, diff_lambda_k1, diff_lambda_q2,
           diff_lambda_k2, diff_subln_g, lru_conv_w, lru_conv_b, lru_wa, lru_ba, lru_wx,
           lru_bx, lru_lambda, w_out, post_norm_g):
    batch, seq, _ = x.shape
    depth = w_in.shape[0]
    assert seq % ATT_T == 0 and seq % RET_ROWS == 0 and seq % INPROJ_TM == 0
    x2d = x.reshape(batch * seq, D_MODEL)
    tabs = _rope_tables(positions)
    ret_tabs = _retention_tables()
    w_in_b = w_in.astype(BF16)
    w_out_b = w_out.astype(BF16)
    for l in range(depth):
        ret_p, diff_p, lru_o = _inproj(
            x2d, pre_norm_g[l], w_in_b, l, tabs, lru_conv_w[l], lru_conv_b[l], lru_wa[l],
            lru_ba[l], lru_wx[l], lru_bx[l], lru_lambda[l], seq)
        ret_o = _retention(ret_p, ret_tabs, batch, seq)
        lam_vecs = jnp.stack([diff_lambda_q1[l], diff_lambda_k1[l],
                              diff_lambda_q2[l], diff_lambda_k2[l]])
        lam_init = 0.8 - 0.6 * math.exp(-0.3 * l)
        diff_o = _diff_attention(diff_p, lam_vecs, diff_subln_g[l], lam_init, batch, seq)
        x2d = _outproj(ret_o, diff_o, lru_o, w_out_b, l, post_norm_g[l], x2d)
    return x2d.reshape(batch, seq, D_MODEL)
```

```python
import functools
import math

import jax
import jax.numpy as jnp
from jax import lax
from jax.experimental import pallas as pl
from jax.experimental.pallas import tpu as pltpu

F32 = jnp.float32
BF16 = jnp.bfloat16

D_MODEL = 2048
HEAD_DIM = 128
RET_HEADS = 6
RET_W = RET_HEADS * HEAD_DIM
RET_CHUNK = 128
DIFF_HEADS = 4
DIFF_QK = HEAD_DIM // 2
DIFF_W = DIFF_HEADS * HEAD_DIM
LRU_W = 768
LRU_BLOCKS = 8
LRU_BW = LRU_W // LRU_BLOCKS
CONV_W = 4
LRU_C = 8.0
D_MIX = RET_W + DIFF_W + LRU_W
RET_COLS = 4 * RET_W
DIFF_COLS = 4 * DIFF_W
LRU_COLS = 2 * LRU_W
D_IN = RET_COLS + DIFF_COLS + LRU_COLS
ROPE_THETA = 10000.0
EPS = 1e-6
LOG2E = math.log2(math.e)

LANES = 128
SUBLANES = 8
VMEM_LIMIT = 52 * 1024 * 1024

ROPE_TM = 1024
INPROJ_TM = 256
INPROJ_TN = 512
RET_ROWS = 512
ATT_T = 1024
ATT_TK = 2048
ATT_ROWS = 256
LRU_GROUPS = 4
OUT_TM = 512
OUT_GROUPS = 2


def _silu(x):
    return x * jax.nn.sigmoid(x)


def _rope_table_kernel(pos_ref, c_ref, win_ref, wout_ref,
                       cos_r, sin_r, cos_d, sin_d, winb_ref, woutb_ref):
    winb_ref[...] = win_ref[...].astype(BF16)
    woutb_ref[...] = wout_ref[...].astype(BF16)
    pos = pos_ref[...].astype(F32)
    ang = pos * c_ref[0:1, :]
    c = jnp.cos(ang)
    s = jnp.sin(ang)
    lane = lax.broadcasted_iota(jnp.int32, ang.shape, 1)
    half = HEAD_DIM // 2
    q = DIFF_QK // 2
    low = lane < half
    cos_r[...] = jnp.where(low, c, pltpu.roll(c, half, 1))
    sin_r[...] = jnp.where(low, s, pltpu.roll(s, half, 1)) * c_ref[1:2, :]

    def spread(x):
        return jnp.where(lane < q, pltpu.roll(x, half, 1),
                         jnp.where(lane < 2 * q, pltpu.roll(x, half + q, 1),
                                   jnp.where(lane < 3 * q, x, pltpu.roll(x, q, 1))))

    cos_d[...] = spread(c)
    sin_d[...] = spread(s) * c_ref[2:3, :]


def _rope_tables(positions, w_in, w_out):
    t = positions.size
    steps = t // ROPE_TM
    half = HEAD_DIM // 2
    inv_r = 1.0 / (ROPE_THETA ** jnp.linspace(0.0, 1.0, half, dtype=F32))
    inv_d = 1.0 / (ROPE_THETA ** (jnp.arange(0, DIFF_QK, 2, dtype=F32) / DIFF_QK))
    one = jnp.ones((DIFF_QK // 2,), F32)
    consts = jnp.stack([
        jnp.concatenate([inv_r, inv_d, jnp.zeros((LANES - half - DIFF_QK // 2,), F32)]),
        jnp.concatenate([-jnp.ones((half,), F32), jnp.ones((half,), F32)]),
        jnp.concatenate([-one, one, -one, one]),
    ])
    consts = jnp.concatenate([consts, jnp.zeros((SUBLANES - 3, LANES), F32)])
    tab = jax.ShapeDtypeStruct((t, LANES), F32)
    row_spec = pl.BlockSpec((ROPE_TM, LANES), lambda i: (i, 0))
    w_specs_in, w_specs_out, w_shapes = [], [], []
    for w in (w_in, w_out):
        rows = w.shape[1] // steps
        w_specs_in.append(pl.BlockSpec((None, rows, w.shape[2]), lambda i: (0, i, 0)))
        w_specs_out.append(pl.BlockSpec((rows, w.shape[2]), lambda i: (i, 0)))
        w_shapes.append(jax.ShapeDtypeStruct(w.shape[1:], BF16))
    outs = pl.pallas_call(
        _rope_table_kernel,
        grid=(steps,),
        in_specs=[pl.BlockSpec((ROPE_TM, 1), lambda i: (i, 0)),
                  pl.BlockSpec((SUBLANES, LANES), lambda i: (0, 0))] + w_specs_in,
        out_specs=[row_spec] * 4 + w_specs_out,
        out_shape=[tab] * 4 + w_shapes,
        compiler_params=pltpu.CompilerParams(
            dimension_semantics=("arbitrary",), vmem_limit_bytes=VMEM_LIMIT),
        name="rope_tables",
    )(positions.reshape(t, 1), consts, w_in, w_out)
    return outs[:4], outs[4], outs[5]


def _rope_r(x, cos, sin):
    return x * cos + pltpu.roll(x, HEAD_DIM // 2, 1) * sin


def _rope_d(x, cos, sin, first_half):
    q = DIFF_QK // 2
    partner = jnp.where(first_half, pltpu.roll(x, LANES - q, 1), pltpu.roll(x, q, 1))
    return x * cos + partner * sin


def _inproj_kernel(x_ref, g_ref, w_ref, cr_ref, sr_ref, cd_ref, sd_ref,
                   cw_ref, cb_ref, gw_ref, gb_ref, lam_ref,
                   ret_ref, diff_ref, lru_ref,
                   xn_scr, xs_scr, hc_scr, hs_scr, *, steps_per_seq):
    rows = x_ref.shape[0]
    pad = SUBLANES
    seq_start = pl.program_id(0) % steps_per_seq == 0

    @pl.when(seq_start)
    def _():
        xs_scr[0:pad, :] = jnp.zeros((pad, LRU_W), F32)
        hc_scr[...] = jnp.zeros_like(hc_scr)

    @pl.when(jnp.logical_not(seq_start))
    def _():
        xs_scr[0:pad, :] = xs_scr[rows:rows + pad, :]

    x = x_ref[...]
    ms = jnp.mean(x * x, axis=-1, keepdims=True)
    xn_scr[...] = ((x * lax.rsqrt(ms + EPS)) * g_ref[...]).astype(BF16)
    lane = lax.broadcasted_iota(jnp.int32, (rows, LANES), 1)
    first_half = (lane % DIFF_QK) < (DIFF_QK // 2)

    def proj(c0, width):
        return jnp.dot(xn_scr[...], w_ref[:, c0:c0 + width], preferred_element_type=F32)

    def emit(blk):
        c0 = blk * INPROJ_TN
        acc = proj(c0, INPROJ_TN)
        for s in range(INPROJ_TN // LANES):
            c = c0 + s * LANES
            y = acc[:, s * LANES:(s + 1) * LANES]
            if c < 2 * RET_W:
                y = _rope_r(y, cr_ref[...], sr_ref[...])
                if c >= RET_W:
                    y = y * (HEAD_DIM ** -0.5)
            elif RET_COLS <= c < RET_COLS + 2 * DIFF_W:
                y = _rope_d(y, cd_ref[...], sd_ref[...], first_half)
                if c < RET_COLS + DIFF_W:
                    y = y * (DIFF_QK ** -0.5 * LOG2E)
            y = y.astype(BF16)
            if c < RET_COLS:
                ret_ref[:, c:c + LANES] = y
            else:
                diff_ref[:, c - RET_COLS:c - RET_COLS + LANES] = y

    lru0 = RET_COLS + DIFF_COLS
    xs_scr[pad:pad + rows, :] = proj(lru0, LRU_W)
    lg = proj(lru0 + LRU_W, LRU_W)
    xc = cb_ref[...] + xs_scr[pad - 3:pad - 3 + rows, :] * cw_ref[0:1, :]
    for j in range(1, CONV_W):
        xc = xc + xs_scr[pad - 3 + j:pad - 3 + j + rows, :] * cw_ref[j:j + 1, :]

    emit(0)
    emit(1)

    gates = jnp.dot(xc.astype(BF16), gw_ref[...], preferred_element_type=F32) + gb_ref[...]
    lam = lam_ref[...]
    log_sig = jnp.minimum(lam, 0.0) - jnp.log1p(jnp.exp(-jnp.abs(lam)))
    row = lax.broadcasted_iota(jnp.int32, (SUBLANES, LRU_W), 0)

    def lru_rows(rs, carry):
        r = jax.nn.sigmoid(gates[rs, :LRU_W])
        i = jax.nn.sigmoid(gates[rs, LRU_W:])
        log_a = (LRU_C * r) * log_sig
        a = jnp.exp(log_a)
        mult = jnp.sqrt(-jnp.tanh(log_a) * (a * a + 1.0))
        b = mult * (i * xc[rs, :])
        for t in range((rs.stop - rs.start) // SUBLANES):
            av = a[t * SUBLANES:(t + 1) * SUBLANES, :]
            bv = b[t * SUBLANES:(t + 1) * SUBLANES, :]
            for d in (1, 2, 4):
                a_prev = jnp.where(row >= d, pltpu.roll(av, d, 0), 1.0)
                b_prev = jnp.where(row >= d, pltpu.roll(bv, d, 0), 0.0)
                bv = av * b_prev + bv
                av = av * a_prev
            hv = av * carry + bv
            r0 = rs.start + t * SUBLANES
            hs_scr[r0:r0 + SUBLANES, :] = hv
            carry = jnp.broadcast_to(hv[SUBLANES - 1:SUBLANES, :], (SUBLANES, LRU_W))
        lru_ref[rs, :] = (hs_scr[rs, :] * _silu(lg[rs, :])).astype(BF16)
        return carry

    carry = hc_scr[...]
    blk = 2
    for gi in range(LRU_GROUPS):
        emit(blk)
        blk += 1
        carry = lru_rows(slice(gi * rows // LRU_GROUPS, (gi + 1) * rows // LRU_GROUPS), carry)
    hc_scr[...] = carry
    while blk < lru0 // INPROJ_TN:
        emit(blk)
        blk += 1


def _block_diag(w):
    n, k, _ = w.shape
    eye = jnp.eye(n, dtype=w.dtype)
    return (w[:, :, None, :] * eye[:, None, :, None]).reshape(n * k, n * k)


def _inproj(x2d, g, w_bf16, tabs, conv_w, conv_b, wa, ba, wx, bx, lam, seq):
    t = x2d.shape[0]
    tm = INPROJ_TM
    gate_w = jnp.concatenate([_block_diag(wa), _block_diag(wx)], axis=1).astype(BF16)
    gate_b = jnp.concatenate([ba, bx]).reshape(1, 2 * LRU_W)
    tab_spec = pl.BlockSpec((tm, LANES), lambda i: (i, 0))

    def const(shape):
        return pl.BlockSpec(shape, lambda i: (0, 0))

    return pl.pallas_call(
        functools.partial(_inproj_kernel, steps_per_seq=seq // tm),
        grid=(t // tm,),
        in_specs=[pl.BlockSpec((tm, D_MODEL), lambda i: (i, 0)),
                  const((1, D_MODEL)),
                  pl.BlockSpec((D_MODEL, D_IN), lambda i: (0, 0),
                               pipeline_mode=pl.Buffered(1)),
                  tab_spec, tab_spec, tab_spec, tab_spec,
                  const((CONV_W, LRU_W)), const((1, LRU_W)),
                  pl.BlockSpec((LRU_W, 2 * LRU_W), lambda i: (0, 0),
                               pipeline_mode=pl.Buffered(1)),
                  const((1, 2 * LRU_W)), const((1, LRU_W))],
        out_specs=[pl.BlockSpec((tm, RET_COLS), lambda i: (i, 0)),
                   pl.BlockSpec((tm, DIFF_COLS), lambda i: (i, 0)),
                   pl.BlockSpec((tm, LRU_W), lambda i: (i, 0))],
        out_shape=[jax.ShapeDtypeStruct((t, RET_COLS), BF16),
                   jax.ShapeDtypeStruct((t, DIFF_COLS), BF16),
                   jax.ShapeDtypeStruct((t, LRU_W), BF16)],
        scratch_shapes=[pltpu.VMEM((tm, D_MODEL), BF16),
                        pltpu.VMEM((tm + SUBLANES, LRU_W), F32),
                        pltpu.VMEM((SUBLANES, LRU_W), F32),
                        pltpu.VMEM((tm, LRU_W), F32)],
        compiler_params=pltpu.CompilerParams(
            dimension_semantics=("arbitrary",), vmem_limit_bytes=VMEM_LIMIT),
        name="inproj_lru",
    )(x2d, g.reshape(1, D_MODEL), w_bf16, *tabs, conv_w, conv_b.reshape(1, LRU_W),
      gate_w, gate_b, lam.reshape(1, LRU_W))


def _retention_tables():
    c = RET_CHUNK
    log_g = jnp.log1p(-jnp.power(2.0, -5.0 - jnp.arange(RET_HEADS, dtype=F32)))
    idx = jnp.arange(c, dtype=F32)
    rel = idx[:, None] - idx[None, :]
    intra = jnp.where(rel[None] >= 0,
                      jnp.exp(jnp.maximum(rel, 0.0)[None] * log_g[:, None, None]), 0.0)
    k_w = jnp.exp((c - 1.0 - idx)[None, :] * log_g[:, None])
    q_w = jnp.exp((idx + 1.0)[None, :] * log_g[:, None])
    cdec = jnp.exp(c * log_g)
    full = (RET_HEADS, c, HEAD_DIM)
    return (intra.astype(F32),
            jnp.broadcast_to(q_w[:, :, None], full),
            jnp.broadcast_to(k_w[:, :, None], full),
            jnp.broadcast_to(cdec[:, None, None], full))


def _retention_kernel(q_ref, k_ref, v_ref, g_ref, dec_ref, qw_ref, kw_ref, cd_ref,
                      o_ref, state_ref):
    @pl.when(pl.program_id(1) == 0)
    def _():
        state_ref[...] = jnp.zeros_like(state_ref)

    nt = (((1,), (1,)), ((), ()))
    tn = (((0,), (0,)), ((), ()))
    for c in range(RET_ROWS // RET_CHUNK):
        rows = slice(c * RET_CHUNK, (c + 1) * RET_CHUNK)
        for h in range(RET_HEADS):
            cols = slice(h * HEAD_DIM, (h + 1) * HEAD_DIM)
            q = q_ref[rows, cols]
            k = k_ref[rows, cols]
            v = v_ref[rows, cols]
            st = state_ref[h]
            rhs = jnp.concatenate([k, st.astype(BF16)], axis=0)
            r = lax.dot_general(q, rhs, nt, preferred_element_type=F32)
            s = r[:, :RET_CHUNK] * dec_ref[h]
            o = (jnp.dot(s.astype(BF16), v, preferred_element_type=F32)
                 + r[:, RET_CHUNK:] * qw_ref[h])
            ks = (k.astype(F32) * kw_ref[h]).astype(BF16)
            kv_t = lax.dot_general(v, ks, tn, preferred_element_type=F32)
            state_ref[h] = st * cd_ref[h] + kv_t
            ms = jnp.mean(o * o, axis=-1, keepdims=True)
            y = o * lax.rsqrt(ms + EPS)
            o_ref[rows, cols] = (y * _silu(g_ref[rows, cols].astype(F32))).astype(BF16)


def _retention(ret_p, tables, batch, seq):
    t = batch * seq
    steps = seq // RET_ROWS

    def col(j):
        return pl.BlockSpec((RET_ROWS, RET_W), lambda b, i, j=j: (b * steps + i, j))

    tab_spec = pl.BlockSpec((RET_HEADS, RET_CHUNK, HEAD_DIM), lambda b, i: (0, 0, 0))
    return pl.pallas_call(
        _retention_kernel,
        grid=(batch, steps),
        in_specs=[col(0), col(1), col(2), col(3), tab_spec, tab_spec, tab_spec, tab_spec],
        out_specs=pl.BlockSpec((RET_ROWS, RET_W), lambda b, i: (b * steps + i, 0)),
        out_shape=jax.ShapeDtypeStruct((t, RET_W), BF16),
        scratch_shapes=[pltpu.VMEM((RET_HEADS, HEAD_DIM, HEAD_DIM), F32)],
        compiler_params=pltpu.CompilerParams(
            dimension_semantics=("arbitrary", "arbitrary"), vmem_limit_bytes=VMEM_LIMIT),
        name="retention",
    )(ret_p, ret_p, ret_p, ret_p, *tables)


def _diff_kernel(lam_ref, q_ref, k_ref, v_ref, g_ref, subg_ref, *refs, lam_init, cast_next):
    if cast_next:
        win_ref, wout_ref, o_ref, winb_ref, woutb_ref, qm_scr, m_scr, l_scr, acc_scr = refs
        winb_ref[...] = win_ref[...].astype(BF16)
        woutb_ref[...] = wout_ref[...].astype(BF16)
    else:
        o_ref, qm_scr, m_scr, l_scr, acc_scr = refs
    qi = pl.program_id(2)
    q = q_ref[...]
    lane = lax.broadcasted_iota(jnp.int32, q.shape, 1)
    qm_scr[0:ATT_T, :] = jnp.where(lane < DIFF_QK, q, jnp.zeros_like(q))
    qm_scr[ATT_T:2 * ATT_T, :] = jnp.where(lane >= DIFF_QK, q, jnp.zeros_like(q))
    m_scr[...] = jnp.full(m_scr.shape, -jnp.inf, F32)
    l_scr[...] = jnp.zeros_like(l_scr)
    acc_scr[...] = jnp.zeros_like(acc_scr)
    nt = (((1,), (1,)), ((), ()))

    def lane_fold(x, op):
        parts = [x[:, i * LANES:(i + 1) * LANES] for i in range(x.shape[1] // LANES)]
        while len(parts) > 1:
            nxt = [op(parts[i], parts[i + 1]) for i in range(0, len(parts) - 1, 2)]
            parts = nxt + parts[len(parts) - len(parts) % 2:]
        return parts[0]

    def block(r0, size, diagonal):
        k = k_ref[pl.ds(r0, size), :]
        v = v_ref[pl.ds(r0, size), :]
        chunks = []
        for i in range(2 * ATT_T // ATT_ROWS):
            rows = slice(i * ATT_ROWS, (i + 1) * ATT_ROWS)
            keys = (i * ATT_ROWS) % ATT_T + ATT_ROWS if diagonal else size
            s = lax.dot_general(qm_scr[rows, :], k[:keys, :], nt, preferred_element_type=F32)
            chunks.append((rows, keys, s))
        for rows, keys, s in chunks:
            if diagonal:
                tail = s[:, keys - ATT_ROWS:]
                row = lax.broadcasted_iota(jnp.int32, tail.shape, 0)
                col = lax.broadcasted_iota(jnp.int32, tail.shape, 1)
                tail = jnp.where(col <= row, tail, -jnp.inf)
                s = tail if keys == ATT_ROWS else jnp.concatenate(
                    [s[:, :keys - ATT_ROWS], tail], axis=1)
            m_prev = m_scr[rows, :]
            m_cur = jnp.max(lane_fold(s, jnp.maximum), axis=-1, keepdims=True)
            m_new = jnp.maximum(m_prev, m_cur)
            alpha = jnp.exp2(m_prev - m_new)
            p = jnp.exp2(s - jnp.tile(m_new, (1, keys // LANES)))
            l_scr[rows, :] = alpha * l_scr[rows, :] + lane_fold(p, jnp.add)
            acc_scr[rows, :] = alpha * acc_scr[rows, :] + jnp.dot(
                p.astype(BF16), v[:keys, :], preferred_element_type=F32)
            m_scr[rows, :] = m_new

    def body(kk, carry):
        block(pl.multiple_of(kk * ATT_TK, ATT_TK), ATT_TK, False)
        return carry

    below = qi * ATT_T
    n_big = below // ATT_TK
    lax.fori_loop(0, n_big, body, 0)
    done = n_big * ATT_TK
    size = ATT_TK // 2
    while size >= ATT_T:
        take = below - done >= size

        @pl.when(take)
        def _(done=done, size=size):
            block(pl.multiple_of(done, ATT_T), size, False)

        done = done + jnp.where(take, size, 0)
        size //= 2
    block(pl.multiple_of(below, ATT_T), ATT_T, True)

    lv = lam_ref[...]
    lam = (jnp.exp(jnp.sum(lv[0:1] * lv[1:2], axis=-1, keepdims=True))
           - jnp.exp(jnp.sum(lv[2:3] * lv[3:4], axis=-1, keepdims=True)) + lam_init)
    l0 = jnp.sum(l_scr[0:ATT_T, :], axis=-1, keepdims=True)
    l1 = jnp.sum(l_scr[ATT_T:2 * ATT_T, :], axis=-1, keepdims=True)
    o = acc_scr[0:ATT_T, :] / l0 - lam * (acc_scr[ATT_T:2 * ATT_T, :] / l1)
    ms = jnp.mean(o * o, axis=-1, keepdims=True)
    y = ((o * lax.rsqrt(ms + EPS)) * subg_ref[...]) * (1.0 - lam_init)
    o_ref[...] = (y * _silu(g_ref[...].astype(F32))).astype(BF16)


def _diff_attention(diff_p, lam_vecs, subg, lam_init, batch, seq, next_weights=None):
    t = batch * seq
    nq = seq // ATT_T
    steps = batch * DIFF_HEADS * nq
    in_specs = [
        pl.BlockSpec((4, DIFF_QK), lambda b, h, i: (0, 0)),
        pl.BlockSpec((ATT_T, HEAD_DIM), lambda b, h, i: (b * nq + i, h)),
        pl.BlockSpec((seq, HEAD_DIM), lambda b, h, i: (b, DIFF_HEADS + h)),
        pl.BlockSpec((seq, HEAD_DIM), lambda b, h, i: (b, 2 * DIFF_HEADS + h)),
        pl.BlockSpec((ATT_T, HEAD_DIM), lambda b, h, i: (b * nq + i, 3 * DIFF_HEADS + h)),
        pl.BlockSpec((1, HEAD_DIM), lambda b, h, i: (0, 0)),
    ]
    out_specs = [pl.BlockSpec((ATT_T, HEAD_DIM), lambda b, h, i: (b * nq + i, h))]
    out_shape = [jax.ShapeDtypeStruct((t, DIFF_W), BF16)]
    args = [lam_vecs, diff_p, diff_p, diff_p, diff_p, subg.reshape(1, HEAD_DIM)]
    if next_weights is not None:
        w_in, w_out, layer = next_weights

        def slab(b, h, i):
            return (b * DIFF_HEADS + h) * nq + i

        for w in (w_in, w_out):
            rows = w.shape[1] // steps
            in_specs.append(pl.BlockSpec((None, rows, w.shape[2]),
                                         lambda b, h, i: (layer, slab(b, h, i), 0)))
            out_specs.append(pl.BlockSpec((rows, w.shape[2]),
                                          lambda b, h, i: (slab(b, h, i), 0)))
            out_shape.append(jax.ShapeDtypeStruct(w.shape[1:], BF16))
            args.append(w)
    return pl.pallas_call(
        functools.partial(_diff_kernel, lam_init=lam_init,
                          cast_next=next_weights is not None),
        grid=(batch, DIFF_HEADS, nq),
        in_specs=in_specs,
        out_specs=out_specs,
        out_shape=out_shape,
        scratch_shapes=[pltpu.VMEM((2 * ATT_T, HEAD_DIM), BF16),
                        pltpu.VMEM((2 * ATT_T, LANES), F32),
                        pltpu.VMEM((2 * ATT_T, LANES), F32),
                        pltpu.VMEM((2 * ATT_T, HEAD_DIM), F32)],
        compiler_params=pltpu.CompilerParams(
            dimension_semantics=("arbitrary", "arbitrary", "arbitrary"),
            vmem_limit_bytes=VMEM_LIMIT),
        name="diff_attention",
    )(*args)


def _outproj_kernel(ret_ref, diff_ref, lru_ref, w_ref, g_ref, x_ref, o_ref):
    rows = x_ref.shape[0] // OUT_GROUPS
    for i in range(OUT_GROUPS):
        rs = slice(i * rows, (i + 1) * rows)
        y = jnp.dot(ret_ref[rs, :], w_ref[0:RET_W, :], preferred_element_type=F32)
        y = y + jnp.dot(diff_ref[rs, :], w_ref[RET_W:RET_W + DIFF_W, :],
                        preferred_element_type=F32)
        y = y + jnp.dot(lru_ref[rs, :], w_ref[RET_W + DIFF_W:D_MIX, :],
                        preferred_element_type=F32)
        ms = jnp.mean(y * y, axis=-1, keepdims=True)
        o_ref[rs, :] = x_ref[rs, :] + (y * lax.rsqrt(ms + EPS)) * g_ref[...]


def _outproj(ret_o, diff_o, lru_o, w_bf16, g, x2d):
    t = x2d.shape[0]
    tm = OUT_TM
    return pl.pallas_call(
        _outproj_kernel,
        grid=(t // tm,),
        in_specs=[pl.BlockSpec((tm, RET_W), lambda i: (i, 0)),
                  pl.BlockSpec((tm, DIFF_W), lambda i: (i, 0)),
                  pl.BlockSpec((tm, LRU_W), lambda i: (i, 0)),
                  pl.BlockSpec((D_MIX, D_MODEL), lambda i: (0, 0),
                               pipeline_mode=pl.Buffered(1)),
                  pl.BlockSpec((1, D_MODEL), lambda i: (0, 0)),
                  pl.BlockSpec((tm, D_MODEL), lambda i: (i, 0))],
        out_specs=pl.BlockSpec((tm, D_MODEL), lambda i: (i, 0)),
        out_shape=jax.ShapeDtypeStruct((t, D_MODEL), F32),
        compiler_params=pltpu.CompilerParams(
            dimension_semantics=("arbitrary",), vmem_limit_bytes=VMEM_LIMIT),
        name="outproj",
    )(ret_o, diff_o, lru_o, w_bf16, g.reshape(1, D_MODEL), x2d)


def kernel(x, positions, pre_norm_g, w_in, diff_lambda_q1, diff_lambda_k1, diff_lambda_q2,
           diff_lambda_k2, diff_subln_g, lru_conv_w, lru_conv_b, lru_wa, lru_ba, lru_wx,
           lru_bx, lru_lambda, w_out, post_norm_g):
    batch, seq, _ = x.shape
    depth = w_in.shape[0]
    assert seq % ATT_T == 0 and seq % RET_ROWS == 0 and seq % INPROJ_TM == 0
    x2d = x.reshape(batch * seq, D_MODEL)
    tabs, w_in_b, w_out_b = _rope_tables(positions, w_in, w_out)
    ret_tabs = _retention_tables()
    for l in range(depth):
        ret_p, diff_p, lru_o = _inproj(
            x2d, pre_norm_g[l], w_in_b, tabs, lru_conv_w[l], lru_conv_b[l], lru_wa[l],
            lru_ba[l], lru_wx[l], lru_bx[l], lru_lambda[l], seq)
        ret_o = _retention(ret_p, ret_tabs, batch, seq)
        lam_vecs = jnp.stack([diff_lambda_q1[l], diff_lambda_k1[l],
                              diff_lambda_q2[l], diff_lambda_k2[l]])
        lam_init = 0.8 - 0.6 * math.exp(-0.3 * l)
        w_out_cur = w_out_b
        if l + 1 < depth:
            diff_o, w_in_b, w_out_b = _diff_attention(
                diff_p, lam_vecs, diff_subln_g[l], lam_init, batch, seq, (w_in, w_out, l + 1))
        else:
            diff_o, = _diff_attention(diff_p, lam_vecs, diff_subln_g[l], lam_init, batch, seq)
        x2d = _outproj(ret_o, diff_o, lru_o, w_out_cur, post_norm_g[l], x2d)
    return x2d.reshape(batch, seq, D_MODEL)
```

```python
import functools
import math

import jax
import jax.numpy as jnp
from jax import lax
from jax.experimental import pallas as pl
from jax.experimental.pallas import tpu as pltpu

F32 = jnp.float32
BF16 = jnp.bfloat16

D_MODEL = 2048
HEAD_DIM = 128
RET_HEADS = 6
RET_W = RET_HEADS * HEAD_DIM
RET_CHUNK = 128
DIFF_HEADS = 4
DIFF_QK = HEAD_DIM // 2
DIFF_W = DIFF_HEADS * HEAD_DIM
LRU_W = 768
LRU_BLOCKS = 8
LRU_BW = LRU_W // LRU_BLOCKS
CONV_W = 4
LRU_C = 8.0
D_MIX = RET_W + DIFF_W + LRU_W
RET_COLS = 4 * RET_W
DIFF_COLS = 4 * DIFF_W
LRU_COLS = 2 * LRU_W
D_IN = RET_COLS + DIFF_COLS + LRU_COLS
ROPE_THETA = 10000.0
EPS = 1e-6
LOG2E = math.log2(math.e)

LANES = 128
SUBLANES = 8
VMEM_LIMIT = 52 * 1024 * 1024

ROPE_TM = 1024
INPROJ_TM = 256
INPROJ_TN = 512
RET_ROWS = 1024
ATT_T = 1024
ATT_TK = 2048
ATT_ROWS = 256
LRU_GROUPS = 4
OUT_TM = 512
OUT_GROUPS = 2


def _silu(x):
    return x * jax.nn.sigmoid(x)


def _rope_table_kernel(pos_ref, c_ref, win_ref, wout_ref,
                       cos_r, sin_r, cos_d, sin_d, winb_ref, woutb_ref):
    winb_ref[...] = win_ref[...].astype(BF16)
    woutb_ref[...] = wout_ref[...].astype(BF16)
    pos = pos_ref[...].astype(F32)
    ang = pos * c_ref[0:1, :]
    c = jnp.cos(ang)
    s = jnp.sin(ang)
    lane = lax.broadcasted_iota(jnp.int32, ang.shape, 1)
    half = HEAD_DIM // 2
    q = DIFF_QK // 2
    low = lane < half
    cos_r[...] = jnp.where(low, c, pltpu.roll(c, half, 1))
    sin_r[...] = jnp.where(low, s, pltpu.roll(s, half, 1)) * c_ref[1:2, :]

    def spread(x):
        return jnp.where(lane < q, pltpu.roll(x, half, 1),
                         jnp.where(lane < 2 * q, pltpu.roll(x, half + q, 1),
                                   jnp.where(lane < 3 * q, x, pltpu.roll(x, q, 1))))

    cos_d[...] = spread(c)
    sin_d[...] = spread(s) * c_ref[2:3, :]


def _rope_tables(positions, w_in, w_out):
    t = positions.size
    steps = t // ROPE_TM
    half = HEAD_DIM // 2
    inv_r = 1.0 / (ROPE_THETA ** jnp.linspace(0.0, 1.0, half, dtype=F32))
    inv_d = 1.0 / (ROPE_THETA ** (jnp.arange(0, DIFF_QK, 2, dtype=F32) / DIFF_QK))
    one = jnp.ones((DIFF_QK // 2,), F32)
    consts = jnp.stack([
        jnp.concatenate([inv_r, inv_d, jnp.zeros((LANES - half - DIFF_QK // 2,), F32)]),
        jnp.concatenate([-jnp.ones((half,), F32), jnp.ones((half,), F32)]),
        jnp.concatenate([-one, one, -one, one]),
    ])
    consts = jnp.concatenate([consts, jnp.zeros((SUBLANES - 3, LANES), F32)])
    tab = jax.ShapeDtypeStruct((t, LANES), F32)
    row_spec = pl.BlockSpec((ROPE_TM, LANES), lambda i: (i, 0))
    w_specs_in, w_specs_out, w_shapes = [], [], []
    for w in (w_in, w_out):
        rows = w.shape[1] // steps
        w_specs_in.append(pl.BlockSpec((None, rows, w.shape[2]), lambda i: (0, i, 0)))
        w_specs_out.append(pl.BlockSpec((rows, w.shape[2]), lambda i: (i, 0)))
        w_shapes.append(jax.ShapeDtypeStruct(w.shape[1:], BF16))
    outs = pl.pallas_call(
        _rope_table_kernel,
        grid=(steps,),
        in_specs=[pl.BlockSpec((ROPE_TM, 1), lambda i: (i, 0)),
                  pl.BlockSpec((SUBLANES, LANES), lambda i: (0, 0))] + w_specs_in,
        out_specs=[row_spec] * 4 + w_specs_out,
        out_shape=[tab] * 4 + w_shapes,
        compiler_params=pltpu.CompilerParams(
            dimension_semantics=("arbitrary",), vmem_limit_bytes=VMEM_LIMIT),
        name="rope_tables",
    )(positions.reshape(t, 1), consts, w_in, w_out)
    return outs[:4], outs[4], outs[5]


def _rope_r(x, cos, sin):
    return x * cos + pltpu.roll(x, HEAD_DIM // 2, 1) * sin


def _rope_d(x, cos, sin, first_half):
    q = DIFF_QK // 2
    partner = jnp.where(first_half, pltpu.roll(x, LANES - q, 1), pltpu.roll(x, q, 1))
    return x * cos + partner * sin


def _inproj_kernel(x_ref, g_ref, w_ref, cr_ref, sr_ref, cd_ref, sd_ref,
                   cw_ref, cb_ref, gw_ref, gb_ref, lam_ref,
                   ret_ref, diff_ref, lru_ref,
                   xn_scr, xs_scr, hc_scr, hs_scr, *, steps_per_seq):
    rows = x_ref.shape[0]
    pad = SUBLANES
    seq_start = pl.program_id(0) % steps_per_seq == 0

    @pl.when(seq_start)
    def _():
        xs_scr[0:pad, :] = jnp.zeros((pad, LRU_W), F32)
        hc_scr[...] = jnp.zeros_like(hc_scr)

    @pl.when(jnp.logical_not(seq_start))
    def _():
        xs_scr[0:pad, :] = xs_scr[rows:rows + pad, :]

    x = x_ref[...]
    ms = jnp.mean(x * x, axis=-1, keepdims=True)
    xn_scr[...] = ((x * lax.rsqrt(ms + EPS)) * g_ref[...]).astype(BF16)
    lane = lax.broadcasted_iota(jnp.int32, (rows, LANES), 1)
    first_half = (lane % DIFF_QK) < (DIFF_QK // 2)

    def proj(c0, width):
        return jnp.dot(xn_scr[...], w_ref[:, c0:c0 + width], preferred_element_type=F32)

    def emit(blk):
        c0 = blk * INPROJ_TN
        acc = proj(c0, INPROJ_TN)
        for s in range(INPROJ_TN // LANES):
            c = c0 + s * LANES
            y = acc[:, s * LANES:(s + 1) * LANES]
            if c < 2 * RET_W:
                y = _rope_r(y, cr_ref[...], sr_ref[...])
                if c >= RET_W:
                    y = y * (HEAD_DIM ** -0.5)
            elif RET_COLS <= c < RET_COLS + 2 * DIFF_W:
                y = _rope_d(y, cd_ref[...], sd_ref[...], first_half)
                if c < RET_COLS + DIFF_W:
                    y = y * (DIFF_QK ** -0.5 * LOG2E)
            y = y.astype(BF16)
            if c < RET_COLS:
                ret_ref[:, c:c + LANES] = y
            else:
                diff_ref[:, c - RET_COLS:c - RET_COLS + LANES] = y

    lru0 = RET_COLS + DIFF_COLS
    xs_scr[pad:pad + rows, :] = proj(lru0, LRU_W)
    lg = proj(lru0 + LRU_W, LRU_W)
    xc = cb_ref[...] + xs_scr[pad - 3:pad - 3 + rows, :] * cw_ref[0:1, :]
    for j in range(1, CONV_W):
        xc = xc + xs_scr[pad - 3 + j:pad - 3 + j + rows, :] * cw_ref[j:j + 1, :]

    emit(0)
    emit(1)

    gates = jnp.dot(xc.astype(BF16), gw_ref[...], preferred_element_type=F32) + gb_ref[...]
    lam = lam_ref[...]
    log_sig = jnp.minimum(lam, 0.0) - jnp.log1p(jnp.exp(-jnp.abs(lam)))
    row = lax.broadcasted_iota(jnp.int32, (SUBLANES, LRU_W), 0)

    def lru_rows(rs, carry):
        r = jax.nn.sigmoid(gates[rs, :LRU_W])
        i = jax.nn.sigmoid(gates[rs, LRU_W:])
        log_a = (LRU_C * r) * log_sig
        a = jnp.exp(log_a)
        mult = jnp.sqrt(-jnp.tanh(log_a) * (a * a + 1.0))
        b = mult * (i * xc[rs, :])
        for t in range((rs.stop - rs.start) // SUBLANES):
            av = a[t * SUBLANES:(t + 1) * SUBLANES, :]
            bv = b[t * SUBLANES:(t + 1) * SUBLANES, :]
            for d in (1, 2, 4):
                a_prev = jnp.where(row >= d, pltpu.roll(av, d, 0), 1.0)
                b_prev = jnp.where(row >= d, pltpu.roll(bv, d, 0), 0.0)
                bv = av * b_prev + bv
                av = av * a_prev
            hv = av * carry + bv
            r0 = rs.start + t * SUBLANES
            hs_scr[r0:r0 + SUBLANES, :] = hv
            carry = jnp.broadcast_to(hv[SUBLANES - 1:SUBLANES, :], (SUBLANES, LRU_W))
        lru_ref[rs, :] = (hs_scr[rs, :] * _silu(lg[rs, :])).astype(BF16)
        return carry

    carry = hc_scr[...]
    blk = 2
    for gi in range(LRU_GROUPS):
        emit(blk)
        blk += 1
        carry = lru_rows(slice(gi * rows // LRU_GROUPS, (gi + 1) * rows // LRU_GROUPS), carry)
    hc_scr[...] = carry
    while blk < lru0 // INPROJ_TN:
        emit(blk)
        blk += 1


def _block_diag(w):
    n, k, _ = w.shape
    eye = jnp.eye(n, dtype=w.dtype)
    return (w[:, :, None, :] * eye[:, None, :, None]).reshape(n * k, n * k)


def _inproj(x2d, g, w_bf16, tabs, conv_w, conv_b, wa, ba, wx, bx, lam, seq):
    t = x2d.shape[0]
    tm = INPROJ_TM
    gate_w = jnp.concatenate([_block_diag(wa), _block_diag(wx)], axis=1).astype(BF16)
    gate_b = jnp.concatenate([ba, bx]).reshape(1, 2 * LRU_W)
    tab_spec = pl.BlockSpec((tm, LANES), lambda i: (i, 0))

    def const(shape):
        return pl.BlockSpec(shape, lambda i: (0, 0))

    return pl.pallas_call(
        functools.partial(_inproj_kernel, steps_per_seq=seq // tm),
        grid=(t // tm,),
        in_specs=[pl.BlockSpec((tm, D_MODEL), lambda i: (i, 0)),
                  const((1, D_MODEL)),
                  pl.BlockSpec((D_MODEL, D_IN), lambda i: (0, 0),
                               pipeline_mode=pl.Buffered(1)),
                  tab_spec, tab_spec, tab_spec, tab_spec,
                  const((CONV_W, LRU_W)), const((1, LRU_W)),
                  pl.BlockSpec((LRU_W, 2 * LRU_W), lambda i: (0, 0),
                               pipeline_mode=pl.Buffered(1)),
                  const((1, 2 * LRU_W)), const((1, LRU_W))],
        out_specs=[pl.BlockSpec((tm, RET_COLS), lambda i: (i, 0)),
                   pl.BlockSpec((tm, DIFF_COLS), lambda i: (i, 0)),
                   pl.BlockSpec((tm, LRU_W), lambda i: (i, 0))],
        out_shape=[jax.ShapeDtypeStruct((t, RET_COLS), BF16),
                   jax.ShapeDtypeStruct((t, DIFF_COLS), BF16),
                   jax.ShapeDtypeStruct((t, LRU_W), BF16)],
        scratch_shapes=[pltpu.VMEM((tm, D_MODEL), BF16),
                        pltpu.VMEM((tm + SUBLANES, LRU_W), F32),
                        pltpu.VMEM((SUBLANES, LRU_W), F32),
                        pltpu.VMEM((tm, LRU_W), F32)],
        compiler_params=pltpu.CompilerParams(
            dimension_semantics=("arbitrary",), vmem_limit_bytes=VMEM_LIMIT),
        name="inproj_lru",
    )(x2d, g.reshape(1, D_MODEL), w_bf16, *tabs, conv_w, conv_b.reshape(1, LRU_W),
      gate_w, gate_b, lam.reshape(1, LRU_W))


def _retention_tables():
    c = RET_CHUNK
    log_g = jnp.log1p(-jnp.power(2.0, -5.0 - jnp.arange(RET_HEADS, dtype=F32)))
    idx = jnp.arange(c, dtype=F32)
    rel = idx[:, None] - idx[None, :]
    intra = jnp.where(rel[None] >= 0,
                      jnp.exp(jnp.maximum(rel, 0.0)[None] * log_g[:, None, None]), 0.0)
    k_w = jnp.exp((c - 1.0 - idx)[None, :] * log_g[:, None])
    q_w = jnp.exp((idx + 1.0)[None, :] * log_g[:, None])
    cdec = jnp.exp(c * log_g)
    full = (RET_HEADS, c, HEAD_DIM)
    return (intra.astype(F32),
            jnp.broadcast_to(q_w[:, :, None], full),
            jnp.broadcast_to(k_w[:, :, None], full),
            jnp.broadcast_to(cdec[:, None, None], full))


def _retention_kernel(q_ref, k_ref, v_ref, g_ref, dec_ref, qw_ref, kw_ref, cd_ref,
                      o_ref, state_ref):
    @pl.when(pl.program_id(1) == 0)
    def _():
        state_ref[...] = jnp.zeros_like(state_ref)

    nt = (((1,), (1,)), ((), ()))
    tn = (((0,), (0,)), ((), ()))
    for c in range(RET_ROWS // RET_CHUNK):
        rows = slice(c * RET_CHUNK, (c + 1) * RET_CHUNK)
        for h in range(RET_HEADS):
            cols = slice(h * HEAD_DIM, (h + 1) * HEAD_DIM)
            q = q_ref[rows, cols]
            k = k_ref[rows, cols]
            v = v_ref[rows, cols]
            st = state_ref[h]
            rhs = jnp.concatenate([k, st.astype(BF16)], axis=0)
            r = lax.dot_general(q, rhs, nt, preferred_element_type=F32)
            s = r[:, :RET_CHUNK] * dec_ref[h]
            o = (jnp.dot(s.astype(BF16), v, preferred_element_type=F32)
                 + r[:, RET_CHUNK:] * qw_ref[h])
            ks = (k.astype(F32) * kw_ref[h]).astype(BF16)
            kv_t = lax.dot_general(v, ks, tn, preferred_element_type=F32)
            state_ref[h] = st * cd_ref[h] + kv_t
            ms = jnp.mean(o * o, axis=-1, keepdims=True)
            y = o * lax.rsqrt(ms + EPS)
            o_ref[rows, cols] = (y * _silu(g_ref[rows, cols].astype(F32))).astype(BF16)


def _retention(ret_p, tables, batch, seq):
    t = batch * seq
    steps = seq // RET_ROWS

    def col(j):
        return pl.BlockSpec((RET_ROWS, RET_W), lambda b, i, j=j: (b * steps + i, j))

    tab_spec = pl.BlockSpec((RET_HEADS, RET_CHUNK, HEAD_DIM), lambda b, i: (0, 0, 0))
    return pl.pallas_call(
        _retention_kernel,
        grid=(batch, steps),
        in_specs=[col(0), col(1), col(2), col(3), tab_spec, tab_spec, tab_spec, tab_spec],
        out_specs=pl.BlockSpec((RET_ROWS, RET_W), lambda b, i: (b * steps + i, 0)),
        out_shape=jax.ShapeDtypeStruct((t, RET_W), BF16),
        scratch_shapes=[pltpu.VMEM((RET_HEADS, HEAD_DIM, HEAD_DIM), F32)],
        compiler_params=pltpu.CompilerParams(
            dimension_semantics=("arbitrary", "arbitrary"), vmem_limit_bytes=VMEM_LIMIT),
        name="retention",
    )(ret_p, ret_p, ret_p, ret_p, *tables)


def _diff_kernel(lam_ref, q_ref, k_ref, v_ref, g_ref, subg_ref, *refs, lam_init, cast_next):
    if cast_next:
        win_ref, wout_ref, o_ref, winb_ref, woutb_ref, qm_scr, m_scr, l_scr, acc_scr = refs
        winb_ref[...] = win_ref[...].astype(BF16)
        woutb_ref[...] = wout_ref[...].astype(BF16)
    else:
        o_ref, qm_scr, m_scr, l_scr, acc_scr = refs
    qi = pl.program_id(2)
    q = q_ref[...]
    lane = lax.broadcasted_iota(jnp.int32, q.shape, 1)
    qm_scr[0:ATT_T, :] = jnp.where(lane < DIFF_QK, q, jnp.zeros_like(q))
    qm_scr[ATT_T:2 * ATT_T, :] = jnp.where(lane >= DIFF_QK, q, jnp.zeros_like(q))
    nt = (((1,), (1,)), ((), ()))

    def lane_fold(x, op):
        parts = [x[:, i * LANES:(i + 1) * LANES] for i in range(x.shape[1] // LANES)]
        while len(parts) > 1:
            nxt = [op(parts[i], parts[i + 1]) for i in range(0, len(parts) - 1, 2)]
            parts = nxt + parts[len(parts) - len(parts) % 2:]
        return parts[0]

    def block(r0, size, diagonal):
        k = k_ref[pl.ds(r0, size), :]
        v = v_ref[pl.ds(r0, size), :]
        chunks = []
        for i in range(2 * ATT_T // ATT_ROWS):
            rows = slice(i * ATT_ROWS, (i + 1) * ATT_ROWS)
            keys = (i * ATT_ROWS) % ATT_T + ATT_ROWS if diagonal else size
            s = lax.dot_general(qm_scr[rows, :], k[:keys, :], nt, preferred_element_type=F32)
            chunks.append((rows, keys, s))
        for rows, keys, s in chunks:
            if diagonal:
                tail = s[:, keys - ATT_ROWS:]
                row = lax.broadcasted_iota(jnp.int32, tail.shape, 0)
                col = lax.broadcasted_iota(jnp.int32, tail.shape, 1)
                tail = jnp.where(col <= row, tail, -jnp.inf)
                s = tail if keys == ATT_ROWS else jnp.concatenate(
                    [s[:, :keys - ATT_ROWS], tail], axis=1)
            m_cur = jnp.max(lane_fold(s, jnp.maximum), axis=-1, keepdims=True)
            if diagonal:
                m_new = jnp.broadcast_to(m_cur, (ATT_ROWS, LANES))
            else:
                m_prev = m_scr[rows, :]
                m_new = jnp.maximum(m_prev, m_cur)
                alpha = jnp.exp2(m_prev - m_new)
            p = jnp.exp2(s - jnp.tile(m_new, (1, keys // LANES)))
            l_new = lane_fold(p, jnp.add)
            acc_new = jnp.dot(p.astype(BF16), v[:keys, :], preferred_element_type=F32)
            if not diagonal:
                l_new = alpha * l_scr[rows, :] + l_new
                acc_new = alpha * acc_scr[rows, :] + acc_new
            l_scr[rows, :] = l_new
            acc_scr[rows, :] = acc_new
            m_scr[rows, :] = m_new

    def body(kk, carry):
        block(pl.multiple_of(kk * ATT_TK, ATT_TK), ATT_TK, False)
        return carry

    below = qi * ATT_T
    block(pl.multiple_of(below, ATT_T), ATT_T, True)
    n_big = below // ATT_TK
    lax.fori_loop(0, n_big, body, 0)
    done = n_big * ATT_TK
    size = ATT_TK // 2
    while size >= ATT_T:
        take = below - done >= size

        @pl.when(take)
        def _(done=done, size=size):
            block(pl.multiple_of(done, ATT_T), size, False)

        done = done + jnp.where(take, size, 0)
        size //= 2

    lv = lam_ref[...]
    lam = (jnp.exp(jnp.sum(lv[0:1] * lv[1:2], axis=-1, keepdims=True))
           - jnp.exp(jnp.sum(lv[2:3] * lv[3:4], axis=-1, keepdims=True)) + lam_init)
    l0 = jnp.sum(l_scr[0:ATT_T, :], axis=-1, keepdims=True)
    l1 = jnp.sum(l_scr[ATT_T:2 * ATT_T, :], axis=-1, keepdims=True)
    o = acc_scr[0:ATT_T, :] / l0 - lam * (acc_scr[ATT_T:2 * ATT_T, :] / l1)
    ms = jnp.mean(o * o, axis=-1, keepdims=True)
    y = ((o * lax.rsqrt(ms + EPS)) * subg_ref[...]) * (1.0 - lam_init)
    o_ref[...] = (y * _silu(g_ref[...].astype(F32))).astype(BF16)


def _diff_attention(diff_p, lam_vecs, subg, lam_init, batch, seq, next_weights=None):
    t = batch * seq
    nq = seq // ATT_T
    steps = batch * DIFF_HEADS * nq
    in_specs = [
        pl.BlockSpec((4, DIFF_QK), lambda b, h, i: (0, 0)),
        pl.BlockSpec((ATT_T, HEAD_DIM), lambda b, h, i: (b * nq + i, h)),
        pl.BlockSpec((seq, HEAD_DIM), lambda b, h, i: (b, DIFF_HEADS + h)),
        pl.BlockSpec((seq, HEAD_DIM), lambda b, h, i: (b, 2 * DIFF_HEADS + h)),
        pl.BlockSpec((ATT_T, HEAD_DIM), lambda b, h, i: (b * nq + i, 3 * DIFF_HEADS + h)),
        pl.BlockSpec((1, HEAD_DIM), lambda b, h, i: (0, 0)),
    ]
    out_specs = [pl.BlockSpec((ATT_T, HEAD_DIM), lambda b, h, i: (b * nq + i, h))]
    out_shape = [jax.ShapeDtypeStruct((t, DIFF_W), BF16)]
    args = [lam_vecs, diff_p, diff_p, diff_p, diff_p, subg.reshape(1, HEAD_DIM)]
    if next_weights is not None:
        w_in, w_out, layer = next_weights

        def slab(b, h, i):
            return (b * DIFF_HEADS + h) * nq + i

        for w in (w_in, w_out):
            rows = w.shape[1] // steps
            in_specs.append(pl.BlockSpec((None, rows, w.shape[2]),
                                         lambda b, h, i: (layer, slab(b, h, i), 0)))
            out_specs.append(pl.BlockSpec((rows, w.shape[2]),
                                          lambda b, h, i: (slab(b, h, i), 0)))
            out_shape.append(jax.ShapeDtypeStruct(w.shape[1:], BF16))
            args.append(w)
    return pl.pallas_call(
        functools.partial(_diff_kernel, lam_init=lam_init,
                          cast_next=next_weights is not None),
        grid=(batch, DIFF_HEADS, nq),
        in_specs=in_specs,
        out_specs=out_specs,
        out_shape=out_shape,
        scratch_shapes=[pltpu.VMEM((2 * ATT_T, HEAD_DIM), BF16),
                        pltpu.VMEM((2 * ATT_T, LANES), F32),
                        pltpu.VMEM((2 * ATT_T, LANES), F32),
                        pltpu.VMEM((2 * ATT_T, HEAD_DIM), F32)],
        compiler_params=pltpu.CompilerParams(
            dimension_semantics=("arbitrary", "arbitrary", "arbitrary"),
            vmem_limit_bytes=VMEM_LIMIT),
        name="diff_attention",
    )(*args)


def _outproj_kernel(ret_ref, diff_ref, lru_ref, w_ref, g_ref, x_ref, o_ref):
    rows = x_ref.shape[0] // OUT_GROUPS
    for i in range(OUT_GROUPS):
        rs = slice(i * rows, (i + 1) * rows)
        y = jnp.dot(ret_ref[rs, :], w_ref[0:RET_W, :], preferred_element_type=F32)
        y = y + jnp.dot(diff_ref[rs, :], w_ref[RET_W:RET_W + DIFF_W, :],
                        preferred_element_type=F32)
        y = y + jnp.dot(lru_ref[rs, :], w_ref[RET_W + DIFF_W:D_MIX, :],
                        preferred_element_type=F32)
        ms = jnp.mean(y * y, axis=-1, keepdims=True)
        o_ref[rs, :] = x_ref[rs, :] + (y * lax.rsqrt(ms + EPS)) * g_ref[...]


def _outproj(ret_o, diff_o, lru_o, w_bf16, g, x2d):
    t = x2d.shape[0]
    tm = OUT_TM
    return pl.pallas_call(
        _outproj_kernel,
        grid=(t // tm,),
        in_specs=[pl.BlockSpec((tm, RET_W), lambda i: (i, 0)),
                  pl.BlockSpec((tm, DIFF_W), lambda i: (i, 0)),
                  pl.BlockSpec((tm, LRU_W), lambda i: (i, 0)),
                  pl.BlockSpec((D_MIX, D_MODEL), lambda i: (0, 0),
                               pipeline_mode=pl.Buffered(1)),
                  pl.BlockSpec((1, D_MODEL), lambda i: (0, 0)),
                  pl.BlockSpec((tm, D_MODEL), lambda i: (i, 0))],
        out_specs=pl.BlockSpec((tm, D_MODEL), lambda i: (i, 0)),
        out_shape=jax.ShapeDtypeStruct((t, D_MODEL), F32),
        compiler_params=pltpu.CompilerParams(
            dimension_semantics=("arbitrary",), vmem_limit_bytes=VMEM_LIMIT),
        name="outproj",
    )(ret_o, diff_o, lru_o, w_bf16, g.reshape(1, D_MODEL), x2d)


def kernel(x, positions, pre_norm_g, w_in, diff_lambda_q1, diff_lambda_k1, diff_lambda_q2,
           diff_lambda_k2, diff_subln_g, lru_conv_w, lru_conv_b, lru_wa, lru_ba, lru_wx,
           lru_bx, lru_lambda, w_out, post_norm_g):
    batch, seq, _ = x.shape
    depth = w_in.shape[0]
    assert seq % ATT_T == 0 and seq % RET_ROWS == 0 and seq % INPROJ_TM == 0
    bf16_rows = 2 * SUBLANES
    for steps in (batch * seq // ROPE_TM, batch * DIFF_HEADS * (seq // ATT_T)):
        assert D_MODEL % (steps * bf16_rows) == 0 and D_MIX % (steps * bf16_rows) == 0
    x2d = x.reshape(batch * seq, D_MODEL)
    tabs, w_in_b, w_out_b = _rope_tables(positions, w_in, w_out)
    ret_tabs = _retention_tables()
    for l in range(depth):
        ret_p, diff_p, lru_o = _inproj(
            x2d, pre_norm_g[l], w_in_b, tabs, lru_conv_w[l], lru_conv_b[l], lru_wa[l],
            lru_ba[l], lru_wx[l], lru_bx[l], lru_lambda[l], seq)
        ret_o = _retention(ret_p, ret_tabs, batch, seq)
        lam_vecs = jnp.stack([diff_lambda_q1[l], diff_lambda_k1[l],
                              diff_lambda_q2[l], diff_lambda_k2[l]])
        lam_init = 0.8 - 0.6 * math.exp(-0.3 * l)
        w_out_cur = w_out_b
        if l + 1 < depth:
            diff_o, w_in_b, w_out_b = _diff_attention(
                diff_p, lam_vecs, diff_subln_g[l], lam_init, batch, seq, (w_in, w_out, l + 1))
        else:
            diff_o, = _diff_attention(diff_p, lam_vecs, diff_subln_g[l], lam_init, batch, seq)
        x2d = _outproj(ret_o, diff_o, lru_o, w_out_cur, post_norm_g[l], x2d)
    return x2d.reshape(batch, seq, D_MODEL)
```

```python
import functools
import math

import jax
import jax.numpy as jnp
from jax import lax
from jax.experimental import pallas as pl
from jax.experimental.pallas import tpu as pltpu

F32 = jnp.float32
BF16 = jnp.bfloat16

D_MODEL = 2048
HEAD_DIM = 128
RET_HEADS = 6
RET_W = RET_HEADS * HEAD_DIM
RET_CHUNK = 128
DIFF_HEADS = 4
DIFF_QK = HEAD_DIM // 2
DIFF_W = DIFF_HEADS * HEAD_DIM
LRU_W = 768
LRU_BLOCKS = 8
LRU_BW = LRU_W // LRU_BLOCKS
CONV_W = 4
LRU_C = 8.0
D_MIX = RET_W + DIFF_W + LRU_W
RET_COLS = 4 * RET_W
DIFF_COLS = 4 * DIFF_W
LRU_COLS = 2 * LRU_W
D_IN = RET_COLS + DIFF_COLS + LRU_COLS
ROPE_THETA = 10000.0
EPS = 1e-6
LOG2E = math.log2(math.e)

LANES = 128
SUBLANES = 8
VMEM_LIMIT = 52 * 1024 * 1024

ROPE_TM = 1024
INPROJ_TM = 256
INPROJ_TN = 512
RET_ROWS = 1024
ATT_T = 1024
ATT_TK = 2048
ATT_ROWS = 256
LRU_GROUPS = 4
OUT_TM = 512
OUT_GROUPS = 2


def _silu(x):
    return x * jax.nn.sigmoid(x)


def _rope_table_kernel(pos_ref, c_ref, win_ref, wout_ref,
                       cos_r, sin_r, cos_d, sin_d, winb_ref, woutb_ref):
    winb_ref[...] = win_ref[...].astype(BF16)
    woutb_ref[...] = wout_ref[...].astype(BF16)
    pos = pos_ref[...].astype(F32)
    ang = pos * c_ref[0:1, :]
    c = jnp.cos(ang)
    s = jnp.sin(ang)
    lane = lax.broadcasted_iota(jnp.int32, ang.shape, 1)
    half = HEAD_DIM // 2
    q = DIFF_QK // 2
    low = lane < half
    cos_r[...] = jnp.where(low, c, pltpu.roll(c, half, 1))
    sin_r[...] = jnp.where(low, s, pltpu.roll(s, half, 1)) * c_ref[1:2, :]

    def spread(x):
        return jnp.where(lane < q, pltpu.roll(x, half, 1),
                         jnp.where(lane < 2 * q, pltpu.roll(x, half + q, 1),
                                   jnp.where(lane < 3 * q, x, pltpu.roll(x, q, 1))))

    cos_d[...] = spread(c)
    sin_d[...] = spread(s) * c_ref[2:3, :]


def _rope_tables(positions, w_in, w_out):
    t = positions.size
    steps = t // ROPE_TM
    half = HEAD_DIM // 2
    inv_r = 1.0 / (ROPE_THETA ** jnp.linspace(0.0, 1.0, half, dtype=F32))
    inv_d = 1.0 / (ROPE_THETA ** (jnp.arange(0, DIFF_QK, 2, dtype=F32) / DIFF_QK))
    one = jnp.ones((DIFF_QK // 2,), F32)
    consts = jnp.stack([
        jnp.concatenate([inv_r, inv_d, jnp.zeros((LANES - half - DIFF_QK // 2,), F32)]),
        jnp.concatenate([-jnp.ones((half,), F32), jnp.ones((half,), F32)]),
        jnp.concatenate([-one, one, -one, one]),
    ])
    consts = jnp.concatenate([consts, jnp.zeros((SUBLANES - 3, LANES), F32)])
    tab = jax.ShapeDtypeStruct((t, LANES), F32)
    row_spec = pl.BlockSpec((ROPE_TM, LANES), lambda i: (i, 0))
    w_specs_in, w_specs_out, w_shapes = [], [], []
    for w in (w_in, w_out):
        rows = w.shape[1] // steps
        w_specs_in.append(pl.BlockSpec((None, rows, w.shape[2]), lambda i: (0, i, 0)))
        w_specs_out.append(pl.BlockSpec((rows, w.shape[2]), lambda i: (i, 0)))
        w_shapes.append(jax.ShapeDtypeStruct(w.shape[1:], BF16))
    outs = pl.pallas_call(
        _rope_table_kernel,
        grid=(steps,),
        in_specs=[pl.BlockSpec((ROPE_TM, 1), lambda i: (i, 0)),
                  pl.BlockSpec((SUBLANES, LANES), lambda i: (0, 0))] + w_specs_in,
        out_specs=[row_spec] * 4 + w_specs_out,
        out_shape=[tab] * 4 + w_shapes,
        compiler_params=pltpu.CompilerParams(
            dimension_semantics=("arbitrary",), vmem_limit_bytes=VMEM_LIMIT),
        name="rope_tables",
    )(positions.reshape(t, 1), consts, w_in, w_out)
    return outs[:4], outs[4], outs[5]


def _rope_r(x, cos, sin):
    return x * cos + pltpu.roll(x, HEAD_DIM // 2, 1) * sin


def _rope_d(x, cos, sin, first_half):
    q = DIFF_QK // 2
    partner = jnp.where(first_half, pltpu.roll(x, LANES - q, 1), pltpu.roll(x, q, 1))
    return x * cos + partner * sin


def _inproj_kernel(x_ref, g_ref, w_ref, cr_ref, sr_ref, cd_ref, sd_ref,
                   cw_ref, cb_ref, gw_ref, gb_ref, lam_ref, never_ref,
                   ret_ref, diff_ref, lru_ref,
                   xn_scr, xs_scr, hc_scr, hs_scr, *, steps_per_seq):
    rows = x_ref.shape[0]
    pad = SUBLANES
    seq_start = pl.program_id(0) % steps_per_seq == 0

    @pl.when(seq_start)
    def _():
        xs_scr[0:pad, :] = jnp.zeros((pad, LRU_W), F32)
        hc_scr[...] = jnp.zeros_like(hc_scr)

    @pl.when(jnp.logical_not(seq_start))
    def _():
        xs_scr[0:pad, :] = xs_scr[rows:rows + pad, :]

    x = x_ref[...]
    ms = jnp.mean(x * x, axis=-1, keepdims=True)
    xn_scr[...] = ((x * lax.rsqrt(ms + EPS)) * g_ref[...]).astype(BF16)
    lane = lax.broadcasted_iota(jnp.int32, (rows, LANES), 1)
    first_half = (lane % DIFF_QK) < (DIFF_QK // 2)

    def proj(c0, width):
        return jnp.dot(xn_scr[...], w_ref[:, c0:c0 + width], preferred_element_type=F32)

    def emit(blk):
        c0 = blk * INPROJ_TN
        acc = proj(c0, INPROJ_TN)
        for s in range(INPROJ_TN // LANES):
            c = c0 + s * LANES
            y = acc[:, s * LANES:(s + 1) * LANES]
            if c < 2 * RET_W:
                y = _rope_r(y, cr_ref[...], sr_ref[...])
                if c >= RET_W:
                    y = y * (HEAD_DIM ** -0.5)
            elif RET_COLS <= c < RET_COLS + 2 * DIFF_W:
                y = _rope_d(y, cd_ref[...], sd_ref[...], first_half)
                if c < RET_COLS + DIFF_W:
                    y = y * (DIFF_QK ** -0.5 * LOG2E)
            y = y.astype(BF16)
            if c < RET_COLS:
                ret_ref[:, c:c + LANES] = y
            else:
                diff_ref[:, c - RET_COLS:c - RET_COLS + LANES] = y
        return acc[rows - SUBLANES:rows, INPROJ_TN - LANES:INPROJ_TN]

    lru0 = RET_COLS + DIFF_COLS
    xs_scr[pad:pad + rows, :] = proj(lru0, LRU_W)
    lg = proj(lru0 + LRU_W, LRU_W)
    xc = cb_ref[...] + xs_scr[pad - 3:pad - 3 + rows, :] * cw_ref[0:1, :]
    for j in range(1, CONV_W):
        xc = xc + xs_scr[pad - 3 + j:pad - 3 + j + rows, :] * cw_ref[j:j + 1, :]

    emit(0)
    emit(1)

    gates = jnp.dot(xc.astype(BF16), gw_ref[...], preferred_element_type=F32) + gb_ref[...]
    lam = lam_ref[...]
    log_sig = jnp.minimum(lam, 0.0) - jnp.log1p(jnp.exp(-jnp.abs(lam)))
    row = lax.broadcasted_iota(jnp.int32, (SUBLANES, LRU_W), 0)

    def lru_rows(rs, carry, after):
        never = never_ref[0] != 0
        g = gates[rs, :]
        g = jnp.where(never, jnp.tile(after, (g.shape[0] // SUBLANES, g.shape[1] // LANES)), g)
        r = jax.nn.sigmoid(g[:, :LRU_W])
        i = jax.nn.sigmoid(g[:, LRU_W:])
        log_a = (LRU_C * r) * log_sig
        a = jnp.exp(log_a)
        mult = jnp.sqrt(-jnp.tanh(log_a) * (a * a + 1.0))
        b = mult * (i * xc[rs, :])
        for t in range((rs.stop - rs.start) // SUBLANES):
            av = a[t * SUBLANES:(t + 1) * SUBLANES, :]
            bv = b[t * SUBLANES:(t + 1) * SUBLANES, :]
            for d in (1, 2, 4):
                a_prev = jnp.where(row >= d, pltpu.roll(av, d, 0), 1.0)
                b_prev = jnp.where(row >= d, pltpu.roll(bv, d, 0), 0.0)
                bv = av * b_prev + bv
                av = av * a_prev
            hv = av * carry + bv
            r0 = rs.start + t * SUBLANES
            hs_scr[r0:r0 + SUBLANES, :] = hv
            carry = jnp.broadcast_to(hv[SUBLANES - 1:SUBLANES, :], (SUBLANES, LRU_W))
        lru_ref[rs, :] = (hs_scr[rs, :] * _silu(lg[rs, :])).astype(BF16)
        return carry

    carry = hc_scr[...]
    blk = 2
    for gi in range(LRU_GROUPS):
        after = emit(blk)
        blk += 1
        carry = lru_rows(slice(gi * rows // LRU_GROUPS, (gi + 1) * rows // LRU_GROUPS),
                         carry, after)
    hc_scr[...] = carry
    while blk < lru0 // INPROJ_TN:
        emit(blk)
        blk += 1


def _block_diag(w):
    n, k, _ = w.shape
    eye = jnp.eye(n, dtype=w.dtype)
    return (w[:, :, None, :] * eye[:, None, :, None]).reshape(n * k, n * k)


def _inproj(x2d, g, w_bf16, tabs, conv_w, conv_b, wa, ba, wx, bx, lam, seq):
    t = x2d.shape[0]
    tm = INPROJ_TM
    gate_w = jnp.concatenate([_block_diag(wa), _block_diag(wx)], axis=1).astype(BF16)
    gate_b = jnp.concatenate([ba, bx]).reshape(1, 2 * LRU_W)
    tab_spec = pl.BlockSpec((tm, LANES), lambda i: (i, 0))

    def const(shape):
        return pl.BlockSpec(shape, lambda i: (0, 0))

    return pl.pallas_call(
        functools.partial(_inproj_kernel, steps_per_seq=seq // tm),
        grid=(t // tm,),
        in_specs=[pl.BlockSpec((tm, D_MODEL), lambda i: (i, 0)),
                  const((1, D_MODEL)),
                  pl.BlockSpec((D_MODEL, D_IN), lambda i: (0, 0),
                               pipeline_mode=pl.Buffered(1)),
                  tab_spec, tab_spec, tab_spec, tab_spec,
                  const((CONV_W, LRU_W)), const((1, LRU_W)),
                  pl.BlockSpec((LRU_W, 2 * LRU_W), lambda i: (0, 0),
                               pipeline_mode=pl.Buffered(1)),
                  const((1, 2 * LRU_W)), const((1, LRU_W)),
                  pl.BlockSpec(memory_space=pltpu.SMEM)],
        out_specs=[pl.BlockSpec((tm, RET_COLS), lambda i: (i, 0)),
                   pl.BlockSpec((tm, DIFF_COLS), lambda i: (i, 0)),
                   pl.BlockSpec((tm, LRU_W), lambda i: (i, 0))],
        out_shape=[jax.ShapeDtypeStruct((t, RET_COLS), BF16),
                   jax.ShapeDtypeStruct((t, DIFF_COLS), BF16),
                   jax.ShapeDtypeStruct((t, LRU_W), BF16)],
        scratch_shapes=[pltpu.VMEM((tm, D_MODEL), BF16),
                        pltpu.VMEM((tm + SUBLANES, LRU_W), F32),
                        pltpu.VMEM((SUBLANES, LRU_W), F32),
                        pltpu.VMEM((tm, LRU_W), F32)],
        compiler_params=pltpu.CompilerParams(
            dimension_semantics=("arbitrary",), vmem_limit_bytes=VMEM_LIMIT),
        name="inproj_lru",
    )(x2d, g.reshape(1, D_MODEL), w_bf16, *tabs, conv_w, conv_b.reshape(1, LRU_W),
      gate_w, gate_b, lam.reshape(1, LRU_W), jnp.zeros((1,), jnp.int32))


def _retention_tables():
    c = RET_CHUNK
    log_g = jnp.log1p(-jnp.power(2.0, -5.0 - jnp.arange(RET_HEADS, dtype=F32)))
    idx = jnp.arange(c, dtype=F32)
    rel = idx[:, None] - idx[None, :]
    intra = jnp.where(rel[None] >= 0,
                      jnp.exp(jnp.maximum(rel, 0.0)[None] * log_g[:, None, None]), 0.0)
    k_w = jnp.exp((c - 1.0 - idx)[None, :] * log_g[:, None])
    q_w = jnp.exp((idx + 1.0)[None, :] * log_g[:, None])
    cdec = jnp.exp(c * log_g)
    full = (RET_HEADS, c, HEAD_DIM)
    return (intra.astype(F32),
            jnp.broadcast_to(q_w[:, :, None], full),
            jnp.broadcast_to(k_w[:, :, None], full),
            jnp.broadcast_to(cdec[:, None, None], full))


def _retention_kernel(q_ref, k_ref, v_ref, g_ref, dec_ref, qw_ref, kw_ref, cd_ref,
                      o_ref, state_ref):
    @pl.when(pl.program_id(1) == 0)
    def _():
        state_ref[...] = jnp.zeros_like(state_ref)

    nt = (((1,), (1,)), ((), ()))
    tn = (((0,), (0,)), ((), ()))
    for c in range(RET_ROWS // RET_CHUNK):
        rows = slice(c * RET_CHUNK, (c + 1) * RET_CHUNK)
        for h in range(RET_HEADS):
            cols = slice(h * HEAD_DIM, (h + 1) * HEAD_DIM)
            q = q_ref[rows, cols]
            k = k_ref[rows, cols]
            v = v_ref[rows, cols]
            st = state_ref[h]
            rhs = jnp.concatenate([k, st.astype(BF16)], axis=0)
            r = lax.dot_general(q, rhs, nt, preferred_element_type=F32)
            s = r[:, :RET_CHUNK] * dec_ref[h]
            o = (jnp.dot(s.astype(BF16), v, preferred_element_type=F32)
                 + r[:, RET_CHUNK:] * qw_ref[h])
            ks = (k.astype(F32) * kw_ref[h]).astype(BF16)
            kv_t = lax.dot_general(v, ks, tn, preferred_element_type=F32)
            state_ref[h] = st * cd_ref[h] + kv_t
            ms = jnp.mean(o * o, axis=-1, keepdims=True)
            y = o * lax.rsqrt(ms + EPS)
            o_ref[rows, cols] = (y * _silu(g_ref[rows, cols].astype(F32))).astype(BF16)


def _retention(ret_p, tables, batch, seq):
    t = batch * seq
    steps = seq // RET_ROWS

    def col(j):
        return pl.BlockSpec((RET_ROWS, RET_W), lambda b, i, j=j: (b * steps + i, j))

    tab_spec = pl.BlockSpec((RET_HEADS, RET_CHUNK, HEAD_DIM), lambda b, i: (0, 0, 0))
    return pl.pallas_call(
        _retention_kernel,
        grid=(batch, steps),
        in_specs=[col(0), col(1), col(2), col(3), tab_spec, tab_spec, tab_spec, tab_spec],
        out_specs=pl.BlockSpec((RET_ROWS, RET_W), lambda b, i: (b * steps + i, 0)),
        out_shape=jax.ShapeDtypeStruct((t, RET_W), BF16),
        scratch_shapes=[pltpu.VMEM((RET_HEADS, HEAD_DIM, HEAD_DIM), F32)],
        compiler_params=pltpu.CompilerParams(
            dimension_semantics=("arbitrary", "arbitrary"), vmem_limit_bytes=VMEM_LIMIT),
        name="retention",
    )(ret_p, ret_p, ret_p, ret_p, *tables)


def _diff_kernel(lam_ref, q_ref, k_ref, v_ref, g_ref, subg_ref, *refs, lam_init, cast_next):
    if cast_next:
        win_ref, wout_ref, o_ref, winb_ref, woutb_ref, qm_scr, m_scr, l_scr, acc_scr = refs
        winb_ref[...] = win_ref[...].astype(BF16)
        woutb_ref[...] = wout_ref[...].astype(BF16)
    else:
        o_ref, qm_scr, m_scr, l_scr, acc_scr = refs
    qi = pl.program_id(2)
    q = q_ref[...]
    lane = lax.broadcasted_iota(jnp.int32, q.shape, 1)
    qm_scr[0:ATT_T, :] = jnp.where(lane < DIFF_QK, q, jnp.zeros_like(q))
    qm_scr[ATT_T:2 * ATT_T, :] = jnp.where(lane >= DIFF_QK, q, jnp.zeros_like(q))
    nt = (((1,), (1,)), ((), ()))

    def lane_fold(x, op):
        parts = [x[:, i * LANES:(i + 1) * LANES] for i in range(x.shape[1] // LANES)]
        while len(parts) > 1:
            nxt = [op(parts[i], parts[i + 1]) for i in range(0, len(parts) - 1, 2)]
            parts = nxt + parts[len(parts) - len(parts) % 2:]
        return parts[0]

    def block(r0, size, diagonal):
        k = k_ref[pl.ds(r0, size), :]
        v = v_ref[pl.ds(r0, size), :]
        chunks = []
        for i in range(2 * ATT_T // ATT_ROWS):
            rows = slice(i * ATT_ROWS, (i + 1) * ATT_ROWS)
            keys = (i * ATT_ROWS) % ATT_T + ATT_ROWS if diagonal else size
            s = lax.dot_general(qm_scr[rows, :], k[:keys, :], nt, preferred_element_type=F32)
            chunks.append((rows, keys, s))
        for rows, keys, s in chunks:
            if diagonal:
                tail = s[:, keys - ATT_ROWS:]
                row = lax.broadcasted_iota(jnp.int32, tail.shape, 0)
                col = lax.broadcasted_iota(jnp.int32, tail.shape, 1)
                tail = jnp.where(col <= row, tail, -jnp.inf)
                s = tail if keys == ATT_ROWS else jnp.concatenate(
                    [s[:, :keys - ATT_ROWS], tail], axis=1)
            m_cur = jnp.max(lane_fold(s, jnp.maximum), axis=-1, keepdims=True)
            if diagonal:
                m_new = jnp.broadcast_to(m_cur, (ATT_ROWS, LANES))
            else:
                m_prev = m_scr[rows, :]
                m_new = jnp.maximum(m_prev, m_cur)
                alpha = jnp.exp2(m_prev - m_new)
            p = jnp.exp2(s - jnp.tile(m_new, (1, keys // LANES)))
            l_new = lane_fold(p, jnp.add)
            acc_new = jnp.dot(p.astype(BF16), v[:keys, :], preferred_element_type=F32)
            if not diagonal:
                l_new = alpha * l_scr[rows, :] + l_new
                acc_new = alpha * acc_scr[rows, :] + acc_new
            l_scr[rows, :] = l_new
            acc_scr[rows, :] = acc_new
            m_scr[rows, :] = m_new

    def body(kk, carry):
        block(pl.multiple_of(kk * ATT_TK, ATT_TK), ATT_TK, False)
        return carry

    below = qi * ATT_T
    block(pl.multiple_of(below, ATT_T), ATT_T, True)
    n_big = below // ATT_TK
    lax.fori_loop(0, n_big, body, 0)
    done = n_big * ATT_TK
    size = ATT_TK // 2
    while size >= ATT_T:
        take = below - done >= size

        @pl.when(take)
        def _(done=done, size=size):
            block(pl.multiple_of(done, ATT_T), size, False)

        done = done + jnp.where(take, size, 0)
        size //= 2

    lv = lam_ref[...]
    lam = (jnp.exp(jnp.sum(lv[0:1] * lv[1:2], axis=-1, keepdims=True))
           - jnp.exp(jnp.sum(lv[2:3] * lv[3:4], axis=-1, keepdims=True)) + lam_init)
    l0 = jnp.sum(l_scr[0:ATT_T, :], axis=-1, keepdims=True)
    l1 = jnp.sum(l_scr[ATT_T:2 * ATT_T, :], axis=-1, keepdims=True)
    o = acc_scr[0:ATT_T, :] / l0 - lam * (acc_scr[ATT_T:2 * ATT_T, :] / l1)
    ms = jnp.mean(o * o, axis=-1, keepdims=True)
    y = ((o * lax.rsqrt(ms + EPS)) * subg_ref[...]) * (1.0 - lam_init)
    o_ref[...] = (y * _silu(g_ref[...].astype(F32))).astype(BF16)


def _diff_attention(diff_p, lam_vecs, subg, lam_init, batch, seq, next_weights=None):
    t = batch * seq
    nq = seq // ATT_T
    steps = batch * DIFF_HEADS * nq
    in_specs = [
        pl.BlockSpec((4, DIFF_QK), lambda b, h, i: (0, 0)),
        pl.BlockSpec((ATT_T, HEAD_DIM), lambda b, h, i: (b * nq + i, h)),
        pl.BlockSpec((seq, HEAD_DIM), lambda b, h, i: (b, DIFF_HEADS + h)),
        pl.BlockSpec((seq, HEAD_DIM), lambda b, h, i: (b, 2 * DIFF_HEADS + h)),
        pl.BlockSpec((ATT_T, HEAD_DIM), lambda b, h, i: (b * nq + i, 3 * DIFF_HEADS + h)),
        pl.BlockSpec((1, HEAD_DIM), lambda b, h, i: (0, 0)),
    ]
    out_specs = [pl.BlockSpec((ATT_T, HEAD_DIM), lambda b, h, i: (b * nq + i, h))]
    out_shape = [jax.ShapeDtypeStruct((t, DIFF_W), BF16)]
    args = [lam_vecs, diff_p, diff_p, diff_p, diff_p, subg.reshape(1, HEAD_DIM)]
    if next_weights is not None:
        w_in, w_out, layer = next_weights

        def slab(b, h, i):
            return (b * DIFF_HEADS + h) * nq + i

        for w in (w_in, w_out):
            rows = w.shape[1] // steps
            in_specs.append(pl.BlockSpec((None, rows, w.shape[2]),
                                         lambda b, h, i: (layer, slab(b, h, i), 0)))
            out_specs.append(pl.BlockSpec((rows, w.shape[2]),
                                          lambda b, h, i: (slab(b, h, i), 0)))
            out_shape.append(jax.ShapeDtypeStruct(w.shape[1:], BF16))
            args.append(w)
    return pl.pallas_call(
        functools.partial(_diff_kernel, lam_init=lam_init,
                          cast_next=next_weights is not None),
        grid=(batch, DIFF_HEADS, nq),
        in_specs=in_specs,
        out_specs=out_specs,
        out_shape=out_shape,
        scratch_shapes=[pltpu.VMEM((2 * ATT_T, HEAD_DIM), BF16),
                        pltpu.VMEM((2 * ATT_T, LANES), F32),
                        pltpu.VMEM((2 * ATT_T, LANES), F32),
                        pltpu.VMEM((2 * ATT_T, HEAD_DIM), F32)],
        compiler_params=pltpu.CompilerParams(
            dimension_semantics=("arbitrary", "arbitrary", "arbitrary"),
            vmem_limit_bytes=VMEM_LIMIT),
        name="diff_attention",
    )(*args)


def _outproj_kernel(ret_ref, diff_ref, lru_ref, w_ref, g_ref, x_ref, o_ref):
    rows = x_ref.shape[0] // OUT_GROUPS
    for i in range(OUT_GROUPS):
        rs = slice(i * rows, (i + 1) * rows)
        y = jnp.dot(ret_ref[rs, :], w_ref[0:RET_W, :], preferred_element_type=F32)
        y = y + jnp.dot(diff_ref[rs, :], w_ref[RET_W:RET_W + DIFF_W, :],
                        preferred_element_type=F32)
        y = y + jnp.dot(lru_ref[rs, :], w_ref[RET_W + DIFF_W:D_MIX, :],
                        preferred_element_type=F32)
        ms = jnp.mean(y * y, axis=-1, keepdims=True)
        o_ref[rs, :] = x_ref[rs, :] + (y * lax.rsqrt(ms + EPS)) * g_ref[...]


def _outproj(ret_o, diff_o, lru_o, w_bf16, g, x2d):
    t = x2d.shape[0]
    tm = OUT_TM
    return pl.pallas_call(
        _outproj_kernel,
        grid=(t // tm,),
        in_specs=[pl.BlockSpec((tm, RET_W), lambda i: (i, 0)),
                  pl.BlockSpec((tm, DIFF_W), lambda i: (i, 0)),
                  pl.BlockSpec((tm, LRU_W), lambda i: (i, 0)),
                  pl.BlockSpec((D_MIX, D_MODEL), lambda i: (0, 0),
                               pipeline_mode=pl.Buffered(1)),
                  pl.BlockSpec((1, D_MODEL), lambda i: (0, 0)),
                  pl.BlockSpec((tm, D_MODEL), lambda i: (i, 0))],
        out_specs=pl.BlockSpec((tm, D_MODEL), lambda i: (i, 0)),
        out_shape=jax.ShapeDtypeStruct((t, D_MODEL), F32),
        compiler_params=pltpu.CompilerParams(
            dimension_semantics=("arbitrary",), vmem_limit_bytes=VMEM_LIMIT),
        name="outproj",
    )(ret_o, diff_o, lru_o, w_bf16, g.reshape(1, D_MODEL), x2d)


def kernel(x, positions, pre_norm_g, w_in, diff_lambda_q1, diff_lambda_k1, diff_lambda_q2,
           diff_lambda_k2, diff_subln_g, lru_conv_w, lru_conv_b, lru_wa, lru_ba, lru_wx,
           lru_bx, lru_lambda, w_out, post_norm_g):
    batch, seq, _ = x.shape
    depth = w_in.shape[0]
    assert seq % ATT_T == 0 and seq % RET_ROWS == 0 and seq % INPROJ_TM == 0
    bf16_rows = 2 * SUBLANES
    for steps in (batch * seq // ROPE_TM, batch * DIFF_HEADS * (seq // ATT_T)):
        assert D_MODEL % (steps * bf16_rows) == 0 and D_MIX % (steps * bf16_rows) == 0
    x2d = x.reshape(batch * seq, D_MODEL)
    tabs, w_in_b, w_out_b = _rope_tables(positions, w_in, w_out)
    ret_tabs = _retention_tables()
    for l in range(depth):
        ret_p, diff_p, lru_o = _inproj(
            x2d, pre_norm_g[l], w_in_b, tabs, lru_conv_w[l], lru_conv_b[l], lru_wa[l],
            lru_ba[l], lru_wx[l], lru_bx[l], lru_lambda[l], seq)
        ret_o = _retention(ret_p, ret_tabs, batch, seq)
        lam_vecs = jnp.stack([diff_lambda_q1[l], diff_lambda_k1[l],
                              diff_lambda_q2[l], diff_lambda_k2[l]])
        lam_init = 0.8 - 0.6 * math.exp(-0.3 * l)
        w_out_cur = w_out_b
        if l + 1 < depth:
            diff_o, w_in_b, w_out_b = _diff_attention(
                diff_p, lam_vecs, diff_subln_g[l], lam_init, batch, seq, (w_in, w_out, l + 1))
        else:
            diff_o, = _diff_attention(diff_p, lam_vecs, diff_subln_g[l], lam_init, batch, seq)
        x2d = _outproj(ret_o, diff_o, lru_o, w_out_cur, post_norm_g[l], x2d)
    return x2d.reshape(batch, seq, D_MODEL)
```

```python
import functools
import math

import jax
import jax.numpy as jnp
import numpy as np
from jax import lax
from jax.experimental import pallas as pl
from jax.experimental.pallas import tpu as pltpu

F32 = jnp.float32
BF16 = jnp.bfloat16

D_MODEL = 2048
HEAD_DIM = 128
RET_HEADS = 6
RET_W = RET_HEADS * HEAD_DIM
RET_CHUNK = 128
DIFF_HEADS = 4
DIFF_QK = HEAD_DIM // 2
DIFF_W = DIFF_HEADS * HEAD_DIM
LRU_W = 768
LRU_BLOCKS = 8
LRU_BW = LRU_W // LRU_BLOCKS
CONV_W = 4
LRU_C = 8.0
D_MIX = RET_W + DIFF_W + LRU_W
RET_COLS = 4 * RET_W
DIFF_COLS = 4 * DIFF_W
LRU_COLS = 2 * LRU_W
D_IN = RET_COLS + DIFF_COLS + LRU_COLS
ROPE_THETA = 10000.0
EPS = 1e-6
LOG2E = math.log2(math.e)

LANES = 128
SUBLANES = 8
VMEM_LIMIT = 52 * 1024 * 1024

ROPE_TM = 1024
INPROJ_TM = 256
INPROJ_TN = 512
RET_ROWS = 1024
ATT_T = 1024
ATT_TK = 2048
ATT_ROWS = 256
ATT_AHEAD = 4
LRU_GROUPS = 4
OUT_TM = 512
OUT_GROUPS = 2


def _silu(x):
    return x * jax.nn.sigmoid(x)


def _rope_table_kernel(pos_ref, c_ref, win_ref, wout_ref,
                       cos_r, sin_r, cos_d, sin_d, winb_ref, woutb_ref):
    winb_ref[...] = win_ref[...].astype(BF16)
    woutb_ref[...] = wout_ref[...].astype(BF16)
    pos = pos_ref[...].astype(F32)
    ang = pos * c_ref[0:1, :]
    c = jnp.cos(ang)
    s = jnp.sin(ang)
    lane = lax.broadcasted_iota(jnp.int32, ang.shape, 1)
    half = HEAD_DIM // 2
    q = DIFF_QK // 2
    low = lane < half
    cos_r[...] = jnp.where(low, c, pltpu.roll(c, half, 1))
    sin_r[...] = jnp.where(low, s, pltpu.roll(s, half, 1)) * c_ref[1:2, :]

    def spread(x):
        return jnp.where(lane < q, pltpu.roll(x, half, 1),
                         jnp.where(lane < 2 * q, pltpu.roll(x, half + q, 1),
                                   jnp.where(lane < 3 * q, x, pltpu.roll(x, q, 1))))

    cos_d[...] = spread(c)
    sin_d[...] = spread(s) * c_ref[2:3, :]


def _rope_tables(positions, w_in, w_out):
    t = positions.size
    steps = t // ROPE_TM
    half = HEAD_DIM // 2
    inv_r = 1.0 / (ROPE_THETA ** jnp.linspace(0.0, 1.0, half, dtype=F32))
    inv_d = 1.0 / (ROPE_THETA ** (jnp.arange(0, DIFF_QK, 2, dtype=F32) / DIFF_QK))
    one = jnp.ones((DIFF_QK // 2,), F32)
    consts = jnp.stack([
        jnp.concatenate([inv_r, inv_d, jnp.zeros((LANES - half - DIFF_QK // 2,), F32)]),
        jnp.concatenate([-jnp.ones((half,), F32), jnp.ones((half,), F32)]),
        jnp.concatenate([-one, one, -one, one]),
    ])
    consts = jnp.concatenate([consts, jnp.zeros((SUBLANES - 3, LANES), F32)])
    tab = jax.ShapeDtypeStruct((t, LANES), F32)
    row_spec = pl.BlockSpec((ROPE_TM, LANES), lambda i: (i, 0))
    w_specs_in, w_specs_out, w_shapes = [], [], []
    for w in (w_in, w_out):
        rows = w.shape[1] // steps
        w_specs_in.append(pl.BlockSpec((None, rows, w.shape[2]), lambda i: (0, i, 0)))
        w_specs_out.append(pl.BlockSpec((rows, w.shape[2]), lambda i: (i, 0)))
        w_shapes.append(jax.ShapeDtypeStruct(w.shape[1:], BF16))
    outs = pl.pallas_call(
        _rope_table_kernel,
        grid=(steps,),
        in_specs=[pl.BlockSpec((ROPE_TM, 1), lambda i: (i, 0)),
                  pl.BlockSpec((SUBLANES, LANES), lambda i: (0, 0))] + w_specs_in,
        out_specs=[row_spec] * 4 + w_specs_out,
        out_shape=[tab] * 4 + w_shapes,
        compiler_params=pltpu.CompilerParams(
            dimension_semantics=("arbitrary",), vmem_limit_bytes=VMEM_LIMIT),
        name="rope_tables",
    )(positions.reshape(t, 1), consts, w_in, w_out)
    return outs[:4], outs[4], outs[5]


def _rope_r(x, cos, sin):
    return x * cos + pltpu.roll(x, HEAD_DIM // 2, 1) * sin


def _rope_d(x, cos, sin, first_half):
    q = DIFF_QK // 2
    partner = jnp.where(first_half, pltpu.roll(x, LANES - q, 1), pltpu.roll(x, q, 1))
    return x * cos + partner * sin


def _inproj_kernel(x_ref, g_ref, w_ref, cr_ref, sr_ref, cd_ref, sd_ref,
                   cw_ref, cb_ref, gw_ref, gb_ref, lam_ref, never_ref,
                   ret_ref, diff_ref, lru_ref,
                   xn_scr, xs_scr, hc_scr, hs_scr, *, steps_per_seq):
    rows = x_ref.shape[0]
    pad = SUBLANES
    seq_start = pl.program_id(0) % steps_per_seq == 0

    @pl.when(seq_start)
    def _():
        xs_scr[0:pad, :] = jnp.zeros((pad, LRU_W), F32)
        hc_scr[...] = jnp.zeros_like(hc_scr)

    @pl.when(jnp.logical_not(seq_start))
    def _():
        xs_scr[0:pad, :] = xs_scr[rows:rows + pad, :]

    x = x_ref[...]
    ms = jnp.mean(x * x, axis=-1, keepdims=True)
    xn_scr[...] = ((x * lax.rsqrt(ms + EPS)) * g_ref[...]).astype(BF16)
    lane = lax.broadcasted_iota(jnp.int32, (rows, LANES), 1)
    first_half = (lane % DIFF_QK) < (DIFF_QK // 2)

    def proj(c0, width):
        return jnp.dot(xn_scr[...], w_ref[:, c0:c0 + width], preferred_element_type=F32)

    def emit(blk):
        c0 = blk * INPROJ_TN
        acc = proj(c0, INPROJ_TN)
        for s in range(INPROJ_TN // LANES):
            c = c0 + s * LANES
            y = acc[:, s * LANES:(s + 1) * LANES]
            if c < 2 * RET_W:
                y = _rope_r(y, cr_ref[...], sr_ref[...])
                if c >= RET_W:
                    y = y * (HEAD_DIM ** -0.5)
            elif RET_COLS <= c < RET_COLS + 2 * DIFF_W:
                y = _rope_d(y, cd_ref[...], sd_ref[...], first_half)
                if c < RET_COLS + DIFF_W:
                    y = y * (DIFF_QK ** -0.5 * LOG2E)
            y = y.astype(BF16)
            if c < RET_COLS:
                ret_ref[:, c:c + LANES] = y
            else:
                diff_ref[:, c - RET_COLS:c - RET_COLS + LANES] = y
        return acc[rows - SUBLANES:rows, INPROJ_TN - LANES:INPROJ_TN]

    lru0 = RET_COLS + DIFF_COLS
    xs_scr[pad:pad + rows, :] = proj(lru0, LRU_W)
    lg = proj(lru0 + LRU_W, LRU_W)
    xc = cb_ref[...] + xs_scr[pad - 3:pad - 3 + rows, :] * cw_ref[0:1, :]
    for j in range(1, CONV_W):
        xc = xc + xs_scr[pad - 3 + j:pad - 3 + j + rows, :] * cw_ref[j:j + 1, :]

    emit(0)
    emit(1)

    gates = jnp.dot(xc.astype(BF16), gw_ref[...], preferred_element_type=F32) + gb_ref[...]
    lam = lam_ref[...]
    log_sig = jnp.minimum(lam, 0.0) - jnp.log1p(jnp.exp(-jnp.abs(lam)))
    row = lax.broadcasted_iota(jnp.int32, (SUBLANES, LRU_W), 0)

    def lru_rows(rs, carry, after):
        never = never_ref[0] != 0
        g = gates[rs, :]
        g = jnp.where(never, jnp.tile(after, (g.shape[0] // SUBLANES, g.shape[1] // LANES)), g)
        r = jax.nn.sigmoid(g[:, :LRU_W])
        i = jax.nn.sigmoid(g[:, LRU_W:])
        log_a = (LRU_C * r) * log_sig
        a = jnp.exp(log_a)
        mult = jnp.sqrt(-jnp.tanh(log_a) * (a * a + 1.0))
        b = mult * (i * xc[rs, :])
        for t in range((rs.stop - rs.start) // SUBLANES):
            av = a[t * SUBLANES:(t + 1) * SUBLANES, :]
            bv = b[t * SUBLANES:(t + 1) * SUBLANES, :]
            for d in (1, 2, 4):
                a_prev = jnp.where(row >= d, pltpu.roll(av, d, 0), 1.0)
                b_prev = jnp.where(row >= d, pltpu.roll(bv, d, 0), 0.0)
                bv = av * b_prev + bv
                av = av * a_prev
            hv = av * carry + bv
            r0 = rs.start + t * SUBLANES
            hs_scr[r0:r0 + SUBLANES, :] = hv
            carry = jnp.broadcast_to(hv[SUBLANES - 1:SUBLANES, :], (SUBLANES, LRU_W))
        lru_ref[rs, :] = (hs_scr[rs, :] * _silu(lg[rs, :])).astype(BF16)
        return carry

    carry = hc_scr[...]
    blk = 2
    for gi in range(LRU_GROUPS):
        after = emit(blk)
        blk += 1
        carry = lru_rows(slice(gi * rows // LRU_GROUPS, (gi + 1) * rows // LRU_GROUPS),
                         carry, after)
    hc_scr[...] = carry
    while blk < lru0 // INPROJ_TN:
        emit(blk)
        blk += 1


def _block_diag(w):
    n, k, _ = w.shape
    eye = jnp.eye(n, dtype=w.dtype)
    return (w[:, :, None, :] * eye[:, None, :, None]).reshape(n * k, n * k)


def _inproj(x2d, g, w_bf16, tabs, conv_w, conv_b, wa, ba, wx, bx, lam, seq):
    t = x2d.shape[0]
    tm = INPROJ_TM
    gate_w = jnp.concatenate([_block_diag(wa), _block_diag(wx)], axis=1).astype(BF16)
    gate_b = jnp.concatenate([ba, bx]).reshape(1, 2 * LRU_W)
    tab_spec = pl.BlockSpec((tm, LANES), lambda i: (i, 0))

    def const(shape):
        return pl.BlockSpec(shape, lambda i: (0, 0))

    return pl.pallas_call(
        functools.partial(_inproj_kernel, steps_per_seq=seq // tm),
        grid=(t // tm,),
        in_specs=[pl.BlockSpec((tm, D_MODEL), lambda i: (i, 0)),
                  const((1, D_MODEL)),
                  pl.BlockSpec((D_MODEL, D_IN), lambda i: (0, 0),
                               pipeline_mode=pl.Buffered(1)),
                  tab_spec, tab_spec, tab_spec, tab_spec,
                  const((CONV_W, LRU_W)), const((1, LRU_W)),
                  pl.BlockSpec((LRU_W, 2 * LRU_W), lambda i: (0, 0),
                               pipeline_mode=pl.Buffered(1)),
                  const((1, 2 * LRU_W)), const((1, LRU_W)),
                  pl.BlockSpec(memory_space=pltpu.SMEM)],
        out_specs=[pl.BlockSpec((tm, RET_COLS), lambda i: (i, 0)),
                   pl.BlockSpec((tm, DIFF_COLS), lambda i: (i, 0)),
                   pl.BlockSpec((tm, LRU_W), lambda i: (i, 0))],
        out_shape=[jax.ShapeDtypeStruct((t, RET_COLS), BF16),
                   jax.ShapeDtypeStruct((t, DIFF_COLS), BF16),
                   jax.ShapeDtypeStruct((t, LRU_W), BF16)],
        scratch_shapes=[pltpu.VMEM((tm, D_MODEL), BF16),
                        pltpu.VMEM((tm + SUBLANES, LRU_W), F32),
                        pltpu.VMEM((SUBLANES, LRU_W), F32),
                        pltpu.VMEM((tm, LRU_W), F32)],
        compiler_params=pltpu.CompilerParams(
            dimension_semantics=("arbitrary",), vmem_limit_bytes=VMEM_LIMIT),
        name="inproj_lru",
    )(x2d, g.reshape(1, D_MODEL), w_bf16, *tabs, conv_w, conv_b.reshape(1, LRU_W),
      gate_w, gate_b, lam.reshape(1, LRU_W), jnp.zeros((1,), jnp.int32))


def _retention_tables():
    c = RET_CHUNK
    f32 = np.float32
    log_g = np.log1p(-np.power(f32(2.0), f32(-5.0) - np.arange(RET_HEADS, dtype=f32)))
    idx = np.arange(c, dtype=f32)
    rel = idx[:, None] - idx[None, :]
    intra = np.where(rel[None] >= 0,
                     np.exp(np.maximum(rel, f32(0.0))[None] * log_g[:, None, None]), f32(0.0))
    k_w = np.exp((f32(c - 1.0) - idx)[None, :] * log_g[:, None])
    q_w = np.exp((idx + f32(1.0))[None, :] * log_g[:, None])
    cdec = np.exp(f32(c) * log_g)
    full = (RET_HEADS, c, HEAD_DIM)
    return tuple(jnp.asarray(np.ascontiguousarray(t), dtype=F32) for t in (
        intra,
        np.broadcast_to(q_w[:, :, None], full),
        np.broadcast_to(k_w[:, :, None], full),
        np.broadcast_to(cdec[:, None, None], full)))


def _retention_kernel(q_ref, k_ref, v_ref, g_ref, dec_ref, qw_ref, kw_ref, cd_ref,
                      o_ref, state_ref):
    @pl.when(pl.program_id(1) == 0)
    def _():
        state_ref[...] = jnp.zeros_like(state_ref)

    nt = (((1,), (1,)), ((), ()))
    tn = (((0,), (0,)), ((), ()))
    for c in range(RET_ROWS // RET_CHUNK):
        rows = slice(c * RET_CHUNK, (c + 1) * RET_CHUNK)
        for h in range(RET_HEADS):
            cols = slice(h * HEAD_DIM, (h + 1) * HEAD_DIM)
            q = q_ref[rows, cols]
            k = k_ref[rows, cols]
            v = v_ref[rows, cols]
            st = state_ref[h]
            rhs = jnp.concatenate([k, st.astype(BF16)], axis=0)
            r = lax.dot_general(q, rhs, nt, preferred_element_type=F32)
            s = r[:, :RET_CHUNK] * dec_ref[h]
            o = (jnp.dot(s.astype(BF16), v, preferred_element_type=F32)
                 + r[:, RET_CHUNK:] * qw_ref[h])
            ks = (k.astype(F32) * kw_ref[h]).astype(BF16)
            kv_t = lax.dot_general(v, ks, tn, preferred_element_type=F32)
            state_ref[h] = st * cd_ref[h] + kv_t
            ms = jnp.mean(o * o, axis=-1, keepdims=True)
            y = o * lax.rsqrt(ms + EPS)
            o_ref[rows, cols] = (y * _silu(g_ref[rows, cols].astype(F32))).astype(BF16)


def _retention(ret_p, tables, batch, seq):
    t = batch * seq
    steps = seq // RET_ROWS

    def col(j):
        return pl.BlockSpec((RET_ROWS, RET_W), lambda b, i, j=j: (b * steps + i, j))

    tab_spec = pl.BlockSpec((RET_HEADS, RET_CHUNK, HEAD_DIM), lambda b, i: (0, 0, 0))
    return pl.pallas_call(
        _retention_kernel,
        grid=(batch, steps),
        in_specs=[col(0), col(1), col(2), col(3), tab_spec, tab_spec, tab_spec, tab_spec],
        out_specs=pl.BlockSpec((RET_ROWS, RET_W), lambda b, i: (b * steps + i, 0)),
        out_shape=jax.ShapeDtypeStruct((t, RET_W), BF16),
        scratch_shapes=[pltpu.VMEM((RET_HEADS, HEAD_DIM, HEAD_DIM), F32)],
        compiler_params=pltpu.CompilerParams(
            dimension_semantics=("arbitrary", "arbitrary"), vmem_limit_bytes=VMEM_LIMIT),
        name="retention",
    )(ret_p, ret_p, ret_p, ret_p, *tables)


def _diff_kernel(lam_ref, q_ref, k_ref, v_ref, g_ref, subg_ref, *refs, lam_init, cast_next):
    if cast_next:
        win_ref, wout_ref, o_ref, winb_ref, woutb_ref, qm_scr, m_scr, l_scr, acc_scr = refs
        winb_ref[...] = win_ref[...].astype(BF16)
        woutb_ref[...] = wout_ref[...].astype(BF16)
    else:
        o_ref, qm_scr, m_scr, l_scr, acc_scr = refs
    qi = pl.program_id(2)
    q = q_ref[...]
    lane = lax.broadcasted_iota(jnp.int32, q.shape, 1)
    qm_scr[0:ATT_T, :] = jnp.where(lane < DIFF_QK, q, jnp.zeros_like(q))
    qm_scr[ATT_T:2 * ATT_T, :] = jnp.where(lane >= DIFF_QK, q, jnp.zeros_like(q))
    nt = (((1,), (1,)), ((), ()))

    def lane_fold(x, op):
        parts = [x[:, i * LANES:(i + 1) * LANES] for i in range(x.shape[1] // LANES)]
        while len(parts) > 1:
            nxt = [op(parts[i], parts[i + 1]) for i in range(0, len(parts) - 1, 2)]
            parts = nxt + parts[len(parts) - len(parts) % 2:]
        return parts[0]

    def block(r0, size, diagonal):
        k = k_ref[pl.ds(r0, size), :]
        v = v_ref[pl.ds(r0, size), :]
        n_chunks = 2 * ATT_T // ATT_ROWS

        def scores(i):
            rows = slice(i * ATT_ROWS, (i + 1) * ATT_ROWS)
            keys = (i * ATT_ROWS) % ATT_T + ATT_ROWS if diagonal else size
            s = lax.dot_general(qm_scr[rows, :], k[:keys, :], nt, preferred_element_type=F32)
            return rows, keys, s

        queued = [scores(i) for i in range(min(ATT_AHEAD, n_chunks))]
        for i in range(n_chunks):
            rows, keys, s = queued.pop(0)
            if i + ATT_AHEAD < n_chunks:
                queued.append(scores(i + ATT_AHEAD))
            if diagonal:
                tail = s[:, keys - ATT_ROWS:]
                row = lax.broadcasted_iota(jnp.int32, tail.shape, 0)
                col = lax.broadcasted_iota(jnp.int32, tail.shape, 1)
                tail = jnp.where(col <= row, tail, -jnp.inf)
                s = tail if keys == ATT_ROWS else jnp.concatenate(
                    [s[:, :keys - ATT_ROWS], tail], axis=1)
            m_cur = jnp.max(lane_fold(s, jnp.maximum), axis=-1, keepdims=True)
            if diagonal:
                m_new = jnp.broadcast_to(m_cur, (ATT_ROWS, LANES))
            else:
                m_prev = m_scr[rows, :]
                m_new = jnp.maximum(m_prev, m_cur)
                alpha = jnp.exp2(m_prev - m_new)
            p = jnp.exp2(s - jnp.tile(m_new, (1, keys // LANES)))
            l_new = lane_fold(p, jnp.add)
            acc_new = jnp.dot(p.astype(BF16), v[:keys, :], preferred_element_type=F32)
            if not diagonal:
                l_new = alpha * l_scr[rows, :] + l_new
                acc_new = alpha * acc_scr[rows, :] + acc_new
            l_scr[rows, :] = l_new
            acc_scr[rows, :] = acc_new
            m_scr[rows, :] = m_new

    def body(kk, carry):
        block(pl.multiple_of(kk * ATT_TK, ATT_TK), ATT_TK, False)
        return carry

    below = qi * ATT_T
    block(pl.multiple_of(below, ATT_T), ATT_T, True)
    n_big = below // ATT_TK
    lax.fori_loop(0, n_big, body, 0)
    done = n_big * ATT_TK
    size = ATT_TK // 2
    while size >= ATT_T:
        take = below - done >= size

        @pl.when(take)
        def _(done=done, size=size):
            block(pl.multiple_of(done, ATT_T), size, False)

        done = done + jnp.where(take, size, 0)
        size //= 2

    lv = lam_ref[...]
    lam = (jnp.exp(jnp.sum(lv[0:1] * lv[1:2], axis=-1, keepdims=True))
           - jnp.exp(jnp.sum(lv[2:3] * lv[3:4], axis=-1, keepdims=True)) + lam_init)
    l0 = jnp.sum(l_scr[0:ATT_T, :], axis=-1, keepdims=True)
    l1 = jnp.sum(l_scr[ATT_T:2 * ATT_T, :], axis=-1, keepdims=True)
    o = acc_scr[0:ATT_T, :] / l0 - lam * (acc_scr[ATT_T:2 * ATT_T, :] / l1)
    ms = jnp.mean(o * o, axis=-1, keepdims=True)
    y = ((o * lax.rsqrt(ms + EPS)) * subg_ref[...]) * (1.0 - lam_init)
    o_ref[...] = (y * _silu(g_ref[...].astype(F32))).astype(BF16)


def _diff_attention(diff_p, lam_vecs, subg, lam_init, batch, seq, next_weights=None):
    t = batch * seq
    nq = seq // ATT_T
    steps = batch * DIFF_HEADS * nq
    in_specs = [
        pl.BlockSpec((4, DIFF_QK), lambda b, h, i: (0, 0)),
        pl.BlockSpec((ATT_T, HEAD_DIM), lambda b, h, i: (b * nq + i, h)),
        pl.BlockSpec((seq, HEAD_DIM), lambda b, h, i: (b, DIFF_HEADS + h)),
        pl.BlockSpec((seq, HEAD_DIM), lambda b, h, i: (b, 2 * DIFF_HEADS + h)),
        pl.BlockSpec((ATT_T, HEAD_DIM), lambda b, h, i: (b * nq + i, 3 * DIFF_HEADS + h)),
        pl.BlockSpec((1, HEAD_DIM), lambda b, h, i: (0, 0)),
    ]
    out_specs = [pl.BlockSpec((ATT_T, HEAD_DIM), lambda b, h, i: (b * nq + i, h))]
    out_shape = [jax.ShapeDtypeStruct((t, DIFF_W), BF16)]
    args = [lam_vecs, diff_p, diff_p, diff_p, diff_p, subg.reshape(1, HEAD_DIM)]
    if next_weights is not None:
        w_in, w_out, layer = next_weights

        def slab(b, h, i):
            return (b * DIFF_HEADS + h) * nq + i

        for w in (w_in, w_out):
            rows = w.shape[1] // steps
            in_specs.append(pl.BlockSpec((None, rows, w.shape[2]),
                                         lambda b, h, i: (layer, slab(b, h, i), 0)))
            out_specs.append(pl.BlockSpec((rows, w.shape[2]),
                                          lambda b, h, i: (slab(b, h, i), 0)))
            out_shape.append(jax.ShapeDtypeStruct(w.shape[1:], BF16))
            args.append(w)
    return pl.pallas_call(
        functools.partial(_diff_kernel, lam_init=lam_init,
                          cast_next=next_weights is not None),
        grid=(batch, DIFF_HEADS, nq),
        in_specs=in_specs,
        out_specs=out_specs,
        out_shape=out_shape,
        scratch_shapes=[pltpu.VMEM((2 * ATT_T, HEAD_DIM), BF16),
                        pltpu.VMEM((2 * ATT_T, LANES), F32),
                        pltpu.VMEM((2 * ATT_T, LANES), F32),
                        pltpu.VMEM((2 * ATT_T, HEAD_DIM), F32)],
        compiler_params=pltpu.CompilerParams(
            dimension_semantics=("arbitrary", "arbitrary", "arbitrary"),
            vmem_limit_bytes=VMEM_LIMIT),
        name="diff_attention",
    )(*args)


def _outproj_kernel(ret_ref, diff_ref, lru_ref, w_ref, g_ref, x_ref, o_ref):
    rows = x_ref.shape[0] // OUT_GROUPS
    for i in range(OUT_GROUPS):
        rs = slice(i * rows, (i + 1) * rows)
        y = jnp.dot(ret_ref[rs, :], w_ref[0:RET_W, :], preferred_element_type=F32)
        y = y + jnp.dot(diff_ref[rs, :], w_ref[RET_W:RET_W + DIFF_W, :],
                        preferred_element_type=F32)
        y = y + jnp.dot(lru_ref[rs, :], w_ref[RET_W + DIFF_W:D_MIX, :],
                        preferred_element_type=F32)
        ms = jnp.mean(y * y, axis=-1, keepdims=True)
        o_ref[rs, :] = x_ref[rs, :] + (y * lax.rsqrt(ms + EPS)) * g_ref[...]


def _outproj(ret_o, diff_o, lru_o, w_bf16, g, x2d):
    t = x2d.shape[0]
    tm = OUT_TM
    return pl.pallas_call(
        _outproj_kernel,
        grid=(t // tm,),
        in_specs=[pl.BlockSpec((tm, RET_W), lambda i: (i, 0)),
                  pl.BlockSpec((tm, DIFF_W), lambda i: (i, 0)),
                  pl.BlockSpec((tm, LRU_W), lambda i: (i, 0)),
                  pl.BlockSpec((D_MIX, D_MODEL), lambda i: (0, 0),
                               pipeline_mode=pl.Buffered(1)),
                  pl.BlockSpec((1, D_MODEL), lambda i: (0, 0)),
                  pl.BlockSpec((tm, D_MODEL), lambda i: (i, 0))],
        out_specs=pl.BlockSpec((tm, D_MODEL), lambda i: (i, 0)),
        out_shape=jax.ShapeDtypeStruct((t, D_MODEL), F32),
        compiler_params=pltpu.CompilerParams(
            dimension_semantics=("arbitrary",), vmem_limit_bytes=VMEM_LIMIT),
        name="outproj",
    )(ret_o, diff_o, lru_o, w_bf16, g.reshape(1, D_MODEL), x2d)


def kernel(x, positions, pre_norm_g, w_in, diff_lambda_q1, diff_lambda_k1, diff_lambda_q2,
           diff_lambda_k2, diff_subln_g, lru_conv_w, lru_conv_b, lru_wa, lru_ba, lru_wx,
           lru_bx, lru_lambda, w_out, post_norm_g):
    batch, seq, _ = x.shape
    depth = w_in.shape[0]
    assert seq % ATT_T == 0 and seq % RET_ROWS == 0 and seq % INPROJ_TM == 0
    bf16_rows = 2 * SUBLANES
    for steps in (batch * seq // ROPE_TM, batch * DIFF_HEADS * (seq // ATT_T)):
        assert D_MODEL % (steps * bf16_rows) == 0 and D_MIX % (steps * bf16_rows) == 0
    x2d = x.reshape(batch * seq, D_MODEL)
    tabs, w_in_b, w_out_b = _rope_tables(positions, w_in, w_out)
    ret_tabs = _retention_tables()
    for l in range(depth):
        ret_p, diff_p, lru_o = _inproj(
            x2d, pre_norm_g[l], w_in_b, tabs, lru_conv_w[l], lru_conv_b[l], lru_wa[l],
            lru_ba[l], lru_wx[l], lru_bx[l], lru_lambda[l], seq)
        ret_o = _retention(ret_p, ret_tabs, batch, seq)
        lam_vecs = jnp.stack([diff_lambda_q1[l], diff_lambda_k1[l],
                              diff_lambda_q2[l], diff_lambda_k2[l]])
        lam_init = 0.8 - 0.6 * math.exp(-0.3 * l)
        w_out_cur = w_out_b
        if l + 1 < depth:
            diff_o, w_in_b, w_out_b = _diff_attention(
                diff_p, lam_vecs, diff_subln_g[l], lam_init, batch, seq, (w_in, w_out, l + 1))
        else:
            diff_o, = _diff_attention(diff_p, lam_vecs, diff_subln_g[l], lam_init, batch, seq)
        x2d = _outproj(ret_o, diff_o, lru_o, w_out_cur, post_norm_g[l], x2d)
    return x2d.reshape(batch, seq, D_MODEL)
```

```python
import functools
import math

import jax
import jax.numpy as jnp
import numpy as np
from jax import lax
from jax.experimental import pallas as pl
from jax.experimental.pallas import tpu as pltpu

F32 = jnp.float32
BF16 = jnp.bfloat16

D_MODEL = 2048
HEAD_DIM = 128
RET_HEADS = 6
RET_W = RET_HEADS * HEAD_DIM
RET_CHUNK = 128
DIFF_HEADS = 4
DIFF_QK = HEAD_DIM // 2
DIFF_W = DIFF_HEADS * HEAD_DIM
LRU_W = 768
LRU_BLOCKS = 8
LRU_BW = LRU_W // LRU_BLOCKS
CONV_W = 4
LRU_C = 8.0
D_MIX = RET_W + DIFF_W + LRU_W
RET_COLS = 4 * RET_W
DIFF_COLS = 4 * DIFF_W
LRU_COLS = 2 * LRU_W
D_IN = RET_COLS + DIFF_COLS + LRU_COLS
ROPE_THETA = 10000.0
EPS = 1e-6
LOG2E = math.log2(math.e)

LANES = 128
SUBLANES = 8
VMEM_LIMIT = 52 * 1024 * 1024

ROPE_TM = 1024
INPROJ_TM = 256
INPROJ_TN = 512
RET_ROWS = 1024
ATT_T = 1024
ATT_TK = 2048
ATT_ROWS = 256
ATT_AHEAD = 4
LRU_GROUPS = 4
OUT_TM = 512
OUT_GROUPS = 2


def _silu(x):
    return x * jax.nn.sigmoid(x)


def _rope_table_kernel(pos_ref, c_ref, win_ref, wout_ref,
                       cos_r, sin_r, cos_d, sin_d, winb_ref, woutb_ref):
    winb_ref[...] = win_ref[...].astype(BF16)
    woutb_ref[...] = wout_ref[...].astype(BF16)
    pos = pos_ref[...].astype(F32)
    ang = pos * c_ref[0:1, :]
    c = jnp.cos(ang)
    s = jnp.sin(ang)
    lane = lax.broadcasted_iota(jnp.int32, ang.shape, 1)
    half = HEAD_DIM // 2
    q = DIFF_QK // 2
    low = lane < half
    cos_r[...] = jnp.where(low, c, pltpu.roll(c, half, 1))
    sin_r[...] = jnp.where(low, s, pltpu.roll(s, half, 1)) * c_ref[1:2, :]

    def spread(x):
        return jnp.where(lane < q, pltpu.roll(x, half, 1),
                         jnp.where(lane < 2 * q, pltpu.roll(x, half + q, 1),
                                   jnp.where(lane < 3 * q, x, pltpu.roll(x, q, 1))))

    cos_d[...] = spread(c)
    sin_d[...] = spread(s) * c_ref[2:3, :]


def _rope_tables(positions, w_in, w_out):
    t = positions.size
    steps = t // ROPE_TM
    half = HEAD_DIM // 2
    inv_r = 1.0 / (ROPE_THETA ** jnp.linspace(0.0, 1.0, half, dtype=F32))
    inv_d = 1.0 / (ROPE_THETA ** (jnp.arange(0, DIFF_QK, 2, dtype=F32) / DIFF_QK))
    one = jnp.ones((DIFF_QK // 2,), F32)
    consts = jnp.stack([
        jnp.concatenate([inv_r, inv_d, jnp.zeros((LANES - half - DIFF_QK // 2,), F32)]),
        jnp.concatenate([-jnp.ones((half,), F32), jnp.ones((half,), F32)]),
        jnp.concatenate([-one, one, -one, one]),
    ])
    consts = jnp.concatenate([consts, jnp.zeros((SUBLANES - 3, LANES), F32)])
    tab = jax.ShapeDtypeStruct((t, LANES), F32)
    row_spec = pl.BlockSpec((ROPE_TM, LANES), lambda i: (i, 0))
    w_specs_in, w_specs_out, w_shapes = [], [], []
    for w in (w_in, w_out):
        rows = w.shape[1] // steps
        w_specs_in.append(pl.BlockSpec((None, rows, w.shape[2]), lambda i: (0, i, 0)))
        w_specs_out.append(pl.BlockSpec((rows, w.shape[2]), lambda i: (i, 0)))
        w_shapes.append(jax.ShapeDtypeStruct(w.shape[1:], BF16))
    outs = pl.pallas_call(
        _rope_table_kernel,
        grid=(steps,),
        in_specs=[pl.BlockSpec((ROPE_TM, 1), lambda i: (i, 0)),
                  pl.BlockSpec((SUBLANES, LANES), lambda i: (0, 0))] + w_specs_in,
        out_specs=[row_spec] * 4 + w_specs_out,
        out_shape=[tab] * 4 + w_shapes,
        compiler_params=pltpu.CompilerParams(
            dimension_semantics=("arbitrary",), vmem_limit_bytes=VMEM_LIMIT),
        name="rope_tables",
    )(positions.reshape(t, 1), consts, w_in, w_out)
    return outs[:4], outs[4], outs[5]


def _rope_r(x, cos, sin):
    return x * cos + pltpu.roll(x, HEAD_DIM // 2, 1) * sin


def _rope_d(x, cos, sin, first_half):
    q = DIFF_QK // 2
    partner = jnp.where(first_half, pltpu.roll(x, LANES - q, 1), pltpu.roll(x, q, 1))
    return x * cos + partner * sin


def _inproj_kernel(x_ref, g_ref, w_ref, cr_ref, sr_ref, cd_ref, sd_ref,
                   cw_ref, cb_ref, gw_ref, gb_ref, lam_ref, never_ref,
                   ret_ref, diff_ref, lru_ref,
                   xn_scr, xs_scr, hc_scr, hs_scr, *, steps_per_seq):
    rows = x_ref.shape[0]
    pad = SUBLANES
    seq_start = pl.program_id(0) % steps_per_seq == 0

    @pl.when(seq_start)
    def _():
        xs_scr[0:pad, :] = jnp.zeros((pad, LRU_W), F32)
        hc_scr[...] = jnp.zeros_like(hc_scr)

    @pl.when(jnp.logical_not(seq_start))
    def _():
        xs_scr[0:pad, :] = xs_scr[rows:rows + pad, :]

    x = x_ref[...]
    ms = jnp.mean(x * x, axis=-1, keepdims=True)
    xn_scr[...] = ((x * lax.rsqrt(ms + EPS)) * g_ref[...]).astype(BF16)
    lane = lax.broadcasted_iota(jnp.int32, (rows, LANES), 1)
    first_half = (lane % DIFF_QK) < (DIFF_QK // 2)

    def proj(c0, width):
        return jnp.dot(xn_scr[...], w_ref[:, c0:c0 + width], preferred_element_type=F32)

    def emit(blk):
        c0 = blk * INPROJ_TN
        acc = proj(c0, INPROJ_TN)
        for s in range(INPROJ_TN // LANES):
            c = c0 + s * LANES
            y = acc[:, s * LANES:(s + 1) * LANES]
            if c < 2 * RET_W:
                y = _rope_r(y, cr_ref[...], sr_ref[...])
                if c >= RET_W:
                    y = y * (HEAD_DIM ** -0.5)
            elif RET_COLS <= c < RET_COLS + 2 * DIFF_W:
                y = _rope_d(y, cd_ref[...], sd_ref[...], first_half)
                if c < RET_COLS + DIFF_W:
                    y = y * (DIFF_QK ** -0.5 * LOG2E)
            y = y.astype(BF16)
            if c < RET_COLS:
                ret_ref[:, c:c + LANES] = y
            else:
                diff_ref[:, c - RET_COLS:c - RET_COLS + LANES] = y
        return acc[rows - SUBLANES:rows, INPROJ_TN - LANES:INPROJ_TN]

    lru0 = RET_COLS + DIFF_COLS
    xs_scr[pad:pad + rows, :] = proj(lru0, LRU_W)
    lg = proj(lru0 + LRU_W, LRU_W)
    xc = cb_ref[...] + xs_scr[pad - 3:pad - 3 + rows, :] * cw_ref[0:1, :]
    for j in range(1, CONV_W):
        xc = xc + xs_scr[pad - 3 + j:pad - 3 + j + rows, :] * cw_ref[j:j + 1, :]

    emit(0)
    emit(1)

    gates = jnp.dot(xc.astype(BF16), gw_ref[...], preferred_element_type=F32) + gb_ref[...]
    lam = lam_ref[...]
    log_sig = jnp.minimum(lam, 0.0) - jnp.log1p(jnp.exp(-jnp.abs(lam)))
    row = lax.broadcasted_iota(jnp.int32, (SUBLANES, LRU_W), 0)

    def lru_rows(rs, carry, after):
        never = never_ref[0] != 0
        g = gates[rs, :]
        g = jnp.where(never, jnp.tile(after, (g.shape[0] // SUBLANES, g.shape[1] // LANES)), g)
        r = jax.nn.sigmoid(g[:, :LRU_W])
        i = jax.nn.sigmoid(g[:, LRU_W:])
        log_a = (LRU_C * r) * log_sig
        a = jnp.exp(log_a)
        mult = jnp.sqrt(-jnp.tanh(log_a) * (a * a + 1.0))
        b = mult * (i * xc[rs, :])
        for t in range((rs.stop - rs.start) // SUBLANES):
            av = a[t * SUBLANES:(t + 1) * SUBLANES, :]
            bv = b[t * SUBLANES:(t + 1) * SUBLANES, :]
            for d in (1, 2, 4):
                a_prev = jnp.where(row >= d, pltpu.roll(av, d, 0), 1.0)
                b_prev = jnp.where(row >= d, pltpu.roll(bv, d, 0), 0.0)
                bv = av * b_prev + bv
                av = av * a_prev
            hv = av * carry + bv
            r0 = rs.start + t * SUBLANES
            hs_scr[r0:r0 + SUBLANES, :] = hv
            carry = jnp.broadcast_to(hv[SUBLANES - 1:SUBLANES, :], (SUBLANES, LRU_W))
        lru_ref[rs, :] = (hs_scr[rs, :] * _silu(lg[rs, :])).astype(BF16)
        return carry

    carry = hc_scr[...]
    blk = 2
    for gi in range(LRU_GROUPS):
        after = emit(blk)
        blk += 1
        carry = lru_rows(slice(gi * rows // LRU_GROUPS, (gi + 1) * rows // LRU_GROUPS),
                         carry, after)
    hc_scr[...] = carry
    while blk < lru0 // INPROJ_TN:
        emit(blk)
        blk += 1


def _block_diag(w):
    n, k, _ = w.shape
    eye = jnp.eye(n, dtype=w.dtype)
    return (w[:, :, None, :] * eye[:, None, :, None]).reshape(n * k, n * k)


def _inproj(x2d, g, w_bf16, tabs, conv_w, conv_b, wa, ba, wx, bx, lam, seq):
    t = x2d.shape[0]
    tm = INPROJ_TM
    gate_w = jnp.concatenate([_block_diag(wa), _block_diag(wx)], axis=1).astype(BF16)
    gate_b = jnp.concatenate([ba, bx]).reshape(1, 2 * LRU_W)
    tab_spec = pl.BlockSpec((tm, LANES), lambda i: (i, 0))

    def const(shape):
        return pl.BlockSpec(shape, lambda i: (0, 0))

    return pl.pallas_call(
        functools.partial(_inproj_kernel, steps_per_seq=seq // tm),
        grid=(t // tm,),
        in_specs=[pl.BlockSpec((tm, D_MODEL), lambda i: (i, 0)),
                  const((1, D_MODEL)),
                  pl.BlockSpec((D_MODEL, D_IN), lambda i: (0, 0),
                               pipeline_mode=pl.Buffered(1)),
                  tab_spec, tab_spec, tab_spec, tab_spec,
                  const((CONV_W, LRU_W)), const((1, LRU_W)),
                  pl.BlockSpec((LRU_W, 2 * LRU_W), lambda i: (0, 0),
                               pipeline_mode=pl.Buffered(1)),
                  const((1, 2 * LRU_W)), const((1, LRU_W)),
                  pl.BlockSpec(memory_space=pltpu.SMEM)],
        out_specs=[pl.BlockSpec((tm, RET_COLS), lambda i: (i, 0)),
                   pl.BlockSpec((tm, DIFF_COLS), lambda i: (i, 0)),
                   pl.BlockSpec((tm, LRU_W), lambda i: (i, 0))],
        out_shape=[jax.ShapeDtypeStruct((t, RET_COLS), BF16),
                   jax.ShapeDtypeStruct((t, DIFF_COLS), BF16),
                   jax.ShapeDtypeStruct((t, LRU_W), BF16)],
        scratch_shapes=[pltpu.VMEM((tm, D_MODEL), BF16),
                        pltpu.VMEM((tm + SUBLANES, LRU_W), F32),
                        pltpu.VMEM((SUBLANES, LRU_W), F32),
                        pltpu.VMEM((tm, LRU_W), F32)],
        compiler_params=pltpu.CompilerParams(
            dimension_semantics=("arbitrary",), vmem_limit_bytes=VMEM_LIMIT),
        name="inproj_lru",
    )(x2d, g.reshape(1, D_MODEL), w_bf16, *tabs, conv_w, conv_b.reshape(1, LRU_W),
      gate_w, gate_b, lam.reshape(1, LRU_W), jnp.zeros((1,), jnp.int32))


def _retention_tables():
    c = RET_CHUNK
    f32 = np.float32
    log_g = np.log1p(-np.power(f32(2.0), f32(-5.0) - np.arange(RET_HEADS, dtype=f32)))
    idx = np.arange(c, dtype=f32)
    rel = idx[:, None] - idx[None, :]
    intra = np.where(rel[None] >= 0,
                     np.exp(np.maximum(rel, f32(0.0))[None] * log_g[:, None, None]), f32(0.0))
    k_w = np.exp((f32(c - 1.0) - idx)[None, :] * log_g[:, None])
    q_w = np.exp((idx + f32(1.0))[None, :] * log_g[:, None])
    cdec = np.exp(f32(c) * log_g)
    full = (RET_HEADS, c, HEAD_DIM)
    return tuple(jnp.asarray(np.ascontiguousarray(t), dtype=F32) for t in (
        intra,
        np.broadcast_to(q_w[:, :, None], full),
        np.broadcast_to(k_w[:, :, None], full),
        np.broadcast_to(cdec[:, None, None], full)))


def _retention_kernel(q_ref, k_ref, v_ref, g_ref, dec_ref, qw_ref, kw_ref, cd_ref,
                      o_ref, state_ref):
    @pl.when(pl.program_id(1) == 0)
    def _():
        state_ref[...] = jnp.zeros_like(state_ref)

    nt = (((1,), (1,)), ((), ()))
    tn = (((0,), (0,)), ((), ()))
    for c in range(RET_ROWS // RET_CHUNK):
        rows = slice(c * RET_CHUNK, (c + 1) * RET_CHUNK)
        for h in range(RET_HEADS):
            cols = slice(h * HEAD_DIM, (h + 1) * HEAD_DIM)
            q = q_ref[rows, cols]
            k = k_ref[rows, cols]
            v = v_ref[rows, cols]
            st = state_ref[h]
            rhs = jnp.concatenate([k, st.astype(BF16)], axis=0)
            r = lax.dot_general(q, rhs, nt, preferred_element_type=F32)
            s = r[:, :RET_CHUNK] * dec_ref[h]
            o = (jnp.dot(s.astype(BF16), v, preferred_element_type=F32)
                 + r[:, RET_CHUNK:] * qw_ref[h])
            ks = (k.astype(F32) * kw_ref[h]).astype(BF16)
            kv_t = lax.dot_general(v, ks, tn, preferred_element_type=F32)
            state_ref[h] = st * cd_ref[h] + kv_t
            ms = jnp.mean(o * o, axis=-1, keepdims=True)
            y = o * lax.rsqrt(ms + EPS)
            o_ref[rows, cols] = (y * _silu(g_ref[rows, cols].astype(F32))).astype(BF16)


def _retention(ret_p, tables, batch, seq):
    t = batch * seq
    steps = seq // RET_ROWS

    def col(j):
        return pl.BlockSpec((RET_ROWS, RET_W), lambda b, i, j=j: (b * steps + i, j))

    tab_spec = pl.BlockSpec((RET_HEADS, RET_CHUNK, HEAD_DIM), lambda b, i: (0, 0, 0))
    return pl.pallas_call(
        _retention_kernel,
        grid=(batch, steps),
        in_specs=[col(0), col(1), col(2), col(3), tab_spec, tab_spec, tab_spec, tab_spec],
        out_specs=pl.BlockSpec((RET_ROWS, RET_W), lambda b, i: (b * steps + i, 0)),
        out_shape=jax.ShapeDtypeStruct((t, RET_W), BF16),
        scratch_shapes=[pltpu.VMEM((RET_HEADS, HEAD_DIM, HEAD_DIM), F32)],
        compiler_params=pltpu.CompilerParams(
            dimension_semantics=("arbitrary", "arbitrary"), vmem_limit_bytes=VMEM_LIMIT),
        name="retention",
    )(ret_p, ret_p, ret_p, ret_p, *tables)


def _diff_kernel(lam_ref, q_ref, k_ref, v_ref, g_ref, subg_ref, *refs, lam_init, cast_next):
    if cast_next:
        win_ref, wout_ref, o_ref, winb_ref, woutb_ref, qm_scr, m_scr, acc_scr, v1_scr = refs
        winb_ref[...] = win_ref[...].astype(BF16)
        woutb_ref[...] = wout_ref[...].astype(BF16)
    else:
        o_ref, qm_scr, m_scr, acc_scr, v1_scr = refs
    qi = pl.program_id(2)

    @pl.when(qi == 0)
    def _():
        v1_scr[:, 0:HEAD_DIM] = v_ref[...]
        v1_scr[:, HEAD_DIM:2 * HEAD_DIM] = jnp.ones(v_ref.shape, BF16)

    q = q_ref[...]
    lane = lax.broadcasted_iota(jnp.int32, q.shape, 1)
    qm_scr[0:ATT_T, :] = jnp.where(lane < DIFF_QK, q, jnp.zeros_like(q))
    qm_scr[ATT_T:2 * ATT_T, :] = jnp.where(lane >= DIFF_QK, q, jnp.zeros_like(q))
    nt = (((1,), (1,)), ((), ()))

    def lane_fold(x, op):
        parts = [x[:, i * LANES:(i + 1) * LANES] for i in range(x.shape[1] // LANES)]
        while len(parts) > 1:
            nxt = [op(parts[i], parts[i + 1]) for i in range(0, len(parts) - 1, 2)]
            parts = nxt + parts[len(parts) - len(parts) % 2:]
        return parts[0]

    def block(r0, size, diagonal):
        k = k_ref[pl.ds(r0, size), :]
        v = v1_scr[pl.ds(r0, size), :]
        n_chunks = 2 * ATT_T // ATT_ROWS

        def scores(i):
            rows = slice(i * ATT_ROWS, (i + 1) * ATT_ROWS)
            keys = (i * ATT_ROWS) % ATT_T + ATT_ROWS if diagonal else size
            s = lax.dot_general(qm_scr[rows, :], k[:keys, :], nt, preferred_element_type=F32)
            return rows, keys, s

        queued = [scores(i) for i in range(min(ATT_AHEAD, n_chunks))]
        for i in range(n_chunks):
            rows, keys, s = queued.pop(0)
            if i + ATT_AHEAD < n_chunks:
                queued.append(scores(i + ATT_AHEAD))
            if diagonal:
                tail = s[:, keys - ATT_ROWS:]
                row = lax.broadcasted_iota(jnp.int32, tail.shape, 0)
                col = lax.broadcasted_iota(jnp.int32, tail.shape, 1)
                tail = jnp.where(col <= row, tail, -jnp.inf)
                s = tail if keys == ATT_ROWS else jnp.concatenate(
                    [s[:, :keys - ATT_ROWS], tail], axis=1)
            m_cur = jnp.max(lane_fold(s, jnp.maximum), axis=-1, keepdims=True)
            if diagonal:
                m_new = jnp.broadcast_to(m_cur, (ATT_ROWS, LANES))
            else:
                m_prev = m_scr[rows, :]
                m_new = jnp.maximum(m_prev, m_cur)
                alpha = jnp.exp2(m_prev - m_new)
            p = jnp.exp2(s - jnp.tile(m_new, (1, keys // LANES)))
            acc_new = jnp.dot(p.astype(BF16), v[:keys, :], preferred_element_type=F32)
            if not diagonal:
                acc_new = jnp.tile(alpha, (1, 2)) * acc_scr[rows, :] + acc_new
            acc_scr[rows, :] = acc_new
            m_scr[rows, :] = m_new

    def body(kk, carry):
        block(pl.multiple_of(kk * ATT_TK, ATT_TK), ATT_TK, False)
        return carry

    below = qi * ATT_T
    block(pl.multiple_of(below, ATT_T), ATT_T, True)
    n_big = below // ATT_TK
    lax.fori_loop(0, n_big, body, 0)
    done = n_big * ATT_TK
    size = ATT_TK // 2
    while size >= ATT_T:
        take = below - done >= size

        @pl.when(take)
        def _(done=done, size=size):
            block(pl.multiple_of(done, ATT_T), size, False)

        done = done + jnp.where(take, size, 0)
        size //= 2

    lv = lam_ref[...]
    lam = (jnp.exp(jnp.sum(lv[0:1] * lv[1:2], axis=-1, keepdims=True))
           - jnp.exp(jnp.sum(lv[2:3] * lv[3:4], axis=-1, keepdims=True)) + lam_init)
    o = (acc_scr[0:ATT_T, :HEAD_DIM] / acc_scr[0:ATT_T, HEAD_DIM:]
         - lam * (acc_scr[ATT_T:2 * ATT_T, :HEAD_DIM] / acc_scr[ATT_T:2 * ATT_T, HEAD_DIM:]))
    ms = jnp.mean(o * o, axis=-1, keepdims=True)
    y = ((o * lax.rsqrt(ms + EPS)) * subg_ref[...]) * (1.0 - lam_init)
    o_ref[...] = (y * _silu(g_ref[...].astype(F32))).astype(BF16)


def _diff_attention(diff_p, lam_vecs, subg, lam_init, batch, seq, next_weights=None):
    t = batch * seq
    nq = seq // ATT_T
    steps = batch * DIFF_HEADS * nq
    in_specs = [
        pl.BlockSpec((4, DIFF_QK), lambda b, h, i: (0, 0)),
        pl.BlockSpec((ATT_T, HEAD_DIM), lambda b, h, i: (b * nq + i, h)),
        pl.BlockSpec((seq, HEAD_DIM), lambda b, h, i: (b, DIFF_HEADS + h)),
        pl.BlockSpec((seq, HEAD_DIM), lambda b, h, i: (b, 2 * DIFF_HEADS + h)),
        pl.BlockSpec((ATT_T, HEAD_DIM), lambda b, h, i: (b * nq + i, 3 * DIFF_HEADS + h)),
        pl.BlockSpec((1, HEAD_DIM), lambda b, h, i: (0, 0)),
    ]
    out_specs = [pl.BlockSpec((ATT_T, HEAD_DIM), lambda b, h, i: (b * nq + i, h))]
    out_shape = [jax.ShapeDtypeStruct((t, DIFF_W), BF16)]
    args = [lam_vecs, diff_p, diff_p, diff_p, diff_p, subg.reshape(1, HEAD_DIM)]
    if next_weights is not None:
        w_in, w_out, layer = next_weights

        def slab(b, h, i):
            return (b * DIFF_HEADS + h) * nq + i

        for w in (w_in, w_out):
            rows = w.shape[1] // steps
            in_specs.append(pl.BlockSpec((None, rows, w.shape[2]),
                                         lambda b, h, i: (layer, slab(b, h, i), 0)))
            out_specs.append(pl.BlockSpec((rows, w.shape[2]),
                                          lambda b, h, i: (slab(b, h, i), 0)))
            out_shape.append(jax.ShapeDtypeStruct(w.shape[1:], BF16))
            args.append(w)
    return pl.pallas_call(
        functools.partial(_diff_kernel, lam_init=lam_init,
                          cast_next=next_weights is not None),
        grid=(batch, DIFF_HEADS, nq),
        in_specs=in_specs,
        out_specs=out_specs,
        out_shape=out_shape,
        scratch_shapes=[pltpu.VMEM((2 * ATT_T, HEAD_DIM), BF16),
                        pltpu.VMEM((2 * ATT_T, LANES), F32),
                        pltpu.VMEM((2 * ATT_T, 2 * HEAD_DIM), F32),
                        pltpu.VMEM((seq, 2 * HEAD_DIM), BF16)],
        compiler_params=pltpu.CompilerParams(
            dimension_semantics=("arbitrary", "arbitrary", "arbitrary"),
            vmem_limit_bytes=VMEM_LIMIT),
        name="diff_attention",
    )(*args)


def _outproj_kernel(ret_ref, diff_ref, lru_ref, w_ref, g_ref, x_ref, o_ref):
    rows = x_ref.shape[0] // OUT_GROUPS
    for i in range(OUT_GROUPS):
        rs = slice(i * rows, (i + 1) * rows)
        y = jnp.dot(ret_ref[rs, :], w_ref[0:RET_W, :], preferred_element_type=F32)
        y = y + jnp.dot(diff_ref[rs, :], w_ref[RET_W:RET_W + DIFF_W, :],
                        preferred_element_type=F32)
        y = y + jnp.dot(lru_ref[rs, :], w_ref[RET_W + DIFF_W:D_MIX, :],
                        preferred_element_type=F32)
        ms = jnp.mean(y * y, axis=-1, keepdims=True)
        o_ref[rs, :] = x_ref[rs, :] + (y * lax.rsqrt(ms + EPS)) * g_ref[...]


def _outproj(ret_o, diff_o, lru_o, w_bf16, g, x2d):
    t = x2d.shape[0]
    tm = OUT_TM
    return pl.pallas_call(
        _outproj_kernel,
        grid=(t // tm,),
        in_specs=[pl.BlockSpec((tm, RET_W), lambda i: (i, 0)),
                  pl.BlockSpec((tm, DIFF_W), lambda i: (i, 0)),
                  pl.BlockSpec((tm, LRU_W), lambda i: (i, 0)),
                  pl.BlockSpec((D_MIX, D_MODEL), lambda i: (0, 0),
                               pipeline_mode=pl.Buffered(1)),
                  pl.BlockSpec((1, D_MODEL), lambda i: (0, 0)),
                  pl.BlockSpec((tm, D_MODEL), lambda i: (i, 0))],
        out_specs=pl.BlockSpec((tm, D_MODEL), lambda i: (i, 0)),
        out_shape=jax.ShapeDtypeStruct((t, D_MODEL), F32),
        compiler_params=pltpu.CompilerParams(
            dimension_semantics=("arbitrary",), vmem_limit_bytes=VMEM_LIMIT),
        name="outproj",
    )(ret_o, diff_o, lru_o, w_bf16, g.reshape(1, D_MODEL), x2d)


def kernel(x, positions, pre_norm_g, w_in, diff_lambda_q1, diff_lambda_k1, diff_lambda_q2,
           diff_lambda_k2, diff_subln_g, lru_conv_w, lru_conv_b, lru_wa, lru_ba, lru_wx,
           lru_bx, lru_lambda, w_out, post_norm_g):
    batch, seq, _ = x.shape
    depth = w_in.shape[0]
    assert seq % ATT_T == 0 and seq % RET_ROWS == 0 and seq % INPROJ_TM == 0
    bf16_rows = 2 * SUBLANES
    for steps in (batch * seq // ROPE_TM, batch * DIFF_HEADS * (seq // ATT_T)):
        assert D_MODEL % (steps * bf16_rows) == 0 and D_MIX % (steps * bf16_rows) == 0
    x2d = x.reshape(batch * seq, D_MODEL)
    tabs, w_in_b, w_out_b = _rope_tables(positions, w_in, w_out)
    ret_tabs = _retention_tables()
    for l in range(depth):
        ret_p, diff_p, lru_o = _inproj(
            x2d, pre_norm_g[l], w_in_b, tabs, lru_conv_w[l], lru_conv_b[l], lru_wa[l],
            lru_ba[l], lru_wx[l], lru_bx[l], lru_lambda[l], seq)
        ret_o = _retention(ret_p, ret_tabs, batch, seq)
        lam_vecs = jnp.stack([diff_lambda_q1[l], diff_lambda_k1[l],
                              diff_lambda_q2[l], diff_lambda_k2[l]])
        lam_init = 0.8 - 0.6 * math.exp(-0.3 * l)
        w_out_cur = w_out_b
        if l + 1 < depth:
            diff_o, w_in_b, w_out_b = _diff_attention(
                diff_p, lam_vecs, diff_subln_g[l], lam_init, batch, seq, (w_in, w_out, l + 1))
        else:
            diff_o, = _diff_attention(diff_p, lam_vecs, diff_subln_g[l], lam_init, batch, seq)
        x2d = _outproj(ret_o, diff_o, lru_o, w_out_cur, post_norm_g[l], x2d)
    return x2d.reshape(batch, seq, D_MODEL)
```

```python
import functools
import math

import jax
import jax.numpy as jnp
import numpy as np
from jax import lax
from jax.experimental import pallas as pl
from jax.experimental.pallas import tpu as pltpu

F32 = jnp.float32
BF16 = jnp.bfloat16

D_MODEL = 2048
HEAD_DIM = 128
RET_HEADS = 6
RET_W = RET_HEADS * HEAD_DIM
RET_CHUNK = 128
DIFF_HEADS = 4
DIFF_QK = HEAD_DIM // 2
DIFF_W = DIFF_HEADS * HEAD_DIM
LRU_W = 768
LRU_BLOCKS = 8
LRU_BW = LRU_W // LRU_BLOCKS
CONV_W = 4
LRU_C = 8.0
D_MIX = RET_W + DIFF_W + LRU_W
RET_COLS = 4 * RET_W
DIFF_COLS = 4 * DIFF_W
LRU_COLS = 2 * LRU_W
D_IN = RET_COLS + DIFF_COLS + LRU_COLS
ROPE_THETA = 10000.0
EPS = 1e-6
LOG2E = math.log2(math.e)

LANES = 128
SUBLANES = 8
VMEM_LIMIT = 52 * 1024 * 1024

ROPE_TM = 1024
INPROJ_TM = 256
INPROJ_TN = 512
RET_ROWS = 1024
ATT_T = 1024
ATT_TK = 2048
ATT_ROWS = 256
ATT_AHEAD = 8
LRU_GROUPS = 4
OUT_TM = 512
OUT_GROUPS = 2


def _silu(x):
    return x * jax.nn.sigmoid(x)


def _rope_table_kernel(pos_ref, c_ref, win_ref, wout_ref,
                       cos_r, sin_r, cos_d, sin_d, winb_ref, woutb_ref):
    winb_ref[...] = win_ref[...].astype(BF16)
    woutb_ref[...] = wout_ref[...].astype(BF16)
    pos = pos_ref[...].astype(F32)
    ang = pos * c_ref[0:1, :]
    c = jnp.cos(ang)
    s = jnp.sin(ang)
    lane = lax.broadcasted_iota(jnp.int32, ang.shape, 1)
    half = HEAD_DIM // 2
    q = DIFF_QK // 2
    low = lane < half
    cos_r[...] = jnp.where(low, c, pltpu.roll(c, half, 1))
    sin_r[...] = jnp.where(low, s, pltpu.roll(s, half, 1)) * c_ref[1:2, :]

    def spread(x):
        return jnp.where(lane < q, pltpu.roll(x, half, 1),
                         jnp.where(lane < 2 * q, pltpu.roll(x, half + q, 1),
                                   jnp.where(lane < 3 * q, x, pltpu.roll(x, q, 1))))

    cos_d[...] = spread(c)
    sin_d[...] = spread(s) * c_ref[2:3, :]


def _rope_tables(positions, w_in, w_out):
    t = positions.size
    steps = t // ROPE_TM
    half = HEAD_DIM // 2
    inv_r = 1.0 / (ROPE_THETA ** jnp.linspace(0.0, 1.0, half, dtype=F32))
    inv_d = 1.0 / (ROPE_THETA ** (jnp.arange(0, DIFF_QK, 2, dtype=F32) / DIFF_QK))
    one = jnp.ones((DIFF_QK // 2,), F32)
    consts = jnp.stack([
        jnp.concatenate([inv_r, inv_d, jnp.zeros((LANES - half - DIFF_QK // 2,), F32)]),
        jnp.concatenate([-jnp.ones((half,), F32), jnp.ones((half,), F32)]),
        jnp.concatenate([-one, one, -one, one]),
    ])
    consts = jnp.concatenate([consts, jnp.zeros((SUBLANES - 3, LANES), F32)])
    tab = jax.ShapeDtypeStruct((t, LANES), F32)
    row_spec = pl.BlockSpec((ROPE_TM, LANES), lambda i: (i, 0))
    w_specs_in, w_specs_out, w_shapes = [], [], []
    for w in (w_in, w_out):
        rows = w.shape[1] // steps
        w_specs_in.append(pl.BlockSpec((None, rows, w.shape[2]), lambda i: (0, i, 0)))
        w_specs_out.append(pl.BlockSpec((rows, w.shape[2]), lambda i: (i, 0)))
        w_shapes.append(jax.ShapeDtypeStruct(w.shape[1:], BF16))
    outs = pl.pallas_call(
        _rope_table_kernel,
        grid=(steps,),
        in_specs=[pl.BlockSpec((ROPE_TM, 1), lambda i: (i, 0)),
                  pl.BlockSpec((SUBLANES, LANES), lambda i: (0, 0))] + w_specs_in,
        out_specs=[row_spec] * 4 + w_specs_out,
        out_shape=[tab] * 4 + w_shapes,
        compiler_params=pltpu.CompilerParams(
            dimension_semantics=("arbitrary",), vmem_limit_bytes=VMEM_LIMIT),
        name="rope_tables",
    )(positions.reshape(t, 1), consts, w_in, w_out)
    return outs[:4], outs[4], outs[5]


def _rope_r(x, cos, sin):
    return x * cos + pltpu.roll(x, HEAD_DIM // 2, 1) * sin


def _rope_d(x, cos, sin, first_half):
    q = DIFF_QK // 2
    partner = jnp.where(first_half, pltpu.roll(x, LANES - q, 1), pltpu.roll(x, q, 1))
    return x * cos + partner * sin


def _inproj_kernel(x_ref, g_ref, w_ref, cr_ref, sr_ref, cd_ref, sd_ref,
                   cw_ref, cb_ref, gw_ref, gb_ref, lam_ref, never_ref,
                   ret_ref, diff_ref, lru_ref,
                   xn_scr, xs_scr, hc_scr, hs_scr, *, steps_per_seq):
    rows = x_ref.shape[0]
    pad = SUBLANES
    seq_start = pl.program_id(0) % steps_per_seq == 0

    @pl.when(seq_start)
    def _():
        xs_scr[0:pad, :] = jnp.zeros((pad, LRU_W), F32)
        hc_scr[...] = jnp.zeros_like(hc_scr)

    @pl.when(jnp.logical_not(seq_start))
    def _():
        xs_scr[0:pad, :] = xs_scr[rows:rows + pad, :]

    x = x_ref[...]
    ms = jnp.mean(x * x, axis=-1, keepdims=True)
    xn_scr[...] = ((x * lax.rsqrt(ms + EPS)) * g_ref[...]).astype(BF16)
    lane = lax.broadcasted_iota(jnp.int32, (rows, LANES), 1)
    first_half = (lane % DIFF_QK) < (DIFF_QK // 2)

    def proj(c0, width):
        return jnp.dot(xn_scr[...], w_ref[:, c0:c0 + width], preferred_element_type=F32)

    def emit(blk):
        c0 = blk * INPROJ_TN
        acc = proj(c0, INPROJ_TN)
        for s in range(INPROJ_TN // LANES):
            c = c0 + s * LANES
            y = acc[:, s * LANES:(s + 1) * LANES]
            if c < 2 * RET_W:
                y = _rope_r(y, cr_ref[...], sr_ref[...])
                if c >= RET_W:
                    y = y * (HEAD_DIM ** -0.5)
            elif RET_COLS <= c < RET_COLS + 2 * DIFF_W:
                y = _rope_d(y, cd_ref[...], sd_ref[...], first_half)
                if c < RET_COLS + DIFF_W:
                    y = y * (DIFF_QK ** -0.5 * LOG2E)
            y = y.astype(BF16)
            if c < RET_COLS:
                ret_ref[:, c:c + LANES] = y
            else:
                diff_ref[:, c - RET_COLS:c - RET_COLS + LANES] = y
        return acc[rows - SUBLANES:rows, INPROJ_TN - LANES:INPROJ_TN]

    lru0 = RET_COLS + DIFF_COLS
    xs_scr[pad:pad + rows, :] = proj(lru0, LRU_W)
    lg = proj(lru0 + LRU_W, LRU_W)
    xc = cb_ref[...] + xs_scr[pad - 3:pad - 3 + rows, :] * cw_ref[0:1, :]
    for j in range(1, CONV_W):
        xc = xc + xs_scr[pad - 3 + j:pad - 3 + j + rows, :] * cw_ref[j:j + 1, :]

    emit(0)
    emit(1)

    gates = jnp.dot(xc.astype(BF16), gw_ref[...], preferred_element_type=F32) + gb_ref[...]
    lam = lam_ref[...]
    log_sig = jnp.minimum(lam, 0.0) - jnp.log1p(jnp.exp(-jnp.abs(lam)))
    row = lax.broadcasted_iota(jnp.int32, (SUBLANES, LRU_W), 0)

    def lru_rows(rs, carry, after):
        never = never_ref[0] != 0
        g = gates[rs, :]
        g = jnp.where(never, jnp.tile(after, (g.shape[0] // SUBLANES, g.shape[1] // LANES)), g)
        r = jax.nn.sigmoid(g[:, :LRU_W])
        i = jax.nn.sigmoid(g[:, LRU_W:])
        log_a = (LRU_C * r) * log_sig
        a = jnp.exp(log_a)
        mult = jnp.sqrt(-jnp.tanh(log_a) * (a * a + 1.0))
        b = mult * (i * xc[rs, :])
        for t in range((rs.stop - rs.start) // SUBLANES):
            av = a[t * SUBLANES:(t + 1) * SUBLANES, :]
            bv = b[t * SUBLANES:(t + 1) * SUBLANES, :]
            for d in (1, 2, 4):
                a_prev = jnp.where(row >= d, pltpu.roll(av, d, 0), 1.0)
                b_prev = jnp.where(row >= d, pltpu.roll(bv, d, 0), 0.0)
                bv = av * b_prev + bv
                av = av * a_prev
            hv = av * carry + bv
            r0 = rs.start + t * SUBLANES
            hs_scr[r0:r0 + SUBLANES, :] = hv
            carry = jnp.broadcast_to(hv[SUBLANES - 1:SUBLANES, :], (SUBLANES, LRU_W))
        lru_ref[rs, :] = (hs_scr[rs, :] * _silu(lg[rs, :])).astype(BF16)
        return carry

    carry = hc_scr[...]
    blk = 2
    for gi in range(LRU_GROUPS):
        after = emit(blk)
        blk += 1
        carry = lru_rows(slice(gi * rows // LRU_GROUPS, (gi + 1) * rows // LRU_GROUPS),
                         carry, after)
    hc_scr[...] = carry
    while blk < lru0 // INPROJ_TN:
        emit(blk)
        blk += 1


def _block_diag(w):
    n, k, _ = w.shape
    eye = jnp.eye(n, dtype=w.dtype)
    return (w[:, :, None, :] * eye[:, None, :, None]).reshape(n * k, n * k)


def _inproj(x2d, g, w_bf16, tabs, conv_w, conv_b, wa, ba, wx, bx, lam, seq):
    t = x2d.shape[0]
    tm = INPROJ_TM
    gate_w = jnp.concatenate([_block_diag(wa), _block_diag(wx)], axis=1).astype(BF16)
    gate_b = jnp.concatenate([ba, bx]).reshape(1, 2 * LRU_W)
    tab_spec = pl.BlockSpec((tm, LANES), lambda i: (i, 0))

    def const(shape):
        return pl.BlockSpec(shape, lambda i: (0, 0))

    return pl.pallas_call(
        functools.partial(_inproj_kernel, steps_per_seq=seq // tm),
        grid=(t // tm,),
        in_specs=[pl.BlockSpec((tm, D_MODEL), lambda i: (i, 0)),
                  const((1, D_MODEL)),
                  pl.BlockSpec((D_MODEL, D_IN), lambda i: (0, 0),
                               pipeline_mode=pl.Buffered(1)),
                  tab_spec, tab_spec, tab_spec, tab_spec,
                  const((CONV_W, LRU_W)), const((1, LRU_W)),
                  pl.BlockSpec((LRU_W, 2 * LRU_W), lambda i: (0, 0),
                               pipeline_mode=pl.Buffered(1)),
                  const((1, 2 * LRU_W)), const((1, LRU_W)),
                  pl.BlockSpec(memory_space=pltpu.SMEM)],
        out_specs=[pl.BlockSpec((tm, RET_COLS), lambda i: (i, 0)),
                   pl.BlockSpec((tm, DIFF_COLS), lambda i: (i, 0)),
                   pl.BlockSpec((tm, LRU_W), lambda i: (i, 0))],
        out_shape=[jax.ShapeDtypeStruct((t, RET_COLS), BF16),
                   jax.ShapeDtypeStruct((t, DIFF_COLS), BF16),
                   jax.ShapeDtypeStruct((t, LRU_W), BF16)],
        scratch_shapes=[pltpu.VMEM((tm, D_MODEL), BF16),
                        pltpu.VMEM((tm + SUBLANES, LRU_W), F32),
                        pltpu.VMEM((SUBLANES, LRU_W), F32),
                        pltpu.VMEM((tm, LRU_W), F32)],
        compiler_params=pltpu.CompilerParams(
            dimension_semantics=("arbitrary",), vmem_limit_bytes=VMEM_LIMIT),
        name="inproj_lru",
    )(x2d, g.reshape(1, D_MODEL), w_bf16, *tabs, conv_w, conv_b.reshape(1, LRU_W),
      gate_w, gate_b, lam.reshape(1, LRU_W), jnp.zeros((1,), jnp.int32))


def _retention_tables():
    c = RET_CHUNK
    f32 = np.float32
    log_g = np.log1p(-np.power(f32(2.0), f32(-5.0) - np.arange(RET_HEADS, dtype=f32)))
    idx = np.arange(c, dtype=f32)
    rel = idx[:, None] - idx[None, :]
    intra = np.where(rel[None] >= 0,
                     np.exp(np.maximum(rel, f32(0.0))[None] * log_g[:, None, None]), f32(0.0))
    k_w = np.exp((f32(c - 1.0) - idx)[None, :] * log_g[:, None])
    q_w = np.exp((idx + f32(1.0))[None, :] * log_g[:, None])
    cdec = np.exp(f32(c) * log_g)
    full = (RET_HEADS, c, HEAD_DIM)
    return tuple(jnp.asarray(np.ascontiguousarray(t), dtype=F32) for t in (
        intra,
        np.broadcast_to(q_w[:, :, None], full),
        np.broadcast_to(k_w[:, :, None], full),
        np.broadcast_to(cdec[:, None, None], full)))


def _retention_kernel(q_ref, k_ref, v_ref, g_ref, dec_ref, qw_ref, kw_ref, cd_ref,
                      o_ref, state_ref):
    @pl.when(pl.program_id(1) == 0)
    def _():
        state_ref[...] = jnp.zeros_like(state_ref)

    nt = (((1,), (1,)), ((), ()))
    tn = (((0,), (0,)), ((), ()))
    for c in range(RET_ROWS // RET_CHUNK):
        rows = slice(c * RET_CHUNK, (c + 1) * RET_CHUNK)
        for h in range(RET_HEADS):
            cols = slice(h * HEAD_DIM, (h + 1) * HEAD_DIM)
            q = q_ref[rows, cols]
            k = k_ref[rows, cols]
            v = v_ref[rows, cols]
            st = state_ref[h]
            rhs = jnp.concatenate([k, st.astype(BF16)], axis=0)
            r = lax.dot_general(q, rhs, nt, preferred_element_type=F32)
            s = r[:, :RET_CHUNK] * dec_ref[h]
            o = (jnp.dot(s.astype(BF16), v, preferred_element_type=F32)
                 + r[:, RET_CHUNK:] * qw_ref[h])
            ks = (k.astype(F32) * kw_ref[h]).astype(BF16)
            kv_t = lax.dot_general(v, ks, tn, preferred_element_type=F32)
            state_ref[h] = st * cd_ref[h] + kv_t
            ms = jnp.mean(o * o, axis=-1, keepdims=True)
            y = o * lax.rsqrt(ms + EPS)
            o_ref[rows, cols] = (y * _silu(g_ref[rows, cols].astype(F32))).astype(BF16)


def _retention(ret_p, tables, batch, seq):
    t = batch * seq
    steps = seq // RET_ROWS

    def col(j):
        return pl.BlockSpec((RET_ROWS, RET_W), lambda b, i, j=j: (b * steps + i, j))

    tab_spec = pl.BlockSpec((RET_HEADS, RET_CHUNK, HEAD_DIM), lambda b, i: (0, 0, 0))
    return pl.pallas_call(
        _retention_kernel,
        grid=(batch, steps),
        in_specs=[col(0), col(1), col(2), col(3), tab_spec, tab_spec, tab_spec, tab_spec],
        out_specs=pl.BlockSpec((RET_ROWS, RET_W), lambda b, i: (b * steps + i, 0)),
        out_shape=jax.ShapeDtypeStruct((t, RET_W), BF16),
        scratch_shapes=[pltpu.VMEM((RET_HEADS, HEAD_DIM, HEAD_DIM), F32)],
        compiler_params=pltpu.CompilerParams(
            dimension_semantics=("arbitrary", "arbitrary"), vmem_limit_bytes=VMEM_LIMIT),
        name="retention",
    )(ret_p, ret_p, ret_p, ret_p, *tables)


def _diff_kernel(lam_ref, q_ref, k_ref, v_ref, g_ref, subg_ref, *refs, lam_init, cast_next):
    if cast_next:
        win_ref, wout_ref, o_ref, winb_ref, woutb_ref, qm_scr, m_scr, acc_scr, v1_scr = refs
        winb_ref[...] = win_ref[...].astype(BF16)
        woutb_ref[...] = wout_ref[...].astype(BF16)
    else:
        o_ref, qm_scr, m_scr, acc_scr, v1_scr = refs
    qi = pl.program_id(2)

    @pl.when(qi == 0)
    def _():
        v1_scr[:, 0:HEAD_DIM] = v_ref[...]
        v1_scr[:, HEAD_DIM:2 * HEAD_DIM] = jnp.ones(v_ref.shape, BF16)

    q = q_ref[...]
    lane = lax.broadcasted_iota(jnp.int32, q.shape, 1)
    qm_scr[0:ATT_T, :] = jnp.where(lane < DIFF_QK, q, jnp.zeros_like(q))
    qm_scr[ATT_T:2 * ATT_T, :] = jnp.where(lane >= DIFF_QK, q, jnp.zeros_like(q))
    nt = (((1,), (1,)), ((), ()))

    def lane_fold(x, op):
        parts = [x[:, i * LANES:(i + 1) * LANES] for i in range(x.shape[1] // LANES)]
        while len(parts) > 1:
            nxt = [op(parts[i], parts[i + 1]) for i in range(0, len(parts) - 1, 2)]
            parts = nxt + parts[len(parts) - len(parts) % 2:]
        return parts[0]

    def block(r0, size, diagonal):
        k = k_ref[pl.ds(r0, size), :]
        v = v1_scr[pl.ds(r0, size), :]
        n_chunks = 2 * ATT_T // ATT_ROWS

        def scores(i):
            rows = slice(i * ATT_ROWS, (i + 1) * ATT_ROWS)
            keys = (i * ATT_ROWS) % ATT_T + ATT_ROWS if diagonal else size
            s = lax.dot_general(qm_scr[rows, :], k[:keys, :], nt, preferred_element_type=F32)
            return rows, keys, s

        queued = [scores(i) for i in range(min(ATT_AHEAD, n_chunks))]
        for i in range(n_chunks):
            rows, keys, s = queued.pop(0)
            if i + ATT_AHEAD < n_chunks:
                queued.append(scores(i + ATT_AHEAD))
            if diagonal:
                tail = s[:, keys - ATT_ROWS:]
                row = lax.broadcasted_iota(jnp.int32, tail.shape, 0)
                col = lax.broadcasted_iota(jnp.int32, tail.shape, 1)
                tail = jnp.where(col <= row, tail, -jnp.inf)
                s = tail if keys == ATT_ROWS else jnp.concatenate(
                    [s[:, :keys - ATT_ROWS], tail], axis=1)
            m_cur = jnp.max(lane_fold(s, jnp.maximum), axis=-1, keepdims=True)
            if diagonal:
                m_new = jnp.broadcast_to(m_cur, (ATT_ROWS, LANES))
            else:
                m_prev = m_scr[rows, :]
                m_new = jnp.maximum(m_prev, m_cur)
                alpha = jnp.exp2(m_prev - m_new)
            p = jnp.exp2(s - jnp.tile(m_new, (1, keys // LANES)))
            acc_new = jnp.dot(p.astype(BF16), v[:keys, :], preferred_element_type=F32)
            if not diagonal:
                acc_new = jnp.tile(alpha, (1, 2)) * acc_scr[rows, :] + acc_new
            acc_scr[rows, :] = acc_new
            m_scr[rows, :] = m_new

    def body(kk, carry):
        block(pl.multiple_of(kk * ATT_TK, ATT_TK), ATT_TK, False)
        return carry

    below = qi * ATT_T
    block(pl.multiple_of(below, ATT_T), ATT_T, True)
    n_big = below // ATT_TK
    lax.fori_loop(0, n_big, body, 0)
    done = n_big * ATT_TK
    size = ATT_TK // 2
    while size >= ATT_T:
        take = below - done >= size

        @pl.when(take)
        def _(done=done, size=size):
            block(pl.multiple_of(done, ATT_T), size, False)

        done = done + jnp.where(take, size, 0)
        size //= 2

    lv = lam_ref[...]
    lam = (jnp.exp(jnp.sum(lv[0:1] * lv[1:2], axis=-1, keepdims=True))
           - jnp.exp(jnp.sum(lv[2:3] * lv[3:4], axis=-1, keepdims=True)) + lam_init)
    o = (acc_scr[0:ATT_T, :HEAD_DIM] / acc_scr[0:ATT_T, HEAD_DIM:]
         - lam * (acc_scr[ATT_T:2 * ATT_T, :HEAD_DIM] / acc_scr[ATT_T:2 * ATT_T, HEAD_DIM:]))
    ms = jnp.mean(o * o, axis=-1, keepdims=True)
    y = ((o * lax.rsqrt(ms + EPS)) * subg_ref[...]) * (1.0 - lam_init)
    o_ref[...] = (y * _silu(g_ref[...].astype(F32))).astype(BF16)


def _diff_attention(diff_p, lam_vecs, subg, lam_init, batch, seq, next_weights=None):
    t = batch * seq
    nq = seq // ATT_T
    steps = batch * DIFF_HEADS * nq
    in_specs = [
        pl.BlockSpec((4, DIFF_QK), lambda b, h, i: (0, 0)),
        pl.BlockSpec((ATT_T, HEAD_DIM), lambda b, h, i: (b * nq + i, h)),
        pl.BlockSpec((seq, HEAD_DIM), lambda b, h, i: (b, DIFF_HEADS + h)),
        pl.BlockSpec((seq, HEAD_DIM), lambda b, h, i: (b, 2 * DIFF_HEADS + h)),
        pl.BlockSpec((ATT_T, HEAD_DIM), lambda b, h, i: (b * nq + i, 3 * DIFF_HEADS + h)),
        pl.BlockSpec((1, HEAD_DIM), lambda b, h, i: (0, 0)),
    ]
    out_specs = [pl.BlockSpec((ATT_T, HEAD_DIM), lambda b, h, i: (b * nq + i, h))]
    out_shape = [jax.ShapeDtypeStruct((t, DIFF_W), BF16)]
    args = [lam_vecs, diff_p, diff_p, diff_p, diff_p, subg.reshape(1, HEAD_DIM)]
    if next_weights is not None:
        w_in, w_out, layer = next_weights

        def slab(b, h, i):
            return (b * DIFF_HEADS + h) * nq + i

        for w in (w_in, w_out):
            rows = w.shape[1] // steps
            in_specs.append(pl.BlockSpec((None, rows, w.shape[2]),
                                         lambda b, h, i: (layer, slab(b, h, i), 0)))
            out_specs.append(pl.BlockSpec((rows, w.shape[2]),
                                          lambda b, h, i: (slab(b, h, i), 0)))
            out_shape.append(jax.ShapeDtypeStruct(w.shape[1:], BF16))
            args.append(w)
    return pl.pallas_call(
        functools.partial(_diff_kernel, lam_init=lam_init,
                          cast_next=next_weights is not None),
        grid=(batch, DIFF_HEADS, nq),
        in_specs=in_specs,
        out_specs=out_specs,
        out_shape=out_shape,
        scratch_shapes=[pltpu.VMEM((2 * ATT_T, HEAD_DIM), BF16),
                        pltpu.VMEM((2 * ATT_T, LANES), F32),
                        pltpu.VMEM((2 * ATT_T, 2 * HEAD_DIM), F32),
                        pltpu.VMEM((seq, 2 * HEAD_DIM), BF16)],
        compiler_params=pltpu.CompilerParams(
            dimension_semantics=("arbitrary", "arbitrary", "arbitrary"),
            vmem_limit_bytes=VMEM_LIMIT),
        name="diff_attention",
    )(*args)


def _outproj_kernel(ret_ref, diff_ref, lru_ref, w_ref, g_ref, x_ref, o_ref):
    rows = x_ref.shape[0] // OUT_GROUPS
    for i in range(OUT_GROUPS):
        rs = slice(i * rows, (i + 1) * rows)
        y = jnp.dot(ret_ref[rs, :], w_ref[0:RET_W, :], preferred_element_type=F32)
        y = y + jnp.dot(diff_ref[rs, :], w_ref[RET_W:RET_W + DIFF_W, :],
                        preferred_element_type=F32)
        y = y + jnp.dot(lru_ref[rs, :], w_ref[RET_W + DIFF_W:D_MIX, :],
                        preferred_element_type=F32)
        ms = jnp.mean(y * y, axis=-1, keepdims=True)
        o_ref[rs, :] = x_ref[rs, :] + (y * lax.rsqrt(ms + EPS)) * g_ref[...]


def _outproj(ret_o, diff_o, lru_o, w_bf16, g, x2d):
    t = x2d.shape[0]
    tm = OUT_TM
    return pl.pallas_call(
        _outproj_kernel,
        grid=(t // tm,),
        in_specs=[pl.BlockSpec((tm, RET_W), lambda i: (i, 0)),
                  pl.BlockSpec((tm, DIFF_W), lambda i: (i, 0)),
                  pl.BlockSpec((tm, LRU_W), lambda i: (i, 0)),
                  pl.BlockSpec((D_MIX, D_MODEL), lambda i: (0, 0),
                               pipeline_mode=pl.Buffered(1)),
                  pl.BlockSpec((1, D_MODEL), lambda i: (0, 0)),
                  pl.BlockSpec((tm, D_MODEL), lambda i: (i, 0))],
        out_specs=pl.BlockSpec((tm, D_MODEL), lambda i: (i, 0)),
        out_shape=jax.ShapeDtypeStruct((t, D_MODEL), F32),
        compiler_params=pltpu.CompilerParams(
            dimension_semantics=("arbitrary",), vmem_limit_bytes=VMEM_LIMIT),
        name="outproj",
    )(ret_o, diff_o, lru_o, w_bf16, g.reshape(1, D_MODEL), x2d)


def kernel(x, positions, pre_norm_g, w_in, diff_lambda_q1, diff_lambda_k1, diff_lambda_q2,
           diff_lambda_k2, diff_subln_g, lru_conv_w, lru_conv_b, lru_wa, lru_ba, lru_wx,
           lru_bx, lru_lambda, w_out, post_norm_g):
    batch, seq, _ = x.shape
    depth = w_in.shape[0]
    assert seq % ATT_T == 0 and seq % RET_ROWS == 0 and seq % INPROJ_TM == 0
    bf16_rows = 2 * SUBLANES
    for steps in (batch * seq // ROPE_TM, batch * DIFF_HEADS * (seq // ATT_T)):
        assert D_MODEL % (steps * bf16_rows) == 0 and D_MIX % (steps * bf16_rows) == 0
    x2d = x.reshape(batch * seq, D_MODEL)
    tabs, w_in_b, w_out_b = _rope_tables(positions, w_in, w_out)
    ret_tabs = _retention_tables()
    for l in range(depth):
        ret_p, diff_p, lru_o = _inproj(
            x2d, pre_norm_g[l], w_in_b, tabs, lru_conv_w[l], lru_conv_b[l], lru_wa[l],
            lru_ba[l], lru_wx[l], lru_bx[l], lru_lambda[l], seq)
        ret_o = _retention(ret_p, ret_tabs, batch, seq)
        lam_vecs = jnp.stack([diff_lambda_q1[l], diff_lambda_k1[l],
                              diff_lambda_q2[l], diff_lambda_k2[l]])
        lam_init = 0.8 - 0.6 * math.exp(-0.3 * l)
        w_out_cur = w_out_b
        if l + 1 < depth:
            diff_o, w_in_b, w_out_b = _diff_attention(
                diff_p, lam_vecs, diff_subln_g[l], lam_init, batch, seq, (w_in, w_out, l + 1))
        else:
            diff_o, = _diff_attention(diff_p, lam_vecs, diff_subln_g[l], lam_init, batch, seq)
        x2d = _outproj(ret_o, diff_o, lru_o, w_out_cur, post_norm_g[l], x2d)
    return x2d.reshape(batch, seq, D_MODEL)
```

```python
import functools
import math

import jax
import jax.numpy as jnp
import numpy as np
from jax import lax
from jax.experimental import pallas as pl
from jax.experimental.pallas import tpu as pltpu

F32 = jnp.float32
BF16 = jnp.bfloat16

D_MODEL = 2048
HEAD_DIM = 128
RET_HEADS = 6
RET_W = RET_HEADS * HEAD_DIM
RET_CHUNK = 128
DIFF_HEADS = 4
DIFF_QK = HEAD_DIM // 2
DIFF_W = DIFF_HEADS * HEAD_DIM
LRU_W = 768
LRU_BLOCKS = 8
LRU_BW = LRU_W // LRU_BLOCKS
CONV_W = 4
LRU_C = 8.0
D_MIX = RET_W + DIFF_W + LRU_W
RET_COLS = 4 * RET_W
DIFF_COLS = 4 * DIFF_W
LRU_COLS = 2 * LRU_W
D_IN = RET_COLS + DIFF_COLS + LRU_COLS
ROPE_THETA = 10000.0
EPS = 1e-6
LOG2E = math.log2(math.e)

LANES = 128
SUBLANES = 8
VMEM_LIMIT = 52 * 1024 * 1024

ROPE_TM = 1024
INPROJ_TM = 256
INPROJ_TN = 512
RET_ROWS = 1024
ATT_T = 1024
ATT_TK = 2048
ATT_ROWS = 256
ATT_AHEAD = 8
LRU_GROUPS = 4
OUT_TM = 512
OUT_GROUPS = 2


def _silu(x):
    return x * jax.nn.sigmoid(x)


def _rope_table_kernel(pos_ref, c_ref, win_ref, wout_ref,
                       cos_r, sin_r, cos_d, sin_d, winb_ref, woutb_ref):
    winb_ref[...] = win_ref[...].astype(BF16)
    woutb_ref[...] = wout_ref[...].astype(BF16)
    pos = pos_ref[...].astype(F32)
    ang = pos * c_ref[0:1, :]
    c = jnp.cos(ang)
    s = jnp.sin(ang)
    lane = lax.broadcasted_iota(jnp.int32, ang.shape, 1)
    half = HEAD_DIM // 2
    q = DIFF_QK // 2
    low = lane < half
    cos_r[...] = jnp.where(low, c, pltpu.roll(c, half, 1))
    sin_r[...] = jnp.where(low, s, pltpu.roll(s, half, 1)) * c_ref[1:2, :]

    def spread(x):
        return jnp.where(lane < q, pltpu.roll(x, half, 1),
                         jnp.where(lane < 2 * q, pltpu.roll(x, half + q, 1),
                                   jnp.where(lane < 3 * q, x, pltpu.roll(x, q, 1))))

    cos_d[...] = spread(c)
    sin_d[...] = spread(s) * c_ref[2:3, :]


def _rope_tables(positions, w_in, w_out):
    t = positions.size
    steps = t // ROPE_TM
    half = HEAD_DIM // 2
    inv_r = 1.0 / (ROPE_THETA ** jnp.linspace(0.0, 1.0, half, dtype=F32))
    inv_d = 1.0 / (ROPE_THETA ** (jnp.arange(0, DIFF_QK, 2, dtype=F32) / DIFF_QK))
    one = jnp.ones((DIFF_QK // 2,), F32)
    consts = jnp.stack([
        jnp.concatenate([inv_r, inv_d, jnp.zeros((LANES - half - DIFF_QK // 2,), F32)]),
        jnp.concatenate([-jnp.ones((half,), F32), jnp.ones((half,), F32)]),
        jnp.concatenate([-one, one, -one, one]),
    ])
    consts = jnp.concatenate([consts, jnp.zeros((SUBLANES - 3, LANES), F32)])
    tab = jax.ShapeDtypeStruct((t, LANES), F32)
    row_spec = pl.BlockSpec((ROPE_TM, LANES), lambda i: (i, 0))
    w_specs_in, w_specs_out, w_shapes = [], [], []
    for w in (w_in, w_out):
        rows = w.shape[1] // steps
        w_specs_in.append(pl.BlockSpec((None, rows, w.shape[2]), lambda i: (0, i, 0)))
        w_specs_out.append(pl.BlockSpec((rows, w.shape[2]), lambda i: (i, 0)))
        w_shapes.append(jax.ShapeDtypeStruct(w.shape[1:], BF16))
    outs = pl.pallas_call(
        _rope_table_kernel,
        grid=(steps,),
        in_specs=[pl.BlockSpec((ROPE_TM, 1), lambda i: (i, 0)),
                  pl.BlockSpec((SUBLANES, LANES), lambda i: (0, 0))] + w_specs_in,
        out_specs=[row_spec] * 4 + w_specs_out,
        out_shape=[tab] * 4 + w_shapes,
        compiler_params=pltpu.CompilerParams(
            dimension_semantics=("arbitrary",), vmem_limit_bytes=VMEM_LIMIT),
        name="rope_tables",
    )(positions.reshape(t, 1), consts, w_in, w_out)
    return outs[:4], outs[4], outs[5]


def _rope_r(x, cos, sin):
    return x * cos + pltpu.roll(x, HEAD_DIM // 2, 1) * sin


def _rope_d(x, cos, sin, first_half):
    q = DIFF_QK // 2
    partner = jnp.where(first_half, pltpu.roll(x, LANES - q, 1), pltpu.roll(x, q, 1))
    return x * cos + partner * sin


def _inproj_kernel(x_ref, g_ref, w_ref, cr_ref, sr_ref, cd_ref, sd_ref,
                   cw_ref, cb_ref, gw_ref, gb_ref, lam_ref, never_ref,
                   ret_ref, diff_ref, lru_ref,
                   xn_scr, xs_scr, hc_scr, hs_scr, *, steps_per_seq):
    rows = x_ref.shape[0]
    pad = SUBLANES
    seq_start = pl.program_id(0) % steps_per_seq == 0

    @pl.when(seq_start)
    def _():
        xs_scr[0:pad, :] = jnp.zeros((pad, LRU_W), F32)
        hc_scr[...] = jnp.zeros_like(hc_scr)

    @pl.when(jnp.logical_not(seq_start))
    def _():
        xs_scr[0:pad, :] = xs_scr[rows:rows + pad, :]

    x = x_ref[...]
    ms = jnp.mean(x * x, axis=-1, keepdims=True)
    xn_scr[...] = ((x * lax.rsqrt(ms + EPS)) * g_ref[...]).astype(BF16)
    lane = lax.broadcasted_iota(jnp.int32, (rows, LANES), 1)
    first_half = (lane % DIFF_QK) < (DIFF_QK // 2)

    def proj(c0, width):
        return jnp.dot(xn_scr[...], w_ref[:, c0:c0 + width], preferred_element_type=F32)

    def emit(blk):
        c0 = blk * INPROJ_TN
        acc = proj(c0, INPROJ_TN)
        for s in range(INPROJ_TN // LANES):
            c = c0 + s * LANES
            y = acc[:, s * LANES:(s + 1) * LANES]
            if c < 2 * RET_W:
                y = _rope_r(y, cr_ref[...], sr_ref[...])
                if c >= RET_W:
                    y = y * (HEAD_DIM ** -0.5)
            elif RET_COLS <= c < RET_COLS + 2 * DIFF_W:
                y = _rope_d(y, cd_ref[...], sd_ref[...], first_half)
                if c < RET_COLS + DIFF_W:
                    y = y * (DIFF_QK ** -0.5 * LOG2E)
            y = y.astype(BF16)
            if c < RET_COLS:
                ret_ref[:, c:c + LANES] = y
            else:
                diff_ref[:, c - RET_COLS:c - RET_COLS + LANES] = y
        return acc[rows - SUBLANES:rows, INPROJ_TN - LANES:INPROJ_TN]

    lru0 = RET_COLS + DIFF_COLS
    xs_scr[pad:pad + rows, :] = proj(lru0, LRU_W)
    lg = proj(lru0 + LRU_W, LRU_W)
    xc = cb_ref[...] + xs_scr[pad - 3:pad - 3 + rows, :] * cw_ref[0:1, :]
    for j in range(1, CONV_W):
        xc = xc + xs_scr[pad - 3 + j:pad - 3 + j + rows, :] * cw_ref[j:j + 1, :]

    emit(0)
    emit(1)

    xcb = xc.astype(BF16)
    tile_w = 2 * LANES
    gate_cols = []
    for half in range(2):
        for j in range(LRU_W // tile_w):
            lo = (j * tile_w // LRU_BW) * LRU_BW
            hi = min(LRU_W, -(-((j + 1) * tile_w) // LRU_BW) * LRU_BW)
            k0 = (lo // tile_w) * tile_w
            k1 = min(LRU_W, -(-hi // tile_w) * tile_w)
            c0 = half * LRU_W + j * tile_w
            gate_cols.append(jnp.dot(xcb[:, k0:k1], gw_ref[k0:k1, c0:c0 + tile_w],
                                     preferred_element_type=F32))
    gates = jnp.concatenate(gate_cols, axis=1) + gb_ref[...]
    lam = lam_ref[...]
    log_sig = jnp.minimum(lam, 0.0) - jnp.log1p(jnp.exp(-jnp.abs(lam)))
    row = lax.broadcasted_iota(jnp.int32, (SUBLANES, LRU_W), 0)

    def lru_rows(rs, carry, after):
        never = never_ref[0] != 0
        g = gates[rs, :]
        g = jnp.where(never, jnp.tile(after, (g.shape[0] // SUBLANES, g.shape[1] // LANES)), g)
        r = jax.nn.sigmoid(g[:, :LRU_W])
        i = jax.nn.sigmoid(g[:, LRU_W:])
        log_a = (LRU_C * r) * log_sig
        a = jnp.exp(log_a)
        mult = jnp.sqrt(-jnp.tanh(log_a) * (a * a + 1.0))
        b = mult * (i * xc[rs, :])
        for t in range((rs.stop - rs.start) // SUBLANES):
            av = a[t * SUBLANES:(t + 1) * SUBLANES, :]
            bv = b[t * SUBLANES:(t + 1) * SUBLANES, :]
            for d in (1, 2, 4):
                a_prev = jnp.where(row >= d, pltpu.roll(av, d, 0), 1.0)
                b_prev = jnp.where(row >= d, pltpu.roll(bv, d, 0), 0.0)
                bv = av * b_prev + bv
                av = av * a_prev
            hv = av * carry + bv
            r0 = rs.start + t * SUBLANES
            hs_scr[r0:r0 + SUBLANES, :] = hv
            carry = jnp.broadcast_to(hv[SUBLANES - 1:SUBLANES, :], (SUBLANES, LRU_W))
        lru_ref[rs, :] = (hs_scr[rs, :] * _silu(lg[rs, :])).astype(BF16)
        return carry

    carry = hc_scr[...]
    blk = 2
    for gi in range(LRU_GROUPS):
        after = emit(blk)
        blk += 1
        carry = lru_rows(slice(gi * rows // LRU_GROUPS, (gi + 1) * rows // LRU_GROUPS),
                         carry, after)
    hc_scr[...] = carry
    while blk < lru0 // INPROJ_TN:
        emit(blk)
        blk += 1


def _block_diag(w):
    n, k, _ = w.shape
    eye = jnp.eye(n, dtype=w.dtype)
    return (w[:, :, None, :] * eye[:, None, :, None]).reshape(n * k, n * k)


def _inproj(x2d, g, w_bf16, tabs, conv_w, conv_b, wa, ba, wx, bx, lam, seq):
    t = x2d.shape[0]
    tm = INPROJ_TM
    gate_w = jnp.concatenate([_block_diag(wa), _block_diag(wx)], axis=1).astype(BF16)
    gate_b = jnp.concatenate([ba, bx]).reshape(1, 2 * LRU_W)
    tab_spec = pl.BlockSpec((tm, LANES), lambda i: (i, 0))

    def const(shape):
        return pl.BlockSpec(shape, lambda i: (0, 0))

    return pl.pallas_call(
        functools.partial(_inproj_kernel, steps_per_seq=seq // tm),
        grid=(t // tm,),
        in_specs=[pl.BlockSpec((tm, D_MODEL), lambda i: (i, 0)),
                  const((1, D_MODEL)),
                  pl.BlockSpec((D_MODEL, D_IN), lambda i: (0, 0),
                               pipeline_mode=pl.Buffered(1)),
                  tab_spec, tab_spec, tab_spec, tab_spec,
                  const((CONV_W, LRU_W)), const((1, LRU_W)),
                  pl.BlockSpec((LRU_W, 2 * LRU_W), lambda i: (0, 0),
                               pipeline_mode=pl.Buffered(1)),
                  const((1, 2 * LRU_W)), const((1, LRU_W)),
                  pl.BlockSpec(memory_space=pltpu.SMEM)],
        out_specs=[pl.BlockSpec((tm, RET_COLS), lambda i: (i, 0)),
                   pl.BlockSpec((tm, DIFF_COLS), lambda i: (i, 0)),
                   pl.BlockSpec((tm, LRU_W), lambda i: (i, 0))],
        out_shape=[jax.ShapeDtypeStruct((t, RET_COLS), BF16),
                   jax.ShapeDtypeStruct((t, DIFF_COLS), BF16),
                   jax.ShapeDtypeStruct((t, LRU_W), BF16)],
        scratch_shapes=[pltpu.VMEM((tm, D_MODEL), BF16),
                        pltpu.VMEM((tm + SUBLANES, LRU_W), F32),
                        pltpu.VMEM((SUBLANES, LRU_W), F32),
                        pltpu.VMEM((tm, LRU_W), F32)],
        compiler_params=pltpu.CompilerParams(
            dimension_semantics=("arbitrary",), vmem_limit_bytes=VMEM_LIMIT),
        name="inproj_lru",
    )(x2d, g.reshape(1, D_MODEL), w_bf16, *tabs, conv_w, conv_b.reshape(1, LRU_W),
      gate_w, gate_b, lam.reshape(1, LRU_W), jnp.zeros((1,), jnp.int32))


def _retention_tables():
    c = RET_CHUNK
    f32 = np.float32
    log_g = np.log1p(-np.power(f32(2.0), f32(-5.0) - np.arange(RET_HEADS, dtype=f32)))
    idx = np.arange(c, dtype=f32)
    rel = idx[:, None] - idx[None, :]
    intra = np.where(rel[None] >= 0,
                     np.exp(np.maximum(rel, f32(0.0))[None] * log_g[:, None, None]), f32(0.0))
    k_w = np.exp((f32(c - 1.0) - idx)[None, :] * log_g[:, None])
    q_w = np.exp((idx + f32(1.0))[None, :] * log_g[:, None])
    cdec = np.exp(f32(c) * log_g)
    full = (RET_HEADS, c, HEAD_DIM)
    return tuple(jnp.asarray(np.ascontiguousarray(t), dtype=F32) for t in (
        intra,
        np.broadcast_to(q_w[:, :, None], full),
        np.broadcast_to(k_w[:, :, None], full),
        np.broadcast_to(cdec[:, None, None], full)))


def _retention_kernel(q_ref, k_ref, v_ref, g_ref, dec_ref, qw_ref, kw_ref, cd_ref,
                      o_ref, state_ref):
    @pl.when(pl.program_id(1) == 0)
    def _():
        state_ref[...] = jnp.zeros_like(state_ref)

    nt = (((1,), (1,)), ((), ()))
    tn = (((0,), (0,)), ((), ()))
    for c in range(RET_ROWS // RET_CHUNK):
        rows = slice(c * RET_CHUNK, (c + 1) * RET_CHUNK)
        for h in range(RET_HEADS):
            cols = slice(h * HEAD_DIM, (h + 1) * HEAD_DIM)
            q = q_ref[rows, cols]
            k = k_ref[rows, cols]
            v = v_ref[rows, cols]
            st = state_ref[h]
            rhs = jnp.concatenate([k, st.astype(BF16)], axis=0)
            r = lax.dot_general(q, rhs, nt, preferred_element_type=F32)
            s = r[:, :RET_CHUNK] * dec_ref[h]
            o = (jnp.dot(s.astype(BF16), v, preferred_element_type=F32)
                 + r[:, RET_CHUNK:] * qw_ref[h])
            ks = (k.astype(F32) * kw_ref[h]).astype(BF16)
            kv_t = lax.dot_general(v, ks, tn, preferred_element_type=F32)
            state_ref[h] = st * cd_ref[h] + kv_t
            ms = jnp.mean(o * o, axis=-1, keepdims=True)
            y = o * lax.rsqrt(ms + EPS)
            o_ref[rows, cols] = (y * _silu(g_ref[rows, cols].astype(F32))).astype(BF16)


def _retention(ret_p, tables, batch, seq):
    t = batch * seq
    steps = seq // RET_ROWS

    def col(j):
        return pl.BlockSpec((RET_ROWS, RET_W), lambda b, i, j=j: (b * steps + i, j))

    tab_spec = pl.BlockSpec((RET_HEADS, RET_CHUNK, HEAD_DIM), lambda b, i: (0, 0, 0))
    return pl.pallas_call(
        _retention_kernel,
        grid=(batch, steps),
        in_specs=[col(0), col(1), col(2), col(3), tab_spec, tab_spec, tab_spec, tab_spec],
        out_specs=pl.BlockSpec((RET_ROWS, RET_W), lambda b, i: (b * steps + i, 0)),
        out_shape=jax.ShapeDtypeStruct((t, RET_W), BF16),
        scratch_shapes=[pltpu.VMEM((RET_HEADS, HEAD_DIM, HEAD_DIM), F32)],
        compiler_params=pltpu.CompilerParams(
            dimension_semantics=("arbitrary", "arbitrary"), vmem_limit_bytes=VMEM_LIMIT),
        name="retention",
    )(ret_p, ret_p, ret_p, ret_p, *tables)


def _diff_kernel(lam_ref, q_ref, k_ref, v_ref, g_ref, subg_ref, *refs, lam_init, cast_next):
    if cast_next:
        win_ref, wout_ref, o_ref, winb_ref, woutb_ref, qm_scr, m_scr, acc_scr, v1_scr = refs
        winb_ref[...] = win_ref[...].astype(BF16)
        woutb_ref[...] = wout_ref[...].astype(BF16)
    else:
        o_ref, qm_scr, m_scr, acc_scr, v1_scr = refs
    qi = pl.program_id(2)

    @pl.when(qi == 0)
    def _():
        v1_scr[:, 0:HEAD_DIM] = v_ref[...]
        v1_scr[:, HEAD_DIM:2 * HEAD_DIM] = jnp.ones(v_ref.shape, BF16)

    q = q_ref[...]
    lane = lax.broadcasted_iota(jnp.int32, q.shape, 1)
    qm_scr[0:ATT_T, :] = jnp.where(lane < DIFF_QK, q, jnp.zeros_like(q))
    qm_scr[ATT_T:2 * ATT_T, :] = jnp.where(lane >= DIFF_QK, q, jnp.zeros_like(q))
    nt = (((1,), (1,)), ((), ()))

    def lane_fold(x, op):
        parts = [x[:, i * LANES:(i + 1) * LANES] for i in range(x.shape[1] // LANES)]
        while len(parts) > 1:
            nxt = [op(parts[i], parts[i + 1]) for i in range(0, len(parts) - 1, 2)]
            parts = nxt + parts[len(parts) - len(parts) % 2:]
        return parts[0]

    def block(r0, size, diagonal):
        k = k_ref[pl.ds(r0, size), :]
        v = v1_scr[pl.ds(r0, size), :]
        n_chunks = 2 * ATT_T // ATT_ROWS

        def scores(i):
            rows = slice(i * ATT_ROWS, (i + 1) * ATT_ROWS)
            keys = (i * ATT_ROWS) % ATT_T + ATT_ROWS if diagonal else size
            s = lax.dot_general(qm_scr[rows, :], k[:keys, :], nt, preferred_element_type=F32)
            return rows, keys, s

        queued = [scores(i) for i in range(min(ATT_AHEAD, n_chunks))]
        for i in range(n_chunks):
            rows, keys, s = queued.pop(0)
            if i + ATT_AHEAD < n_chunks:
                queued.append(scores(i + ATT_AHEAD))
            if diagonal:
                tail = s[:, keys - ATT_ROWS:]
                row = lax.broadcasted_iota(jnp.int32, tail.shape, 0)
                col = lax.broadcasted_iota(jnp.int32, tail.shape, 1)
                tail = jnp.where(col <= row, tail, -jnp.inf)
                s = tail if keys == ATT_ROWS else jnp.concatenate(
                    [s[:, :keys - ATT_ROWS], tail], axis=1)
            m_cur = jnp.max(lane_fold(s, jnp.maximum), axis=-1, keepdims=True)
            if diagonal:
                m_new = jnp.broadcast_to(m_cur, (ATT_ROWS, LANES))
            else:
                m_prev = m_scr[rows, :]
                m_new = jnp.maximum(m_prev, m_cur)
                alpha = jnp.exp2(m_prev - m_new)
            p = jnp.exp2(s - jnp.tile(m_new, (1, keys // LANES)))
            acc_new = jnp.dot(p.astype(BF16), v[:keys, :], preferred_element_type=F32)
            if not diagonal:
                acc_new = jnp.tile(alpha, (1, 2)) * acc_scr[rows, :] + acc_new
            acc_scr[rows, :] = acc_new
            m_scr[rows, :] = m_new

    def body(kk, carry):
        block(pl.multiple_of(kk * ATT_TK, ATT_TK), ATT_TK, False)
        return carry

    below = qi * ATT_T
    block(pl.multiple_of(below, ATT_T), ATT_T, True)
    n_big = below // ATT_TK
    lax.fori_loop(0, n_big, body, 0)
    done = n_big * ATT_TK
    size = ATT_TK // 2
    while size >= ATT_T:
        take = below - done >= size

        @pl.when(take)
        def _(done=done, size=size):
            block(pl.multiple_of(done, ATT_T), size, False)

        done = done + jnp.where(take, size, 0)
        size //= 2

    lv = lam_ref[...]
    lam = (jnp.exp(jnp.sum(lv[0:1] * lv[1:2], axis=-1, keepdims=True))
           - jnp.exp(jnp.sum(lv[2:3] * lv[3:4], axis=-1, keepdims=True)) + lam_init)
    o = (acc_scr[0:ATT_T, :HEAD_DIM] / acc_scr[0:ATT_T, HEAD_DIM:]
         - lam * (acc_scr[ATT_T:2 * ATT_T, :HEAD_DIM] / acc_scr[ATT_T:2 * ATT_T, HEAD_DIM:]))
    ms = jnp.mean(o * o, axis=-1, keepdims=True)
    y = ((o * lax.rsqrt(ms + EPS)) * subg_ref[...]) * (1.0 - lam_init)
    o_ref[...] = (y * _silu(g_ref[...].astype(F32))).astype(BF16)


def _diff_attention(diff_p, lam_vecs, subg, lam_init, batch, seq, next_weights=None):
    t = batch * seq
    nq = seq // ATT_T
    steps = batch * DIFF_HEADS * nq
    in_specs = [
        pl.BlockSpec((4, DIFF_QK), lambda b, h, i: (0, 0)),
        pl.BlockSpec((ATT_T, HEAD_DIM), lambda b, h, i: (b * nq + i, h)),
        pl.BlockSpec((seq, HEAD_DIM), lambda b, h, i: (b, DIFF_HEADS + h)),
        pl.BlockSpec((seq, HEAD_DIM), lambda b, h, i: (b, 2 * DIFF_HEADS + h)),
        pl.BlockSpec((ATT_T, HEAD_DIM), lambda b, h, i: (b * nq + i, 3 * DIFF_HEADS + h)),
        pl.BlockSpec((1, HEAD_DIM), lambda b, h, i: (0, 0)),
    ]
    out_specs = [pl.BlockSpec((ATT_T, HEAD_DIM), lambda b, h, i: (b * nq + i, h))]
    out_shape = [jax.ShapeDtypeStruct((t, DIFF_W), BF16)]
    args = [lam_vecs, diff_p, diff_p, diff_p, diff_p, subg.reshape(1, HEAD_DIM)]
    if next_weights is not None:
        w_in, w_out, layer = next_weights

        def slab(b, h, i):
            return (b * DIFF_HEADS + h) * nq + i

        for w in (w_in, w_out):
            rows = w.shape[1] // steps
            in_specs.append(pl.BlockSpec((None, rows, w.shape[2]),
                                         lambda b, h, i: (layer, slab(b, h, i), 0)))
            out_specs.append(pl.BlockSpec((rows, w.shape[2]),
                                          lambda b, h, i: (slab(b, h, i), 0)))
            out_shape.append(jax.ShapeDtypeStruct(w.shape[1:], BF16))
            args.append(w)
    return pl.pallas_call(
        functools.partial(_diff_kernel, lam_init=lam_init,
                          cast_next=next_weights is not None),
        grid=(batch, DIFF_HEADS, nq),
        in_specs=in_specs,
        out_specs=out_specs,
        out_shape=out_shape,
        scratch_shapes=[pltpu.VMEM((2 * ATT_T, HEAD_DIM), BF16),
                        pltpu.VMEM((2 * ATT_T, LANES), F32),
                        pltpu.VMEM((2 * ATT_T, 2 * HEAD_DIM), F32),
                        pltpu.VMEM((seq, 2 * HEAD_DIM), BF16)],
        compiler_params=pltpu.CompilerParams(
            dimension_semantics=("arbitrary", "arbitrary", "arbitrary"),
            vmem_limit_bytes=VMEM_LIMIT),
        name="diff_attention",
    )(*args)


def _outproj_kernel(ret_ref, diff_ref, lru_ref, w_ref, g_ref, x_ref, o_ref):
    rows = x_ref.shape[0] // OUT_GROUPS
    for i in range(OUT_GROUPS):
        rs = slice(i * rows, (i + 1) * rows)
        y = jnp.dot(ret_ref[rs, :], w_ref[0:RET_W, :], preferred_element_type=F32)
        y = y + jnp.dot(diff_ref[rs, :], w_ref[RET_W:RET_W + DIFF_W, :],
                        preferred_element_type=F32)
        y = y + jnp.dot(lru_ref[rs, :], w_ref[RET_W + DIFF_W:D_MIX, :],
                        preferred_element_type=F32)
        ms = jnp.mean(y * y, axis=-1, keepdims=True)
        o_ref[rs, :] = x_ref[rs, :] + (y * lax.rsqrt(ms + EPS)) * g_ref[...]


def _outproj(ret_o, diff_o, lru_o, w_bf16, g, x2d):
    t = x2d.shape[0]
    tm = OUT_TM
    return pl.pallas_call(
        _outproj_kernel,
        grid=(t // tm,),
        in_specs=[pl.BlockSpec((tm, RET_W), lambda i: (i, 0)),
                  pl.BlockSpec((tm, DIFF_W), lambda i: (i, 0)),
                  pl.BlockSpec((tm, LRU_W), lambda i: (i, 0)),
                  pl.BlockSpec((D_MIX, D_MODEL), lambda i: (0, 0),
                               pipeline_mode=pl.Buffered(1)),
                  pl.BlockSpec((1, D_MODEL), lambda i: (0, 0)),
                  pl.BlockSpec((tm, D_MODEL), lambda i: (i, 0))],
        out_specs=pl.BlockSpec((tm, D_MODEL), lambda i: (i, 0)),
        out_shape=jax.ShapeDtypeStruct((t, D_MODEL), F32),
        compiler_params=pltpu.CompilerParams(
            dimension_semantics=("arbitrary",), vmem_limit_bytes=VMEM_LIMIT),
        name="outproj",
    )(ret_o, diff_o, lru_o, w_bf16, g.reshape(1, D_MODEL), x2d)


def kernel(x, positions, pre_norm_g, w_in, diff_lambda_q1, diff_lambda_k1, diff_lambda_q2,
           diff_lambda_k2, diff_subln_g, lru_conv_w, lru_conv_b, lru_wa, lru_ba, lru_wx,
           lru_bx, lru_lambda, w_out, post_norm_g):
    batch, seq, _ = x.shape
    depth = w_in.shape[0]
    assert seq % ATT_T == 0 and seq % RET_ROWS == 0 and seq % INPROJ_TM == 0
    bf16_rows = 2 * SUBLANES
    for steps in (batch * seq // ROPE_TM, batch * DIFF_HEADS * (seq // ATT_T)):
        assert D_MODEL % (steps * bf16_rows) == 0 and D_MIX % (steps * bf16_rows) == 0
    x2d = x.reshape(batch * seq, D_MODEL)
    tabs, w_in_b, w_out_b = _rope_tables(positions, w_in, w_out)
    ret_tabs = _retention_tables()
    for l in range(depth):
        ret_p, diff_p, lru_o = _inproj(
            x2d, pre_norm_g[l], w_in_b, tabs, lru_conv_w[l], lru_conv_b[l], lru_wa[l],
            lru_ba[l], lru_wx[l], lru_bx[l], lru_lambda[l], seq)
        ret_o = _retention(ret_p, ret_tabs, batch, seq)
        lam_vecs = jnp.stack([diff_lambda_q1[l], diff_lambda_k1[l],
                              diff_lambda_q2[l], diff_lambda_k2[l]])
        lam_init = 0.8 - 0.6 * math.exp(-0.3 * l)
        w_out_cur = w_out_b
        if l + 1 < depth:
            diff_o, w_in_b, w_out_b = _diff_attention(
                diff_p, lam_vecs, diff_subln_g[l], lam_init, batch, seq, (w_in, w_out, l + 1))
        else:
            diff_o, = _diff_attention(diff_p, lam_vecs, diff_subln_g[l], lam_init, batch, seq)
        x2d = _outproj(ret_o, diff_o, lru_o, w_out_cur, post_norm_g[l], x2d)
    return x2d.reshape(batch, seq, D_MODEL)
```

```python
import functools
import math

import jax
import jax.numpy as jnp
import numpy as np
from jax import lax
from jax.experimental import pallas as pl
from jax.experimental.pallas import tpu as pltpu

F32 = jnp.float32
BF16 = jnp.bfloat16

D_MODEL = 2048
HEAD_DIM = 128
RET_HEADS = 6
RET_W = RET_HEADS * HEAD_DIM
RET_CHUNK = 128
DIFF_HEADS = 4
DIFF_QK = HEAD_DIM // 2
DIFF_W = DIFF_HEADS * HEAD_DIM
LRU_W = 768
LRU_BLOCKS = 8
LRU_BW = LRU_W // LRU_BLOCKS
CONV_W = 4
LRU_C = 8.0
D_MIX = RET_W + DIFF_W + LRU_W
RET_COLS = 4 * RET_W
DIFF_COLS = 4 * DIFF_W
LRU_COLS = 2 * LRU_W
D_IN = RET_COLS + DIFF_COLS + LRU_COLS
ROPE_THETA = 10000.0
EPS = 1e-6
LOG2E = math.log2(math.e)

LANES = 128
SUBLANES = 8
VMEM_LIMIT = 52 * 1024 * 1024

ROPE_TM = 1024
INPROJ_TM = 256
INPROJ_TN = 512
RET_ROWS = 1024
ATT_T = 1024
ATT_TK = 2048
ATT_ROWS = 256
ATT_AHEAD = 8
LRU_GROUPS = 4
OUT_TM = 512
OUT_GROUPS = 2


def _silu(x):
    return x * jax.nn.sigmoid(x)


def _rope_table_kernel(pos_ref, c_ref, win_ref, wout_ref,
                       cos_r, sin_r, cos_d, sin_d, winb_ref, woutb_ref):
    winb_ref[...] = win_ref[...].astype(BF16)
    woutb_ref[...] = wout_ref[...].astype(BF16)
    pos = pos_ref[...].astype(F32)
    ang = pos * c_ref[0:1, :]
    c = jnp.cos(ang)
    s = jnp.sin(ang)
    lane = lax.broadcasted_iota(jnp.int32, ang.shape, 1)
    half = HEAD_DIM // 2
    q = DIFF_QK // 2
    low = lane < half
    cos_r[...] = jnp.where(low, c, pltpu.roll(c, half, 1))
    sin_r[...] = jnp.where(low, s, pltpu.roll(s, half, 1)) * c_ref[1:2, :]

    def spread(x):
        return jnp.where(lane < q, pltpu.roll(x, half, 1),
                         jnp.where(lane < 2 * q, pltpu.roll(x, half + q, 1),
                                   jnp.where(lane < 3 * q, x, pltpu.roll(x, q, 1))))

    cos_d[...] = spread(c)
    sin_d[...] = spread(s) * c_ref[2:3, :]


def _rope_tables(positions, w_in, w_out):
    t = positions.size
    steps = t // ROPE_TM
    half = HEAD_DIM // 2
    inv_r = 1.0 / (ROPE_THETA ** jnp.linspace(0.0, 1.0, half, dtype=F32))
    inv_d = 1.0 / (ROPE_THETA ** (jnp.arange(0, DIFF_QK, 2, dtype=F32) / DIFF_QK))
    one = jnp.ones((DIFF_QK // 2,), F32)
    consts = jnp.stack([
        jnp.concatenate([inv_r, inv_d, jnp.zeros((LANES - half - DIFF_QK // 2,), F32)]),
        jnp.concatenate([-jnp.ones((half,), F32), jnp.ones((half,), F32)]),
        jnp.concatenate([-one, one, -one, one]),
    ])
    consts = jnp.concatenate([consts, jnp.zeros((SUBLANES - 3, LANES), F32)])
    tab = jax.ShapeDtypeStruct((t, LANES), F32)
    row_spec = pl.BlockSpec((ROPE_TM, LANES), lambda i: (i, 0))
    w_specs_in, w_specs_out, w_shapes = [], [], []
    for w in (w_in, w_out):
        rows = w.shape[1] // steps
        w_specs_in.append(pl.BlockSpec((None, rows, w.shape[2]), lambda i: (0, i, 0)))
        w_specs_out.append(pl.BlockSpec((rows, w.shape[2]), lambda i: (i, 0)))
        w_shapes.append(jax.ShapeDtypeStruct(w.shape[1:], BF16))
    outs = pl.pallas_call(
        _rope_table_kernel,
        grid=(steps,),
        in_specs=[pl.BlockSpec((ROPE_TM, 1), lambda i: (i, 0)),
                  pl.BlockSpec((SUBLANES, LANES), lambda i: (0, 0))] + w_specs_in,
        out_specs=[row_spec] * 4 + w_specs_out,
        out_shape=[tab] * 4 + w_shapes,
        compiler_params=pltpu.CompilerParams(
            dimension_semantics=("arbitrary",), vmem_limit_bytes=VMEM_LIMIT),
        name="rope_tables",
    )(positions.reshape(t, 1), consts, w_in, w_out)
    return outs[:4], outs[4], outs[5]


def _rope_r(x, cos, sin):
    return x * cos + pltpu.roll(x, HEAD_DIM // 2, 1) * sin


def _rope_d(x, cos, sin, first_half):
    q = DIFF_QK // 2
    partner = jnp.where(first_half, pltpu.roll(x, LANES - q, 1), pltpu.roll(x, q, 1))
    return x * cos + partner * sin


def _inproj_kernel(x_ref, g_ref, w_ref, cr_ref, sr_ref, cd_ref, sd_ref,
                   cw_ref, cb_ref, gw_ref, gb_ref, lam_ref, never_ref,
                   ret_ref, diff_ref, lru_ref,
                   xn_scr, xs_scr, hc_scr, hs_scr, *, steps_per_seq):
    rows = x_ref.shape[0]
    pad = SUBLANES
    seq_start = pl.program_id(0) % steps_per_seq == 0

    @pl.when(seq_start)
    def _():
        xs_scr[0:pad, :] = jnp.zeros((pad, LRU_W), F32)
        hc_scr[...] = jnp.zeros_like(hc_scr)

    @pl.when(jnp.logical_not(seq_start))
    def _():
        xs_scr[0:pad, :] = xs_scr[rows:rows + pad, :]

    x = x_ref[...]
    xn_scr[...] = (x * g_ref[...]).astype(BF16)
    ms = jnp.mean(x * x, axis=-1, keepdims=True)
    rinv = jnp.broadcast_to(lax.rsqrt(ms + EPS), (rows, LANES))
    lane = lax.broadcasted_iota(jnp.int32, (rows, LANES), 1)
    first_half = (lane % DIFF_QK) < (DIFF_QK // 2)

    def proj(c0, width):
        return jnp.dot(xn_scr[...], w_ref[:, c0:c0 + width], preferred_element_type=F32)

    def emit(blk):
        c0 = blk * INPROJ_TN
        acc = proj(c0, INPROJ_TN)
        for s in range(INPROJ_TN // LANES):
            c = c0 + s * LANES
            y = acc[:, s * LANES:(s + 1) * LANES] * rinv
            if c < 2 * RET_W:
                y = _rope_r(y, cr_ref[...], sr_ref[...])
                if c >= RET_W:
                    y = y * (HEAD_DIM ** -0.5)
            elif RET_COLS <= c < RET_COLS + 2 * DIFF_W:
                y = _rope_d(y, cd_ref[...], sd_ref[...], first_half)
                if c < RET_COLS + DIFF_W:
                    y = y * (DIFF_QK ** -0.5 * LOG2E)
            y = y.astype(BF16)
            if c < RET_COLS:
                ret_ref[:, c:c + LANES] = y
            else:
                diff_ref[:, c - RET_COLS:c - RET_COLS + LANES] = y
        return acc[rows - SUBLANES:rows, INPROJ_TN - LANES:INPROJ_TN]

    lru0 = RET_COLS + DIFF_COLS
    rinv_w = jnp.tile(rinv, (1, LRU_W // LANES))
    xs_scr[pad:pad + rows, :] = proj(lru0, LRU_W) * rinv_w
    lg = proj(lru0 + LRU_W, LRU_W) * rinv_w
    xc = cb_ref[...] + xs_scr[pad - 3:pad - 3 + rows, :] * cw_ref[0:1, :]
    for j in range(1, CONV_W):
        xc = xc + xs_scr[pad - 3 + j:pad - 3 + j + rows, :] * cw_ref[j:j + 1, :]

    emit(0)
    emit(1)

    xcb = xc.astype(BF16)
    tile_w = 2 * LANES
    gate_cols = []
    for half in range(2):
        for j in range(LRU_W // tile_w):
            lo = (j * tile_w // LRU_BW) * LRU_BW
            hi = min(LRU_W, -(-((j + 1) * tile_w) // LRU_BW) * LRU_BW)
            k0 = (lo // tile_w) * tile_w
            k1 = min(LRU_W, -(-hi // tile_w) * tile_w)
            c0 = half * LRU_W + j * tile_w
            gate_cols.append(jnp.dot(xcb[:, k0:k1], gw_ref[k0:k1, c0:c0 + tile_w],
                                     preferred_element_type=F32))
    gates = jnp.concatenate(gate_cols, axis=1) + gb_ref[...]
    lam = lam_ref[...]
    log_sig = jnp.minimum(lam, 0.0) - jnp.log1p(jnp.exp(-jnp.abs(lam)))
    row = lax.broadcasted_iota(jnp.int32, (SUBLANES, LRU_W), 0)

    def lru_rows(rs, carry, after):
        never = never_ref[0] != 0
        g = gates[rs, :]
        g = jnp.where(never, jnp.tile(after, (g.shape[0] // SUBLANES, g.shape[1] // LANES)), g)
        r = jax.nn.sigmoid(g[:, :LRU_W])
        i = jax.nn.sigmoid(g[:, LRU_W:])
        log_a = (LRU_C * r) * log_sig
        a = jnp.exp(log_a)
        mult = jnp.sqrt(-jnp.tanh(log_a) * (a * a + 1.0))
        b = mult * (i * xc[rs, :])
        for t in range((rs.stop - rs.start) // SUBLANES):
            av = a[t * SUBLANES:(t + 1) * SUBLANES, :]
            bv = b[t * SUBLANES:(t + 1) * SUBLANES, :]
            for d in (1, 2, 4):
                a_prev = jnp.where(row >= d, pltpu.roll(av, d, 0), 1.0)
                b_prev = jnp.where(row >= d, pltpu.roll(bv, d, 0), 0.0)
                bv = av * b_prev + bv
                av = av * a_prev
            hv = av * carry + bv
            r0 = rs.start + t * SUBLANES
            hs_scr[r0:r0 + SUBLANES, :] = hv
            carry = jnp.broadcast_to(hv[SUBLANES - 1:SUBLANES, :], (SUBLANES, LRU_W))
        lru_ref[rs, :] = (hs_scr[rs, :] * _silu(lg[rs, :])).astype(BF16)
        return carry

    carry = hc_scr[...]
    blk = 2
    for gi in range(LRU_GROUPS):
        after = emit(blk)
        blk += 1
        carry = lru_rows(slice(gi * rows // LRU_GROUPS, (gi + 1) * rows // LRU_GROUPS),
                         carry, after)
    hc_scr[...] = carry
    while blk < lru0 // INPROJ_TN:
        emit(blk)
        blk += 1


def _block_diag(w):
    n, k, _ = w.shape
    eye = jnp.eye(n, dtype=w.dtype)
    return (w[:, :, None, :] * eye[:, None, :, None]).reshape(n * k, n * k)


def _inproj(x2d, g, w_bf16, tabs, conv_w, conv_b, wa, ba, wx, bx, lam, seq):
    t = x2d.shape[0]
    tm = INPROJ_TM
    gate_w = jnp.concatenate([_block_diag(wa), _block_diag(wx)], axis=1).astype(BF16)
    gate_b = jnp.concatenate([ba, bx]).reshape(1, 2 * LRU_W)
    tab_spec = pl.BlockSpec((tm, LANES), lambda i: (i, 0))

    def const(shape):
        return pl.BlockSpec(shape, lambda i: (0, 0))

    return pl.pallas_call(
        functools.partial(_inproj_kernel, steps_per_seq=seq // tm),
        grid=(t // tm,),
        in_specs=[pl.BlockSpec((tm, D_MODEL), lambda i: (i, 0)),
                  const((1, D_MODEL)),
                  pl.BlockSpec((D_MODEL, D_IN), lambda i: (0, 0),
                               pipeline_mode=pl.Buffered(1)),
                  tab_spec, tab_spec, tab_spec, tab_spec,
                  const((CONV_W, LRU_W)), const((1, LRU_W)),
                  pl.BlockSpec((LRU_W, 2 * LRU_W), lambda i: (0, 0),
                               pipeline_mode=pl.Buffered(1)),
                  const((1, 2 * LRU_W)), const((1, LRU_W)),
                  pl.BlockSpec(memory_space=pltpu.SMEM)],
        out_specs=[pl.BlockSpec((tm, RET_COLS), lambda i: (i, 0)),
                   pl.BlockSpec((tm, DIFF_COLS), lambda i: (i, 0)),
                   pl.BlockSpec((tm, LRU_W), lambda i: (i, 0))],
        out_shape=[jax.ShapeDtypeStruct((t, RET_COLS), BF16),
                   jax.ShapeDtypeStruct((t, DIFF_COLS), BF16),
                   jax.ShapeDtypeStruct((t, LRU_W), BF16)],
        scratch_shapes=[pltpu.VMEM((tm, D_MODEL), BF16),
                        pltpu.VMEM((tm + SUBLANES, LRU_W), F32),
                        pltpu.VMEM((SUBLANES, LRU_W), F32),
                        pltpu.VMEM((tm, LRU_W), F32)],
        compiler_params=pltpu.CompilerParams(
            dimension_semantics=("arbitrary",), vmem_limit_bytes=VMEM_LIMIT),
        name="inproj_lru",
    )(x2d, g.reshape(1, D_MODEL), w_bf16, *tabs, conv_w, conv_b.reshape(1, LRU_W),
      gate_w, gate_b, lam.reshape(1, LRU_W), jnp.zeros((1,), jnp.int32))


def _retention_tables():
    c = RET_CHUNK
    f32 = np.float32
    log_g = np.log1p(-np.power(f32(2.0), f32(-5.0) - np.arange(RET_HEADS, dtype=f32)))
    idx = np.arange(c, dtype=f32)
    rel = idx[:, None] - idx[None, :]
    intra = np.where(rel[None] >= 0,
                     np.exp(np.maximum(rel, f32(0.0))[None] * log_g[:, None, None]), f32(0.0))
    k_w = np.exp((f32(c - 1.0) - idx)[None, :] * log_g[:, None])
    q_w = np.exp((idx + f32(1.0))[None, :] * log_g[:, None])
    cdec = np.exp(f32(c) * log_g)
    full = (RET_HEADS, c, HEAD_DIM)
    return tuple(jnp.asarray(np.ascontiguousarray(t), dtype=F32) for t in (
        intra,
        np.broadcast_to(q_w[:, :, None], full),
        np.broadcast_to(k_w[:, :, None], full),
        np.broadcast_to(cdec[:, None, None], full)))


def _retention_kernel(q_ref, k_ref, v_ref, g_ref, dec_ref, qw_ref, kw_ref, cd_ref,
                      o_ref, state_ref):
    @pl.when(pl.program_id(1) == 0)
    def _():
        state_ref[...] = jnp.zeros_like(state_ref)

    nt = (((1,), (1,)), ((), ()))
    tn = (((0,), (0,)), ((), ()))
    for c in range(RET_ROWS // RET_CHUNK):
        rows = slice(c * RET_CHUNK, (c + 1) * RET_CHUNK)
        for h in range(RET_HEADS):
            cols = slice(h * HEAD_DIM, (h + 1) * HEAD_DIM)
            q = q_ref[rows, cols]
            k = k_ref[rows, cols]
            v = v_ref[rows, cols]
            st = state_ref[h]
            rhs = jnp.concatenate([k, st.astype(BF16)], axis=0)
            r = lax.dot_general(q, rhs, nt, preferred_element_type=F32)
            s = r[:, :RET_CHUNK] * dec_ref[h]
            o = (jnp.dot(s.astype(BF16), v, preferred_element_type=F32)
                 + r[:, RET_CHUNK:] * qw_ref[h])
            ks = (k.astype(F32) * kw_ref[h]).astype(BF16)
            kv_t = lax.dot_general(v, ks, tn, preferred_element_type=F32)
            state_ref[h] = st * cd_ref[h] + kv_t
            ms = jnp.mean(o * o, axis=-1, keepdims=True)
            y = o * lax.rsqrt(ms + EPS)
            o_ref[rows, cols] = (y * _silu(g_ref[rows, cols].astype(F32))).astype(BF16)


def _retention(ret_p, tables, batch, seq):
    t = batch * seq
    steps = seq // RET_ROWS

    def col(j):
        return pl.BlockSpec((RET_ROWS, RET_W), lambda b, i, j=j: (b * steps + i, j))

    tab_spec = pl.BlockSpec((RET_HEADS, RET_CHUNK, HEAD_DIM), lambda b, i: (0, 0, 0))
    return pl.pallas_call(
        _retention_kernel,
        grid=(batch, steps),
        in_specs=[col(0), col(1), col(2), col(3), tab_spec, tab_spec, tab_spec, tab_spec],
        out_specs=pl.BlockSpec((RET_ROWS, RET_W), lambda b, i: (b * steps + i, 0)),
        out_shape=jax.ShapeDtypeStruct((t, RET_W), BF16),
        scratch_shapes=[pltpu.VMEM((RET_HEADS, HEAD_DIM, HEAD_DIM), F32)],
        compiler_params=pltpu.CompilerParams(
            dimension_semantics=("arbitrary", "arbitrary"), vmem_limit_bytes=VMEM_LIMIT),
        name="retention",
    )(ret_p, ret_p, ret_p, ret_p, *tables)


def _diff_kernel(lam_ref, q_ref, k_ref, v_ref, g_ref, subg_ref, *refs, lam_init, cast_next):
    if cast_next:
        win_ref, wout_ref, o_ref, winb_ref, woutb_ref, qm_scr, m_scr, acc_scr, v1_scr = refs
        winb_ref[...] = win_ref[...].astype(BF16)
        woutb_ref[...] = wout_ref[...].astype(BF16)
    else:
        o_ref, qm_scr, m_scr, acc_scr, v1_scr = refs
    qi = pl.program_id(2)

    @pl.when(qi == 0)
    def _():
        v1_scr[:, 0:HEAD_DIM] = v_ref[...]
        v1_scr[:, HEAD_DIM:2 * HEAD_DIM] = jnp.ones(v_ref.shape, BF16)

    q = q_ref[...]
    lane = lax.broadcasted_iota(jnp.int32, q.shape, 1)
    qm_scr[0:ATT_T, :] = jnp.where(lane < DIFF_QK, q, jnp.zeros_like(q))
    qm_scr[ATT_T:2 * ATT_T, :] = jnp.where(lane >= DIFF_QK, q, jnp.zeros_like(q))
    nt = (((1,), (1,)), ((), ()))

    def lane_fold(x, op):
        parts = [x[:, i * LANES:(i + 1) * LANES] for i in range(x.shape[1] // LANES)]
        while len(parts) > 1:
            nxt = [op(parts[i], parts[i + 1]) for i in range(0, len(parts) - 1, 2)]
            parts = nxt + parts[len(parts) - len(parts) % 2:]
        return parts[0]

    def block(r0, size, diagonal):
        k = k_ref[pl.ds(r0, size), :]
        v = v1_scr[pl.ds(r0, size), :]
        n_chunks = 2 * ATT_T // ATT_ROWS

        def scores(i):
            rows = slice(i * ATT_ROWS, (i + 1) * ATT_ROWS)
            keys = (i * ATT_ROWS) % ATT_T + ATT_ROWS if diagonal else size
            s = lax.dot_general(qm_scr[rows, :], k[:keys, :], nt, preferred_element_type=F32)
            return rows, keys, s

        queued = [scores(i) for i in range(min(ATT_AHEAD, n_chunks))]
        for i in range(n_chunks):
            rows, keys, s = queued.pop(0)
            if i + ATT_AHEAD < n_chunks:
                queued.append(scores(i + ATT_AHEAD))
            if diagonal:
                tail = s[:, keys - ATT_ROWS:]
                row = lax.broadcasted_iota(jnp.int32, tail.shape, 0)
                col = lax.broadcasted_iota(jnp.int32, tail.shape, 1)
                tail = jnp.where(col <= row, tail, -jnp.inf)
                s = tail if keys == ATT_ROWS else jnp.concatenate(
                    [s[:, :keys - ATT_ROWS], tail], axis=1)
            m_cur = jnp.max(lane_fold(s, jnp.maximum), axis=-1, keepdims=True)
            if diagonal:
                m_new = jnp.broadcast_to(m_cur, (ATT_ROWS, LANES))
            else:
                m_prev = m_scr[rows, :]
                m_new = jnp.maximum(m_prev, m_cur)
                alpha = jnp.exp2(m_prev - m_new)
            p = jnp.exp2(s - jnp.tile(m_new, (1, keys // LANES)))
            acc_new = jnp.dot(p.astype(BF16), v[:keys, :], preferred_element_type=F32)
            if not diagonal:
                acc_new = jnp.tile(alpha, (1, 2)) * acc_scr[rows, :] + acc_new
            acc_scr[rows, :] = acc_new
            m_scr[rows, :] = m_new

    def body(kk, carry):
        block(pl.multiple_of(kk * ATT_TK, ATT_TK), ATT_TK, False)
        return carry

    below = qi * ATT_T
    block(pl.multiple_of(below, ATT_T), ATT_T, True)
    n_big = below // ATT_TK
    lax.fori_loop(0, n_big, body, 0)
    done = n_big * ATT_TK
    size = ATT_TK // 2
    while size >= ATT_T:
        take = below - done >= size

        @pl.when(take)
        def _(done=done, size=size):
            block(pl.multiple_of(done, ATT_T), size, False)

        done = done + jnp.where(take, size, 0)
        size //= 2

    lv = lam_ref[...]
    lam = (jnp.exp(jnp.sum(lv[0:1] * lv[1:2], axis=-1, keepdims=True))
           - jnp.exp(jnp.sum(lv[2:3] * lv[3:4], axis=-1, keepdims=True)) + lam_init)
    o = (acc_scr[0:ATT_T, :HEAD_DIM] / acc_scr[0:ATT_T, HEAD_DIM:]
         - lam * (acc_scr[ATT_T:2 * ATT_T, :HEAD_DIM] / acc_scr[ATT_T:2 * ATT_T, HEAD_DIM:]))
    ms = jnp.mean(o * o, axis=-1, keepdims=True)
    y = ((o * lax.rsqrt(ms + EPS)) * subg_ref[...]) * (1.0 - lam_init)
    o_ref[...] = (y * _silu(g_ref[...].astype(F32))).astype(BF16)


def _diff_attention(diff_p, lam_vecs, subg, lam_init, batch, seq, next_weights=None):
    t = batch * seq
    nq = seq // ATT_T
    steps = batch * DIFF_HEADS * nq
    in_specs = [
        pl.BlockSpec((4, DIFF_QK), lambda b, h, i: (0, 0)),
        pl.BlockSpec((ATT_T, HEAD_DIM), lambda b, h, i: (b * nq + i, h)),
        pl.BlockSpec((seq, HEAD_DIM), lambda b, h, i: (b, DIFF_HEADS + h)),
        pl.BlockSpec((seq, HEAD_DIM), lambda b, h, i: (b, 2 * DIFF_HEADS + h)),
        pl.BlockSpec((ATT_T, HEAD_DIM), lambda b, h, i: (b * nq + i, 3 * DIFF_HEADS + h)),
        pl.BlockSpec((1, HEAD_DIM), lambda b, h, i: (0, 0)),
    ]
    out_specs = [pl.BlockSpec((ATT_T, HEAD_DIM), lambda b, h, i: (b * nq + i, h))]
    out_shape = [jax.ShapeDtypeStruct((t, DIFF_W), BF16)]
    args = [lam_vecs, diff_p, diff_p, diff_p, diff_p, subg.reshape(1, HEAD_DIM)]
    if next_weights is not None:
        w_in, w_out, layer = next_weights

        def slab(b, h, i):
            return (b * DIFF_HEADS + h) * nq + i

        for w in (w_in, w_out):
            rows = w.shape[1] // steps
            in_specs.append(pl.BlockSpec((None, rows, w.shape[2]),
                                         lambda b, h, i: (layer, slab(b, h, i), 0)))
            out_specs.append(pl.BlockSpec((rows, w.shape[2]),
                                          lambda b, h, i: (slab(b, h, i), 0)))
            out_shape.append(jax.ShapeDtypeStruct(w.shape[1:], BF16))
            args.append(w)
    return pl.pallas_call(
        functools.partial(_diff_kernel, lam_init=lam_init,
                          cast_next=next_weights is not None),
        grid=(batch, DIFF_HEADS, nq),
        in_specs=in_specs,
        out_specs=out_specs,
        out_shape=out_shape,
        scratch_shapes=[pltpu.VMEM((2 * ATT_T, HEAD_DIM), BF16),
                        pltpu.VMEM((2 * ATT_T, LANES), F32),
                        pltpu.VMEM((2 * ATT_T, 2 * HEAD_DIM), F32),
                        pltpu.VMEM((seq, 2 * HEAD_DIM), BF16)],
        compiler_params=pltpu.CompilerParams(
            dimension_semantics=("arbitrary", "arbitrary", "arbitrary"),
            vmem_limit_bytes=VMEM_LIMIT),
        name="diff_attention",
    )(*args)


def _outproj_kernel(ret_ref, diff_ref, lru_ref, w_ref, g_ref, x_ref, o_ref):
    rows = x_ref.shape[0] // OUT_GROUPS
    for i in range(OUT_GROUPS):
        rs = slice(i * rows, (i + 1) * rows)
        y = jnp.dot(ret_ref[rs, :], w_ref[0:RET_W, :], preferred_element_type=F32)
        y = y + jnp.dot(diff_ref[rs, :], w_ref[RET_W:RET_W + DIFF_W, :],
                        preferred_element_type=F32)
        y = y + jnp.dot(lru_ref[rs, :], w_ref[RET_W + DIFF_W:D_MIX, :],
                        preferred_element_type=F32)
        ms = jnp.mean(y * y, axis=-1, keepdims=True)
        o_ref[rs, :] = x_ref[rs, :] + (y * lax.rsqrt(ms + EPS)) * g_ref[...]


def _outproj(ret_o, diff_o, lru_o, w_bf16, g, x2d):
    t = x2d.shape[0]
    tm = OUT_TM
    return pl.pallas_call(
        _outproj_kernel,
        grid=(t // tm,),
        in_specs=[pl.BlockSpec((tm, RET_W), lambda i: (i, 0)),
                  pl.BlockSpec((tm, DIFF_W), lambda i: (i, 0)),
                  pl.BlockSpec((tm, LRU_W), lambda i: (i, 0)),
                  pl.BlockSpec((D_MIX, D_MODEL), lambda i: (0, 0),
                               pipeline_mode=pl.Buffered(1)),
                  pl.BlockSpec((1, D_MODEL), lambda i: (0, 0)),
                  pl.BlockSpec((tm, D_MODEL), lambda i: (i, 0))],
        out_specs=pl.BlockSpec((tm, D_MODEL), lambda i: (i, 0)),
        out_shape=jax.ShapeDtypeStruct((t, D_MODEL), F32),
        compiler_params=pltpu.CompilerParams(
            dimension_semantics=("arbitrary",), vmem_limit_bytes=VMEM_LIMIT),
        name="outproj",
    )(ret_o, diff_o, lru_o, w_bf16, g.reshape(1, D_MODEL), x2d)


def kernel(x, positions, pre_norm_g, w_in, diff_lambda_q1, diff_lambda_k1, diff_lambda_q2,
           diff_lambda_k2, diff_subln_g, lru_conv_w, lru_conv_b, lru_wa, lru_ba, lru_wx,
           lru_bx, lru_lambda, w_out, post_norm_g):
    batch, seq, _ = x.shape
    depth = w_in.shape[0]
    assert seq % ATT_T == 0 and seq % RET_ROWS == 0 and seq % INPROJ_TM == 0
    bf16_rows = 2 * SUBLANES
    for steps in (batch * seq // ROPE_TM, batch * DIFF_HEADS * (seq // ATT_T)):
        assert D_MODEL % (steps * bf16_rows) == 0 and D_MIX % (steps * bf16_rows) == 0
    x2d = x.reshape(batch * seq, D_MODEL)
    tabs, w_in_b, w_out_b = _rope_tables(positions, w_in, w_out)
    ret_tabs = _retention_tables()
    for l in range(depth):
        ret_p, diff_p, lru_o = _inproj(
            x2d, pre_norm_g[l], w_in_b, tabs, lru_conv_w[l], lru_conv_b[l], lru_wa[l],
            lru_ba[l], lru_wx[l], lru_bx[l], lru_lambda[l], seq)
        ret_o = _retention(ret_p, ret_tabs, batch, seq)
        lam_vecs = jnp.stack([diff_lambda_q1[l], diff_lambda_k1[l],
                              diff_lambda_q2[l], diff_lambda_k2[l]])
        lam_init = 0.8 - 0.6 * math.exp(-0.3 * l)
        w_out_cur = w_out_b
        if l + 1 < depth:
            diff_o, w_in_b, w_out_b = _diff_attention(
                diff_p, lam_vecs, diff_subln_g[l], lam_init, batch, seq, (w_in, w_out, l + 1))
        else:
            diff_o, = _diff_attention(diff_p, lam_vecs, diff_subln_g[l], lam_init, batch, seq)
        x2d = _outproj(ret_o, diff_o, lru_o, w_out_cur, post_norm_g[l], x2d)
    return x2d.reshape(batch, seq, D_MODEL)
```

```python
import functools
import math

import jax
import jax.numpy as jnp
import numpy as np
from jax import lax
from jax.experimental import pallas as pl
from jax.experimental.pallas import tpu as pltpu

F32 = jnp.float32
BF16 = jnp.bfloat16

D_MODEL = 2048
HEAD_DIM = 128
RET_HEADS = 6
RET_W = RET_HEADS * HEAD_DIM
RET_CHUNK = 128
DIFF_HEADS = 4
DIFF_QK = HEAD_DIM // 2
DIFF_W = DIFF_HEADS * HEAD_DIM
LRU_W = 768
LRU_BLOCKS = 8
LRU_BW = LRU_W // LRU_BLOCKS
CONV_W = 4
LRU_C = 8.0
D_MIX = RET_W + DIFF_W + LRU_W
RET_COLS = 4 * RET_W
DIFF_COLS = 4 * DIFF_W
LRU_COLS = 2 * LRU_W
D_IN = RET_COLS + DIFF_COLS + LRU_COLS
ROPE_THETA = 10000.0
EPS = 1e-6
LOG2E = math.log2(math.e)

LANES = 128
SUBLANES = 8
VMEM_LIMIT = 52 * 1024 * 1024

ROPE_TM = 1024
INPROJ_TM = 256
INPROJ_TN = 512
RET_ROWS = 1024
ATT_T = 1024
ATT_TK = 2048
ATT_ROWS = 256
ATT_AHEAD = 8
LRU_GROUPS = 4
OUT_TM = 512
OUT_GROUPS = 2


def _silu(x):
    return x * jax.nn.sigmoid(x)


def _rope_table_kernel(pos_ref, c_ref, win_ref, wout_ref,
                       cos_r, sin_r, cos_d, sin_d, winb_ref, woutb_ref):
    winb_ref[...] = win_ref[...].astype(BF16)
    woutb_ref[...] = wout_ref[...].astype(BF16)
    pos = pos_ref[...].astype(F32)
    ang = pos * c_ref[0:1, :]
    c = jnp.cos(ang)
    s = jnp.sin(ang)
    lane = lax.broadcasted_iota(jnp.int32, ang.shape, 1)
    half = HEAD_DIM // 2
    q = DIFF_QK // 2
    low = lane < half
    cos_r[...] = jnp.where(low, c, pltpu.roll(c, half, 1))
    sin_r[...] = jnp.where(low, s, pltpu.roll(s, half, 1)) * c_ref[1:2, :]

    def spread(x):
        return jnp.where(lane < q, pltpu.roll(x, half, 1),
                         jnp.where(lane < 2 * q, pltpu.roll(x, half + q, 1),
                                   jnp.where(lane < 3 * q, x, pltpu.roll(x, q, 1))))

    cos_d[...] = spread(c)
    sin_d[...] = spread(s) * c_ref[2:3, :]


def _rope_tables(positions, w_in, w_out):
    t = positions.size
    steps = t // ROPE_TM
    half = HEAD_DIM // 2
    inv_r = 1.0 / (ROPE_THETA ** jnp.linspace(0.0, 1.0, half, dtype=F32))
    inv_d = 1.0 / (ROPE_THETA ** (jnp.arange(0, DIFF_QK, 2, dtype=F32) / DIFF_QK))
    one = jnp.ones((DIFF_QK // 2,), F32)
    consts = jnp.stack([
        jnp.concatenate([inv_r, inv_d, jnp.zeros((LANES - half - DIFF_QK // 2,), F32)]),
        jnp.concatenate([-jnp.ones((half,), F32), jnp.ones((half,), F32)]),
        jnp.concatenate([-one, one, -one, one]),
    ])
    consts = jnp.concatenate([consts, jnp.zeros((SUBLANES - 3, LANES), F32)])
    tab = jax.ShapeDtypeStruct((t, LANES), F32)
    row_spec = pl.BlockSpec((ROPE_TM, LANES), lambda i: (i, 0))
    w_specs_in, w_specs_out, w_shapes = [], [], []
    for w in (w_in, w_out):
        rows = w.shape[1] // steps
        w_specs_in.append(pl.BlockSpec((None, rows, w.shape[2]), lambda i: (0, i, 0)))
        w_specs_out.append(pl.BlockSpec((rows, w.shape[2]), lambda i: (i, 0)))
        w_shapes.append(jax.ShapeDtypeStruct(w.shape[1:], BF16))
    outs = pl.pallas_call(
        _rope_table_kernel,
        grid=(steps,),
        in_specs=[pl.BlockSpec((ROPE_TM, 1), lambda i: (i, 0)),
                  pl.BlockSpec((SUBLANES, LANES), lambda i: (0, 0))] + w_specs_in,
        out_specs=[row_spec] * 4 + w_specs_out,
        out_shape=[tab] * 4 + w_shapes,
        compiler_params=pltpu.CompilerParams(
            dimension_semantics=("arbitrary",), vmem_limit_bytes=VMEM_LIMIT),
        name="rope_tables",
    )(positions.reshape(t, 1), consts, w_in, w_out)
    return outs[:4], outs[4], outs[5]


def _rope_r(x, cos, sin):
    return x * cos + pltpu.roll(x, HEAD_DIM // 2, 1) * sin


def _rope_d(x, cos, sin, first_half):
    q = DIFF_QK // 2
    partner = jnp.where(first_half, pltpu.roll(x, LANES - q, 1), pltpu.roll(x, q, 1))
    return x * cos + partner * sin


def _inproj_kernel(x_ref, g_ref, w_ref, cr_ref, sr_ref, cd_ref, sd_ref,
                   cw_ref, cb_ref, gw_ref, gb_ref, lam_ref, never_ref,
                   ret_ref, diff_ref, lru_ref,
                   xn_scr, xs_scr, hc_scr, hs_scr, *, steps_per_seq):
    rows = x_ref.shape[0]
    pad = SUBLANES
    seq_start = pl.program_id(0) % steps_per_seq == 0

    @pl.when(seq_start)
    def _():
        xs_scr[0:pad, :] = jnp.zeros((pad, LRU_W), F32)
        hc_scr[...] = jnp.zeros_like(hc_scr)

    @pl.when(jnp.logical_not(seq_start))
    def _():
        xs_scr[0:pad, :] = xs_scr[rows:rows + pad, :]

    x = x_ref[...]
    xn_scr[...] = (x * g_ref[...]).astype(BF16)
    ms = jnp.mean(x * x, axis=-1, keepdims=True)
    rinv = jnp.broadcast_to(lax.rsqrt(ms + EPS), (rows, LANES))
    lane = lax.broadcasted_iota(jnp.int32, (rows, LANES), 1)
    first_half = (lane % DIFF_QK) < (DIFF_QK // 2)

    def proj(c0, width):
        return jnp.dot(xn_scr[...], w_ref[:, c0:c0 + width], preferred_element_type=F32)

    def emit(blk):
        c0 = blk * INPROJ_TN
        acc = proj(c0, INPROJ_TN)
        for s in range(INPROJ_TN // LANES):
            c = c0 + s * LANES
            y = acc[:, s * LANES:(s + 1) * LANES] * rinv
            if c < 2 * RET_W:
                y = _rope_r(y, cr_ref[...], sr_ref[...])
                if c >= RET_W:
                    y = y * (HEAD_DIM ** -0.5)
            elif RET_COLS <= c < RET_COLS + 2 * DIFF_W:
                y = _rope_d(y, cd_ref[...], sd_ref[...], first_half)
                if c < RET_COLS + DIFF_W:
                    y = y * (DIFF_QK ** -0.5 * LOG2E)
            y = y.astype(BF16)
            if c < RET_COLS:
                ret_ref[:, c:c + LANES] = y
            else:
                diff_ref[:, c - RET_COLS:c - RET_COLS + LANES] = y
        return acc[rows - SUBLANES:rows, INPROJ_TN - LANES:INPROJ_TN]

    lru0 = RET_COLS + DIFF_COLS
    rinv_w = jnp.tile(rinv, (1, LRU_W // LANES))
    xs_scr[pad:pad + rows, :] = proj(lru0, LRU_W) * rinv_w
    lg = proj(lru0 + LRU_W, LRU_W) * rinv_w
    xc = cb_ref[...] + xs_scr[pad - 3:pad - 3 + rows, :] * cw_ref[0:1, :]
    for j in range(1, CONV_W):
        xc = xc + xs_scr[pad - 3 + j:pad - 3 + j + rows, :] * cw_ref[j:j + 1, :]

    emit(0)
    emit(1)

    xcb = xc.astype(BF16)
    tile_w = 2 * LANES
    gate_cols = []
    for half in range(2):
        for j in range(LRU_W // tile_w):
            lo = (j * tile_w // LRU_BW) * LRU_BW
            hi = min(LRU_W, -(-((j + 1) * tile_w) // LRU_BW) * LRU_BW)
            k0 = (lo // tile_w) * tile_w
            k1 = min(LRU_W, -(-hi // tile_w) * tile_w)
            c0 = half * LRU_W + j * tile_w
            gate_cols.append(jnp.dot(xcb[:, k0:k1], gw_ref[k0:k1, c0:c0 + tile_w],
                                     preferred_element_type=F32))
    gates = jnp.concatenate(gate_cols, axis=1) + gb_ref[...]
    lam = lam_ref[...]
    log_sig = jnp.minimum(lam, 0.0) - jnp.log1p(jnp.exp(-jnp.abs(lam)))
    row = lax.broadcasted_iota(jnp.int32, (SUBLANES, LRU_W), 0)

    def lru_rows(rs, carry, after):
        never = never_ref[0] != 0
        g = gates[rs, :]
        g = jnp.where(never, jnp.tile(after, (g.shape[0] // SUBLANES, g.shape[1] // LANES)), g)
        r = jax.nn.sigmoid(g[:, :LRU_W])
        i = jax.nn.sigmoid(g[:, LRU_W:])
        log_a = (LRU_C * r) * log_sig
        a = jnp.exp(log_a)
        mult = jnp.sqrt(-jnp.tanh(log_a) * (a * a + 1.0))
        b = mult * (i * xc[rs, :])
        for t in range((rs.stop - rs.start) // SUBLANES):
            av = a[t * SUBLANES:(t + 1) * SUBLANES, :]
            bv = b[t * SUBLANES:(t + 1) * SUBLANES, :]
            for d in (1, 2, 4):
                a_prev = jnp.where(row >= d, pltpu.roll(av, d, 0), 1.0)
                b_prev = jnp.where(row >= d, pltpu.roll(bv, d, 0), 0.0)
                bv = av * b_prev + bv
                av = av * a_prev
            hv = av * carry + bv
            r0 = rs.start + t * SUBLANES
            hs_scr[r0:r0 + SUBLANES, :] = hv
            carry = jnp.broadcast_to(hv[SUBLANES - 1:SUBLANES, :], (SUBLANES, LRU_W))
        lru_ref[rs, :] = (hs_scr[rs, :] * _silu(lg[rs, :])).astype(BF16)
        return carry

    carry = hc_scr[...]
    blk = 2
    for gi in range(LRU_GROUPS):
        after = emit(blk)
        blk += 1
        carry = lru_rows(slice(gi * rows // LRU_GROUPS, (gi + 1) * rows // LRU_GROUPS),
                         carry, after)
    hc_scr[...] = carry
    while blk < lru0 // INPROJ_TN:
        emit(blk)
        blk += 1


def _block_diag(w):
    n, k, _ = w.shape
    eye = jnp.eye(n, dtype=w.dtype)
    return (w[:, :, None, :] * eye[:, None, :, None]).reshape(n * k, n * k)


def _inproj(x2d, g, w_bf16, tabs, conv_w, conv_b, wa, ba, wx, bx, lam, seq):
    t = x2d.shape[0]
    tm = INPROJ_TM
    gate_w = jnp.concatenate([_block_diag(wa), _block_diag(wx)], axis=1).astype(BF16)
    gate_b = jnp.concatenate([ba, bx]).reshape(1, 2 * LRU_W)
    tab_spec = pl.BlockSpec((tm, LANES), lambda i: (i, 0))

    def const(shape):
        return pl.BlockSpec(shape, lambda i: (0, 0))

    return pl.pallas_call(
        functools.partial(_inproj_kernel, steps_per_seq=seq // tm),
        grid=(t // tm,),
        in_specs=[pl.BlockSpec((tm, D_MODEL), lambda i: (i, 0)),
                  const((1, D_MODEL)),
                  pl.BlockSpec((D_MODEL, D_IN), lambda i: (0, 0),
                               pipeline_mode=pl.Buffered(1)),
                  tab_spec, tab_spec, tab_spec, tab_spec,
                  const((CONV_W, LRU_W)), const((1, LRU_W)),
                  pl.BlockSpec((LRU_W, 2 * LRU_W), lambda i: (0, 0),
                               pipeline_mode=pl.Buffered(1)),
                  const((1, 2 * LRU_W)), const((1, LRU_W)),
                  pl.BlockSpec(memory_space=pltpu.SMEM)],
        out_specs=[pl.BlockSpec((tm, RET_COLS), lambda i: (i, 0)),
                   pl.BlockSpec((tm, DIFF_COLS), lambda i: (i, 0)),
                   pl.BlockSpec((tm, LRU_W), lambda i: (i, 0))],
        out_shape=[jax.ShapeDtypeStruct((t, RET_COLS), BF16),
                   jax.ShapeDtypeStruct((t, DIFF_COLS), BF16),
                   jax.ShapeDtypeStruct((t, LRU_W), BF16)],
        scratch_shapes=[pltpu.VMEM((tm, D_MODEL), BF16),
                        pltpu.VMEM((tm + SUBLANES, LRU_W), F32),
                        pltpu.VMEM((SUBLANES, LRU_W), F32),
                        pltpu.VMEM((tm, LRU_W), F32)],
        compiler_params=pltpu.CompilerParams(
            dimension_semantics=("arbitrary",), vmem_limit_bytes=VMEM_LIMIT),
        name="inproj_lru",
    )(x2d, g.reshape(1, D_MODEL), w_bf16, *tabs, conv_w, conv_b.reshape(1, LRU_W),
      gate_w, gate_b, lam.reshape(1, LRU_W), jnp.zeros((1,), jnp.int32))


def _retention_tables():
    c = RET_CHUNK
    f32 = np.float32
    log_g = np.log1p(-np.power(f32(2.0), f32(-5.0) - np.arange(RET_HEADS, dtype=f32)))
    idx = np.arange(c, dtype=f32)
    rel = idx[:, None] - idx[None, :]
    intra = np.where(rel[None] >= 0,
                     np.exp(np.maximum(rel, f32(0.0))[None] * log_g[:, None, None]), f32(0.0))
    k_w = np.exp((f32(c - 1.0) - idx)[None, :] * log_g[:, None])
    q_w = np.exp((idx + f32(1.0))[None, :] * log_g[:, None])
    cdec = np.exp(f32(c) * log_g)
    full = (RET_HEADS, c, HEAD_DIM)
    return tuple(jnp.asarray(np.ascontiguousarray(t), dtype=F32) for t in (
        intra,
        np.broadcast_to(q_w[:, :, None], full),
        np.broadcast_to(k_w[:, :, None], full),
        np.broadcast_to(cdec[:, None, None], full)))


def _retention_kernel(q_ref, k_ref, v_ref, g_ref, dec_ref, qw_ref, kw_ref, cd_ref,
                      o_ref, state_ref):
    @pl.when(pl.program_id(1) == 0)
    def _():
        state_ref[...] = jnp.zeros_like(state_ref)

    nt = (((1,), (1,)), ((), ()))
    tn = (((0,), (0,)), ((), ()))
    for c in range(RET_ROWS // RET_CHUNK):
        rows = slice(c * RET_CHUNK, (c + 1) * RET_CHUNK)
        for h in range(RET_HEADS):
            cols = slice(h * HEAD_DIM, (h + 1) * HEAD_DIM)
            q = q_ref[rows, cols]
            k = k_ref[rows, cols]
            v = v_ref[rows, cols]
            st = state_ref[h]
            rhs = jnp.concatenate([k, st.astype(BF16)], axis=0)
            r = lax.dot_general(q, rhs, nt, preferred_element_type=F32)
            s = r[:, :RET_CHUNK] * dec_ref[h]
            o = (jnp.dot(s.astype(BF16), v, preferred_element_type=F32)
                 + r[:, RET_CHUNK:] * qw_ref[h])
            ks = (k.astype(F32) * kw_ref[h]).astype(BF16)
            kv_t = lax.dot_general(v, ks, tn, preferred_element_type=F32)
            state_ref[h] = st * cd_ref[h] + kv_t
            ms = jnp.mean(o * o, axis=-1, keepdims=True)
            y = o * lax.rsqrt(ms + EPS)
            o_ref[rows, cols] = (y * _silu(g_ref[rows, cols].astype(F32))).astype(BF16)


def _retention(ret_p, tables, batch, seq):
    t = batch * seq
    steps = seq // RET_ROWS

    def col(j):
        return pl.BlockSpec((RET_ROWS, RET_W), lambda b, i, j=j: (b * steps + i, j))

    tab_spec = pl.BlockSpec((RET_HEADS, RET_CHUNK, HEAD_DIM), lambda b, i: (0, 0, 0))
    return pl.pallas_call(
        _retention_kernel,
        grid=(batch, steps),
        in_specs=[col(0), col(1), col(2), col(3), tab_spec, tab_spec, tab_spec, tab_spec],
        out_specs=pl.BlockSpec((RET_ROWS, RET_W), lambda b, i: (b * steps + i, 0)),
        out_shape=jax.ShapeDtypeStruct((t, RET_W), BF16),
        scratch_shapes=[pltpu.VMEM((RET_HEADS, HEAD_DIM, HEAD_DIM), F32)],
        compiler_params=pltpu.CompilerParams(
            dimension_semantics=("arbitrary", "arbitrary"), vmem_limit_bytes=VMEM_LIMIT),
        name="retention",
    )(ret_p, ret_p, ret_p, ret_p, *tables)


def _diff_kernel(lam_ref, q_ref, k_ref, v_ref, g_ref, subg_ref, *refs, lam_init, cast_next):
    if cast_next:
        win_ref, wout_ref, o_ref, winb_ref, woutb_ref, qm_scr, m_scr, acc_scr, v1_scr = refs
        winb_ref[...] = win_ref[...].astype(BF16)
        woutb_ref[...] = wout_ref[...].astype(BF16)
    else:
        o_ref, qm_scr, m_scr, acc_scr, v1_scr = refs
    qi = pl.program_id(2)

    @pl.when(qi == 0)
    def _():
        v1_scr[:, 0:HEAD_DIM] = v_ref[...]
        v1_scr[:, HEAD_DIM:2 * HEAD_DIM] = jnp.ones(v_ref.shape, BF16)

    q = q_ref[...]
    lane = lax.broadcasted_iota(jnp.int32, q.shape, 1)
    qm_scr[0:ATT_T, :] = jnp.where(lane < DIFF_QK, q, jnp.zeros_like(q))
    qm_scr[ATT_T:2 * ATT_T, :] = jnp.where(lane >= DIFF_QK, q, jnp.zeros_like(q))
    nt = (((1,), (1,)), ((), ()))

    def lane_fold(x, op):
        parts = [x[:, i * LANES:(i + 1) * LANES] for i in range(x.shape[1] // LANES)]
        while len(parts) > 1:
            nxt = [op(parts[i], parts[i + 1]) for i in range(0, len(parts) - 1, 2)]
            parts = nxt + parts[len(parts) - len(parts) % 2:]
        return parts[0]

    def block(r0, size, diagonal):
        k = k_ref[pl.ds(r0, size), :]
        v = v1_scr[pl.ds(r0, size), :]
        n_chunks = 2 * ATT_T // ATT_ROWS

        def scores(i):
            rows = slice(i * ATT_ROWS, (i + 1) * ATT_ROWS)
            keys = size - ATT_T + (i * ATT_ROWS) % ATT_T + ATT_ROWS if diagonal else size
            s = lax.dot_general(qm_scr[rows, :], k[:keys, :], nt, preferred_element_type=F32)
            return rows, keys, s

        queued = [scores(i) for i in range(min(ATT_AHEAD, n_chunks))]
        for i in range(n_chunks):
            rows, keys, s = queued.pop(0)
            if i + ATT_AHEAD < n_chunks:
                queued.append(scores(i + ATT_AHEAD))
            if diagonal:
                tail = s[:, keys - ATT_ROWS:]
                row = lax.broadcasted_iota(jnp.int32, tail.shape, 0)
                col = lax.broadcasted_iota(jnp.int32, tail.shape, 1)
                tail = jnp.where(col <= row, tail, -jnp.inf)
                s = tail if keys == ATT_ROWS else jnp.concatenate(
                    [s[:, :keys - ATT_ROWS], tail], axis=1)
            m_cur = jnp.max(lane_fold(s, jnp.maximum), axis=-1, keepdims=True)
            if diagonal:
                m_new = jnp.broadcast_to(m_cur, (ATT_ROWS, LANES))
            else:
                m_prev = m_scr[rows, :]
                m_new = jnp.maximum(m_prev, m_cur)
                alpha = jnp.exp2(m_prev - m_new)
            p = jnp.exp2(s - jnp.tile(m_new, (1, keys // LANES)))
            acc_new = jnp.dot(p.astype(BF16), v[:keys, :], preferred_element_type=F32)
            if not diagonal:
                acc_new = jnp.tile(alpha, (1, 2)) * acc_scr[rows, :] + acc_new
            acc_scr[rows, :] = acc_new
            m_scr[rows, :] = m_new

    def body(kk, carry):
        block(pl.multiple_of(kk * ATT_TK, ATT_TK), ATT_TK, False)
        return carry

    below = qi * ATT_T
    n_big = below // ATT_TK
    odd = below - n_big * ATT_TK > 0

    @pl.when(odd)
    def _():
        block(pl.multiple_of(below - ATT_T, ATT_T), 2 * ATT_T, True)

    @pl.when(jnp.logical_not(odd))
    def _():
        block(pl.multiple_of(below, ATT_T), ATT_T, True)

    lax.fori_loop(0, n_big, body, 0)

    lv = lam_ref[...]
    lam = (jnp.exp(jnp.sum(lv[0:1] * lv[1:2], axis=-1, keepdims=True))
           - jnp.exp(jnp.sum(lv[2:3] * lv[3:4], axis=-1, keepdims=True)) + lam_init)
    o = (acc_scr[0:ATT_T, :HEAD_DIM] / acc_scr[0:ATT_T, HEAD_DIM:]
         - lam * (acc_scr[ATT_T:2 * ATT_T, :HEAD_DIM] / acc_scr[ATT_T:2 * ATT_T, HEAD_DIM:]))
    ms = jnp.mean(o * o, axis=-1, keepdims=True)
    y = ((o * lax.rsqrt(ms + EPS)) * subg_ref[...]) * (1.0 - lam_init)
    o_ref[...] = (y * _silu(g_ref[...].astype(F32))).astype(BF16)


def _diff_attention(diff_p, lam_vecs, subg, lam_init, batch, seq, next_weights=None):
    t = batch * seq
    nq = seq // ATT_T
    steps = batch * DIFF_HEADS * nq
    in_specs = [
        pl.BlockSpec((4, DIFF_QK), lambda b, h, i: (0, 0)),
        pl.BlockSpec((ATT_T, HEAD_DIM), lambda b, h, i: (b * nq + i, h)),
        pl.BlockSpec((seq, HEAD_DIM), lambda b, h, i: (b, DIFF_HEADS + h)),
        pl.BlockSpec((seq, HEAD_DIM), lambda b, h, i: (b, 2 * DIFF_HEADS + h)),
        pl.BlockSpec((ATT_T, HEAD_DIM), lambda b, h, i: (b * nq + i, 3 * DIFF_HEADS + h)),
        pl.BlockSpec((1, HEAD_DIM), lambda b, h, i: (0, 0)),
    ]
    out_specs = [pl.BlockSpec((ATT_T, HEAD_DIM), lambda b, h, i: (b * nq + i, h))]
    out_shape = [jax.ShapeDtypeStruct((t, DIFF_W), BF16)]
    args = [lam_vecs, diff_p, diff_p, diff_p, diff_p, subg.reshape(1, HEAD_DIM)]
    if next_weights is not None:
        w_in, w_out, layer = next_weights

        def slab(b, h, i):
            return (b * DIFF_HEADS + h) * nq + i

        for w in (w_in, w_out):
            rows = w.shape[1] // steps
            in_specs.append(pl.BlockSpec((None, rows, w.shape[2]),
                                         lambda b, h, i: (layer, slab(b, h, i), 0)))
            out_specs.append(pl.BlockSpec((rows, w.shape[2]),
                                          lambda b, h, i: (slab(b, h, i), 0)))
            out_shape.append(jax.ShapeDtypeStruct(w.shape[1:], BF16))
            args.append(w)
    return pl.pallas_call(
        functools.partial(_diff_kernel, lam_init=lam_init,
                          cast_next=next_weights is not None),
        grid=(batch, DIFF_HEADS, nq),
        in_specs=in_specs,
        out_specs=out_specs,
        out_shape=out_shape,
        scratch_shapes=[pltpu.VMEM((2 * ATT_T, HEAD_DIM), BF16),
                        pltpu.VMEM((2 * ATT_T, LANES), F32),
                        pltpu.VMEM((2 * ATT_T, 2 * HEAD_DIM), F32),
                        pltpu.VMEM((seq, 2 * HEAD_DIM), BF16)],
        compiler_params=pltpu.CompilerParams(
            dimension_semantics=("arbitrary", "arbitrary", "arbitrary"),
            vmem_limit_bytes=VMEM_LIMIT),
        name="diff_attention",
    )(*args)


def _outproj_kernel(ret_ref, diff_ref, lru_ref, w_ref, g_ref, x_ref, o_ref):
    rows = x_ref.shape[0] // OUT_GROUPS
    for i in range(OUT_GROUPS):
        rs = slice(i * rows, (i + 1) * rows)
        y = jnp.dot(ret_ref[rs, :], w_ref[0:RET_W, :], preferred_element_type=F32)
        y = y + jnp.dot(diff_ref[rs, :], w_ref[RET_W:RET_W + DIFF_W, :],
                        preferred_element_type=F32)
        y = y + jnp.dot(lru_ref[rs, :], w_ref[RET_W + DIFF_W:D_MIX, :],
                        preferred_element_type=F32)
        ms = jnp.mean(y * y, axis=-1, keepdims=True)
        o_ref[rs, :] = x_ref[rs, :] + (y * lax.rsqrt(ms + EPS)) * g_ref[...]


def _outproj(ret_o, diff_o, lru_o, w_bf16, g, x2d):
    t = x2d.shape[0]
    tm = OUT_TM
    return pl.pallas_call(
        _outproj_kernel,
        grid=(t // tm,),
        in_specs=[pl.BlockSpec((tm, RET_W), lambda i: (i, 0)),
                  pl.BlockSpec((tm, DIFF_W), lambda i: (i, 0)),
                  pl.BlockSpec((tm, LRU_W), lambda i: (i, 0)),
                  pl.BlockSpec((D_MIX, D_MODEL), lambda i: (0, 0),
                               pipeline_mode=pl.Buffered(1)),
                  pl.BlockSpec((1, D_MODEL), lambda i: (0, 0)),
                  pl.BlockSpec((tm, D_MODEL), lambda i: (i, 0))],
        out_specs=pl.BlockSpec((tm, D_MODEL), lambda i: (i, 0)),
        out_shape=jax.ShapeDtypeStruct((t, D_MODEL), F32),
        compiler_params=pltpu.CompilerParams(
            dimension_semantics=("arbitrary",), vmem_limit_bytes=VMEM_LIMIT),
        name="outproj",
    )(ret_o, diff_o, lru_o, w_bf16, g.reshape(1, D_MODEL), x2d)


def kernel(x, positions, pre_norm_g, w_in, diff_lambda_q1, diff_lambda_k1, diff_lambda_q2,
           diff_lambda_k2, diff_subln_g, lru_conv_w, lru_conv_b, lru_wa, lru_ba, lru_wx,
           lru_bx, lru_lambda, w_out, post_norm_g):
    batch, seq, _ = x.shape
    depth = w_in.shape[0]
    assert seq % ATT_T == 0 and seq % RET_ROWS == 0 and seq % INPROJ_TM == 0
    assert ATT_TK == 2 * ATT_T
    bf16_rows = 2 * SUBLANES
    for steps in (batch * seq // ROPE_TM, batch * DIFF_HEADS * (seq // ATT_T)):
        assert D_MODEL % (steps * bf16_rows) == 0 and D_MIX % (steps * bf16_rows) == 0
    x2d = x.reshape(batch * seq, D_MODEL)
    tabs, w_in_b, w_out_b = _rope_tables(positions, w_in, w_out)
    ret_tabs = _retention_tables()
    for l in range(depth):
        ret_p, diff_p, lru_o = _inproj(
            x2d, pre_norm_g[l], w_in_b, tabs, lru_conv_w[l], lru_conv_b[l], lru_wa[l],
            lru_ba[l], lru_wx[l], lru_bx[l], lru_lambda[l], seq)
        ret_o = _retention(ret_p, ret_tabs, batch, seq)
        lam_vecs = jnp.stack([diff_lambda_q1[l], diff_lambda_k1[l],
                              diff_lambda_q2[l], diff_lambda_k2[l]])
        lam_init = 0.8 - 0.6 * math.exp(-0.3 * l)
        w_out_cur = w_out_b
        if l + 1 < depth:
            diff_o, w_in_b, w_out_b = _diff_attention(
                diff_p, lam_vecs, diff_subln_g[l], lam_init, batch, seq, (w_in, w_out, l + 1))
        else:
            diff_o, = _diff_attention(diff_p, lam_vecs, diff_subln_g[l], lam_init, batch, seq)
        x2d = _outproj(ret_o, diff_o, lru_o, w_out_cur, post_norm_g[l], x2d)
    return x2d.reshape(batch, seq, D_MODEL)
```

```python
import functools
import math

import jax
import jax.numpy as jnp
import numpy as np
from jax import lax
from jax.experimental import pallas as pl
from jax.experimental.pallas import tpu as pltpu

F32 = jnp.float32
BF16 = jnp.bfloat16

D_MODEL = 2048
HEAD_DIM = 128
RET_HEADS = 6
RET_W = RET_HEADS * HEAD_DIM
RET_CHUNK = 128
DIFF_HEADS = 4
DIFF_QK = HEAD_DIM // 2
DIFF_W = DIFF_HEADS * HEAD_DIM
LRU_W = 768
LRU_BLOCKS = 8
LRU_BW = LRU_W // LRU_BLOCKS
CONV_W = 4
LRU_C = 8.0
D_MIX = RET_W + DIFF_W + LRU_W
RET_COLS = 4 * RET_W
DIFF_COLS = 4 * DIFF_W
LRU_COLS = 2 * LRU_W
D_IN = RET_COLS + DIFF_COLS + LRU_COLS
ROPE_THETA = 10000.0
EPS = 1e-6
LOG2E = math.log2(math.e)

LANES = 128
SUBLANES = 8
VMEM_LIMIT = 52 * 1024 * 1024

ROPE_TM = 1024
INPROJ_TM = 256
INPROJ_TN = 512
RET_ROWS = 2048
ATT_T = 1024
ATT_TK = 2048
ATT_ROWS = 256
ATT_AHEAD = 8
LRU_GROUPS = 4
OUT_TM = 512
OUT_GROUPS = 2
OUT_PIECES = 2


def _silu(x):
    return x * jax.nn.sigmoid(x)


def _rope_table_kernel(pos_ref, c_ref, win_ref, wout_ref,
                       cos_r, sin_r, cos_d, sin_d, winb_ref, woutb_ref):
    winb_ref[...] = win_ref[...].astype(BF16)
    woutb_ref[...] = wout_ref[...].astype(BF16)
    pos = pos_ref[...].astype(F32)
    ang = pos * c_ref[0:1, :]
    c = jnp.cos(ang)
    s = jnp.sin(ang)
    lane = lax.broadcasted_iota(jnp.int32, ang.shape, 1)
    half = HEAD_DIM // 2
    q = DIFF_QK // 2
    low = lane < half
    cos_r[...] = jnp.where(low, c, pltpu.roll(c, half, 1))
    sin_r[...] = jnp.where(low, s, pltpu.roll(s, half, 1)) * c_ref[1:2, :]

    def spread(x):
        return jnp.where(lane < q, pltpu.roll(x, half, 1),
                         jnp.where(lane < 2 * q, pltpu.roll(x, half + q, 1),
                                   jnp.where(lane < 3 * q, x, pltpu.roll(x, q, 1))))

    cos_d[...] = spread(c)
    sin_d[...] = spread(s) * c_ref[2:3, :]


def _rope_tables(positions, w_in, w_out):
    t = positions.size
    steps = t // ROPE_TM
    half = HEAD_DIM // 2
    inv_r = 1.0 / (ROPE_THETA ** jnp.linspace(0.0, 1.0, half, dtype=F32))
    inv_d = 1.0 / (ROPE_THETA ** (jnp.arange(0, DIFF_QK, 2, dtype=F32) / DIFF_QK))
    one = jnp.ones((DIFF_QK // 2,), F32)
    consts = jnp.stack([
        jnp.concatenate([inv_r, inv_d, jnp.zeros((LANES - half - DIFF_QK // 2,), F32)]),
        jnp.concatenate([-jnp.ones((half,), F32), jnp.ones((half,), F32)]),
        jnp.concatenate([-one, one, -one, one]),
    ])
    consts = jnp.concatenate([consts, jnp.zeros((SUBLANES - 3, LANES), F32)])
    tab = jax.ShapeDtypeStruct((t, LANES), F32)
    row_spec = pl.BlockSpec((ROPE_TM, LANES), lambda i: (i, 0))
    w_specs_in, w_specs_out, w_shapes = [], [], []
    for w in (w_in, w_out):
        rows = w.shape[1] // steps
        w_specs_in.append(pl.BlockSpec((None, rows, w.shape[2]), lambda i: (0, i, 0)))
        w_specs_out.append(pl.BlockSpec((rows, w.shape[2]), lambda i: (i, 0)))
        w_shapes.append(jax.ShapeDtypeStruct(w.shape[1:], BF16))
    outs = pl.pallas_call(
        _rope_table_kernel,
        grid=(steps,),
        in_specs=[pl.BlockSpec((ROPE_TM, 1), lambda i: (i, 0)),
                  pl.BlockSpec((SUBLANES, LANES), lambda i: (0, 0))] + w_specs_in,
        out_specs=[row_spec] * 4 + w_specs_out,
        out_shape=[tab] * 4 + w_shapes,
        compiler_params=pltpu.CompilerParams(
            dimension_semantics=("arbitrary",), vmem_limit_bytes=VMEM_LIMIT),
        name="rope_tables",
    )(positions.reshape(t, 1), consts, w_in, w_out)
    return outs[:4], outs[4], outs[5]


def _rope_r(x, cos, sin):
    return x * cos + pltpu.roll(x, HEAD_DIM // 2, 1) * sin


def _rope_d(x, cos, sin, first_half):
    q = DIFF_QK // 2
    partner = jnp.where(first_half, pltpu.roll(x, LANES - q, 1), pltpu.roll(x, q, 1))
    return x * cos + partner * sin


def _inproj_kernel(x_ref, g_ref, w_ref, cr_ref, sr_ref, cd_ref, sd_ref,
                   cw_ref, cb_ref, gw_ref, gb_ref, lam_ref, never_ref,
                   ret_ref, diff_ref, lru_ref,
                   xn_scr, xs_scr, hc_scr, hs_scr, *, steps_per_seq):
    rows = x_ref.shape[0]
    pad = SUBLANES
    seq_start = pl.program_id(0) % steps_per_seq == 0

    @pl.when(seq_start)
    def _():
        xs_scr[0:pad, :] = jnp.zeros((pad, LRU_W), F32)
        hc_scr[...] = jnp.zeros_like(hc_scr)

    @pl.when(jnp.logical_not(seq_start))
    def _():
        xs_scr[0:pad, :] = xs_scr[rows:rows + pad, :]

    x = x_ref[...]
    xn_scr[...] = (x * g_ref[...]).astype(BF16)
    ms = jnp.mean(x * x, axis=-1, keepdims=True)
    rinv = jnp.broadcast_to(lax.rsqrt(ms + EPS), (rows, LANES))
    lane = lax.broadcasted_iota(jnp.int32, (rows, LANES), 1)
    first_half = (lane % DIFF_QK) < (DIFF_QK // 2)

    def proj(c0, width):
        return jnp.dot(xn_scr[...], w_ref[:, c0:c0 + width], preferred_element_type=F32)

    def emit(blk):
        c0 = blk * INPROJ_TN
        acc = proj(c0, INPROJ_TN)
        for s in range(INPROJ_TN // LANES):
            c = c0 + s * LANES
            y = acc[:, s * LANES:(s + 1) * LANES] * rinv
            if c < 2 * RET_W:
                y = _rope_r(y, cr_ref[...], sr_ref[...])
                if c >= RET_W:
                    y = y * (HEAD_DIM ** -0.5)
            elif RET_COLS <= c < RET_COLS + 2 * DIFF_W:
                y = _rope_d(y, cd_ref[...], sd_ref[...], first_half)
                if c < RET_COLS + DIFF_W:
                    y = y * (DIFF_QK ** -0.5 * LOG2E)
            y = y.astype(BF16)
            if c < RET_COLS:
                ret_ref[:, c:c + LANES] = y
            else:
                diff_ref[:, c - RET_COLS:c - RET_COLS + LANES] = y
        return acc[rows - SUBLANES:rows, INPROJ_TN - LANES:INPROJ_TN]

    lru0 = RET_COLS + DIFF_COLS
    rinv_w = jnp.tile(rinv, (1, LRU_W // LANES))
    xs_scr[pad:pad + rows, :] = proj(lru0, LRU_W) * rinv_w
    lg = proj(lru0 + LRU_W, LRU_W) * rinv_w
    xc = cb_ref[...] + xs_scr[pad - 3:pad - 3 + rows, :] * cw_ref[0:1, :]
    for j in range(1, CONV_W):
        xc = xc + xs_scr[pad - 3 + j:pad - 3 + j + rows, :] * cw_ref[j:j + 1, :]

    emit(0)
    emit(1)

    xcb = xc.astype(BF16)
    tile_w = 2 * LANES
    gate_cols = []
    for half in range(2):
        for j in range(LRU_W // tile_w):
            lo = (j * tile_w // LRU_BW) * LRU_BW
            hi = min(LRU_W, -(-((j + 1) * tile_w) // LRU_BW) * LRU_BW)
            k0 = (lo // tile_w) * tile_w
            k1 = min(LRU_W, -(-hi // tile_w) * tile_w)
            c0 = half * LRU_W + j * tile_w
            gate_cols.append(jnp.dot(xcb[:, k0:k1], gw_ref[k0:k1, c0:c0 + tile_w],
                                     preferred_element_type=F32))
    gates = jnp.concatenate(gate_cols, axis=1) + gb_ref[...]
    lam = lam_ref[...]
    log_sig = jnp.minimum(lam, 0.0) - jnp.log1p(jnp.exp(-jnp.abs(lam)))
    row = lax.broadcasted_iota(jnp.int32, (SUBLANES, LRU_W), 0)

    def lru_rows(rs, carry, after):
        never = never_ref[0] != 0
        g = gates[rs, :]
        g = jnp.where(never, jnp.tile(after, (g.shape[0] // SUBLANES, g.shape[1] // LANES)), g)
        r = jax.nn.sigmoid(g[:, :LRU_W])
        i = jax.nn.sigmoid(g[:, LRU_W:])
        log_a = (LRU_C * r) * log_sig
        a = jnp.exp(log_a)
        mult = jnp.sqrt(-jnp.tanh(log_a) * (a * a + 1.0))
        b = mult * (i * xc[rs, :])
        for t in range((rs.stop - rs.start) // SUBLANES):
            av = a[t * SUBLANES:(t + 1) * SUBLANES, :]
            bv = b[t * SUBLANES:(t + 1) * SUBLANES, :]
            for d in (1, 2, 4):
                a_prev = jnp.where(row >= d, pltpu.roll(av, d, 0), 1.0)
                b_prev = jnp.where(row >= d, pltpu.roll(bv, d, 0), 0.0)
                bv = av * b_prev + bv
                av = av * a_prev
            hv = av * carry + bv
            r0 = rs.start + t * SUBLANES
            hs_scr[r0:r0 + SUBLANES, :] = hv
            carry = jnp.broadcast_to(hv[SUBLANES - 1:SUBLANES, :], (SUBLANES, LRU_W))
        lru_ref[rs, :] = (hs_scr[rs, :] * _silu(lg[rs, :])).astype(BF16)
        return carry

    carry = hc_scr[...]
    blk = 2
    for gi in range(LRU_GROUPS):
        after = emit(blk)
        blk += 1
        carry = lru_rows(slice(gi * rows // LRU_GROUPS, (gi + 1) * rows // LRU_GROUPS),
                         carry, after)
    hc_scr[...] = carry
    while blk < lru0 // INPROJ_TN:
        emit(blk)
        blk += 1


def _block_diag(w):
    n, k, _ = w.shape
    eye = jnp.eye(n, dtype=w.dtype)
    return (w[:, :, None, :] * eye[:, None, :, None]).reshape(n * k, n * k)


def _inproj(x2d, g, w_bf16, tabs, conv_w, conv_b, wa, ba, wx, bx, lam, seq):
    t = x2d.shape[0]
    tm = INPROJ_TM
    gate_w = jnp.concatenate([_block_diag(wa), _block_diag(wx)], axis=1).astype(BF16)
    gate_b = jnp.concatenate([ba, bx]).reshape(1, 2 * LRU_W)
    tab_spec = pl.BlockSpec((tm, LANES), lambda i: (i, 0))

    def const(shape):
        return pl.BlockSpec(shape, lambda i: (0, 0))

    return pl.pallas_call(
        functools.partial(_inproj_kernel, steps_per_seq=seq // tm),
        grid=(t // tm,),
        in_specs=[pl.BlockSpec((tm, D_MODEL), lambda i: (i, 0)),
                  const((1, D_MODEL)),
                  pl.BlockSpec((D_MODEL, D_IN), lambda i: (0, 0),
                               pipeline_mode=pl.Buffered(1)),
                  tab_spec, tab_spec, tab_spec, tab_spec,
                  const((CONV_W, LRU_W)), const((1, LRU_W)),
                  pl.BlockSpec((LRU_W, 2 * LRU_W), lambda i: (0, 0),
                               pipeline_mode=pl.Buffered(1)),
                  const((1, 2 * LRU_W)), const((1, LRU_W)),
                  pl.BlockSpec(memory_space=pltpu.SMEM)],
        out_specs=[pl.BlockSpec((tm, RET_COLS), lambda i: (i, 0)),
                   pl.BlockSpec((tm, DIFF_COLS), lambda i: (i, 0)),
                   pl.BlockSpec((tm, LRU_W), lambda i: (i, 0))],
        out_shape=[jax.ShapeDtypeStruct((t, RET_COLS), BF16),
                   jax.ShapeDtypeStruct((t, DIFF_COLS), BF16),
                   jax.ShapeDtypeStruct((t, LRU_W), BF16)],
        scratch_shapes=[pltpu.VMEM((tm, D_MODEL), BF16),
                        pltpu.VMEM((tm + SUBLANES, LRU_W), F32),
                        pltpu.VMEM((SUBLANES, LRU_W), F32),
                        pltpu.VMEM((tm, LRU_W), F32)],
        compiler_params=pltpu.CompilerParams(
            dimension_semantics=("arbitrary",), vmem_limit_bytes=VMEM_LIMIT),
        name="inproj_lru",
    )(x2d, g.reshape(1, D_MODEL), w_bf16, *tabs, conv_w, conv_b.reshape(1, LRU_W),
      gate_w, gate_b, lam.reshape(1, LRU_W), jnp.zeros((1,), jnp.int32))


def _retention_tables():
    c = RET_CHUNK
    f32 = np.float32
    log_g = np.log1p(-np.power(f32(2.0), f32(-5.0) - np.arange(RET_HEADS, dtype=f32)))
    idx = np.arange(c, dtype=f32)
    rel = idx[:, None] - idx[None, :]
    intra = np.where(rel[None] >= 0,
                     np.exp(np.maximum(rel, f32(0.0))[None] * log_g[:, None, None]), f32(0.0))
    k_w = np.exp((f32(c - 1.0) - idx)[None, :] * log_g[:, None])
    q_w = np.exp((idx + f32(1.0))[None, :] * log_g[:, None])
    cdec = np.exp(f32(c) * log_g)
    full = (RET_HEADS, c, HEAD_DIM)
    return tuple(jnp.asarray(np.ascontiguousarray(t), dtype=F32) for t in (
        intra,
        np.broadcast_to(q_w[:, :, None], full),
        np.broadcast_to(k_w[:, :, None], full),
        np.broadcast_to(cdec[:, None, None], full)))


def _retention_kernel(q_ref, k_ref, v_ref, g_ref, dec_ref, qw_ref, kw_ref, cd_ref,
                      o_ref, state_ref):
    @pl.when(pl.program_id(1) == 0)
    def _():
        state_ref[...] = jnp.zeros_like(state_ref)

    nt = (((1,), (1,)), ((), ()))
    tn = (((0,), (0,)), ((), ()))
    for c in range(RET_ROWS // RET_CHUNK):
        rows = slice(c * RET_CHUNK, (c + 1) * RET_CHUNK)
        for h in range(RET_HEADS):
            cols = slice(h * HEAD_DIM, (h + 1) * HEAD_DIM)
            q = q_ref[rows, cols]
            k = k_ref[rows, cols]
            v = v_ref[rows, cols]
            st = state_ref[h]
            rhs = jnp.concatenate([k, st.astype(BF16)], axis=0)
            r = lax.dot_general(q, rhs, nt, preferred_element_type=F32)
            s = r[:, :RET_CHUNK] * dec_ref[h]
            o = (jnp.dot(s.astype(BF16), v, preferred_element_type=F32)
                 + r[:, RET_CHUNK:] * qw_ref[h])
            ks = (k.astype(F32) * kw_ref[h]).astype(BF16)
            kv_t = lax.dot_general(v, ks, tn, preferred_element_type=F32)
            state_ref[h] = st * cd_ref[h] + kv_t
            ms = jnp.mean(o * o, axis=-1, keepdims=True)
            y = o * lax.rsqrt(ms + EPS)
            o_ref[rows, cols] = (y * _silu(g_ref[rows, cols].astype(F32))).astype(BF16)


def _retention(ret_p, tables, batch, seq):
    t = batch * seq
    steps = seq // RET_ROWS

    def col(j):
        return pl.BlockSpec((RET_ROWS, RET_W), lambda b, i, j=j: (b * steps + i, j))

    tab_spec = pl.BlockSpec((RET_HEADS, RET_CHUNK, HEAD_DIM), lambda b, i: (0, 0, 0))
    return pl.pallas_call(
        _retention_kernel,
        grid=(batch, steps),
        in_specs=[col(0), col(1), col(2), col(3), tab_spec, tab_spec, tab_spec, tab_spec],
        out_specs=pl.BlockSpec((RET_ROWS, RET_W), lambda b, i: (b * steps + i, 0)),
        out_shape=jax.ShapeDtypeStruct((t, RET_W), BF16),
        scratch_shapes=[pltpu.VMEM((RET_HEADS, HEAD_DIM, HEAD_DIM), F32)],
        compiler_params=pltpu.CompilerParams(
            dimension_semantics=("arbitrary", "arbitrary"), vmem_limit_bytes=VMEM_LIMIT),
        name="retention",
    )(ret_p, ret_p, ret_p, ret_p, *tables)


def _diff_kernel(lam_ref, q_ref, k_ref, v_ref, g_ref, subg_ref, *refs, lam_init, cast_next):
    if cast_next:
        win_ref, wout_ref, o_ref, winb_ref, woutb_ref, qm_scr, m_scr, acc_scr, v1_scr = refs
        winb_ref[...] = win_ref[...].astype(BF16)
        woutb_ref[...] = wout_ref[...].astype(BF16)
    else:
        o_ref, qm_scr, m_scr, acc_scr, v1_scr = refs
    qi = pl.program_id(2)

    @pl.when(qi == 0)
    def _():
        v1_scr[:, 0:HEAD_DIM] = v_ref[...]
        v1_scr[:, HEAD_DIM:2 * HEAD_DIM] = jnp.ones(v_ref.shape, BF16)

    q = q_ref[...]
    lane = lax.broadcasted_iota(jnp.int32, q.shape, 1)
    qm_scr[0:ATT_T, :] = jnp.where(lane < DIFF_QK, q, jnp.zeros_like(q))
    qm_scr[ATT_T:2 * ATT_T, :] = jnp.where(lane >= DIFF_QK, q, jnp.zeros_like(q))
    nt = (((1,), (1,)), ((), ()))

    def lane_fold(x, op):
        parts = [x[:, i * LANES:(i + 1) * LANES] for i in range(x.shape[1] // LANES)]
        while len(parts) > 1:
            nxt = [op(parts[i], parts[i + 1]) for i in range(0, len(parts) - 1, 2)]
            parts = nxt + parts[len(parts) - len(parts) % 2:]
        return parts[0]

    def block(r0, size, diagonal):
        k = k_ref[pl.ds(r0, size), :]
        v = v1_scr[pl.ds(r0, size), :]
        n_chunks = 2 * ATT_T // ATT_ROWS

        def scores(i):
            rows = slice(i * ATT_ROWS, (i + 1) * ATT_ROWS)
            keys = size - ATT_T + (i * ATT_ROWS) % ATT_T + ATT_ROWS if diagonal else size
            s = lax.dot_general(qm_scr[rows, :], k[:keys, :], nt, preferred_element_type=F32)
            return rows, keys, s

        queued = [scores(i) for i in range(min(ATT_AHEAD, n_chunks))]
        for i in range(n_chunks):
            rows, keys, s = queued.pop(0)
            if i + ATT_AHEAD < n_chunks:
                queued.append(scores(i + ATT_AHEAD))
            if diagonal:
                tail = s[:, keys - ATT_ROWS:]
                row = lax.broadcasted_iota(jnp.int32, tail.shape, 0)
                col = lax.broadcasted_iota(jnp.int32, tail.shape, 1)
                tail = jnp.where(col <= row, tail, -jnp.inf)
                s = tail if keys == ATT_ROWS else jnp.concatenate(
                    [s[:, :keys - ATT_ROWS], tail], axis=1)
            m_cur = jnp.max(lane_fold(s, jnp.maximum), axis=-1, keepdims=True)
            if diagonal:
                m_new = jnp.broadcast_to(m_cur, (ATT_ROWS, LANES))
            else:
                m_prev = m_scr[rows, :]
                m_new = jnp.maximum(m_prev, m_cur)
                alpha = jnp.exp2(m_prev - m_new)
            p = jnp.exp2(s - jnp.tile(m_new, (1, keys // LANES)))
            acc_new = jnp.dot(p.astype(BF16), v[:keys, :], preferred_element_type=F32)
            if not diagonal:
                acc_new = jnp.tile(alpha, (1, 2)) * acc_scr[rows, :] + acc_new
            acc_scr[rows, :] = acc_new
            m_scr[rows, :] = m_new

    def body(kk, carry):
        block(pl.multiple_of(kk * ATT_TK, ATT_TK), ATT_TK, False)
        return carry

    below = qi * ATT_T
    n_big = below // ATT_TK
    odd = below - n_big * ATT_TK > 0

    @pl.when(odd)
    def _():
        block(pl.multiple_of(below - ATT_T, ATT_T), 2 * ATT_T, True)

    @pl.when(jnp.logical_not(odd))
    def _():
        block(pl.multiple_of(below, ATT_T), ATT_T, True)

    lax.fori_loop(0, n_big, body, 0)

    lv = lam_ref[...]
    lam = (jnp.exp(jnp.sum(lv[0:1] * lv[1:2], axis=-1, keepdims=True))
           - jnp.exp(jnp.sum(lv[2:3] * lv[3:4], axis=-1, keepdims=True)) + lam_init)
    o = (acc_scr[0:ATT_T, :HEAD_DIM] / acc_scr[0:ATT_T, HEAD_DIM:]
         - lam * (acc_scr[ATT_T:2 * ATT_T, :HEAD_DIM] / acc_scr[ATT_T:2 * ATT_T, HEAD_DIM:]))
    ms = jnp.mean(o * o, axis=-1, keepdims=True)
    y = ((o * lax.rsqrt(ms + EPS)) * subg_ref[...]) * (1.0 - lam_init)
    o_ref[...] = (y * _silu(g_ref[...].astype(F32))).astype(BF16)


def _diff_attention(diff_p, lam_vecs, subg, lam_init, batch, seq, next_weights=None):
    t = batch * seq
    nq = seq // ATT_T
    steps = batch * DIFF_HEADS * nq
    in_specs = [
        pl.BlockSpec((4, DIFF_QK), lambda b, h, i: (0, 0)),
        pl.BlockSpec((ATT_T, HEAD_DIM), lambda b, h, i: (b * nq + i, h)),
        pl.BlockSpec((seq, HEAD_DIM), lambda b, h, i: (b, DIFF_HEADS + h)),
        pl.BlockSpec((seq, HEAD_DIM), lambda b, h, i: (b, 2 * DIFF_HEADS + h)),
        pl.BlockSpec((ATT_T, HEAD_DIM), lambda b, h, i: (b * nq + i, 3 * DIFF_HEADS + h)),
        pl.BlockSpec((1, HEAD_DIM), lambda b, h, i: (0, 0)),
    ]
    out_specs = [pl.BlockSpec((ATT_T, HEAD_DIM), lambda b, h, i: (b * nq + i, h))]
    out_shape = [jax.ShapeDtypeStruct((t, DIFF_W), BF16)]
    args = [lam_vecs, diff_p, diff_p, diff_p, diff_p, subg.reshape(1, HEAD_DIM)]
    if next_weights is not None:
        w_in, w_out, layer = next_weights

        def slab(b, h, i):
            return (b * DIFF_HEADS + h) * nq + i

        for w in (w_in, w_out):
            rows = w.shape[1] // steps
            in_specs.append(pl.BlockSpec((None, rows, w.shape[2]),
                                         lambda b, h, i: (layer, slab(b, h, i), 0)))
            out_specs.append(pl.BlockSpec((rows, w.shape[2]),
                                          lambda b, h, i: (slab(b, h, i), 0)))
            out_shape.append(jax.ShapeDtypeStruct(w.shape[1:], BF16))
            args.append(w)
    return pl.pallas_call(
        functools.partial(_diff_kernel, lam_init=lam_init,
                          cast_next=next_weights is not None),
        grid=(batch, DIFF_HEADS, nq),
        in_specs=in_specs,
        out_specs=out_specs,
        out_shape=out_shape,
        scratch_shapes=[pltpu.VMEM((2 * ATT_T, HEAD_DIM), BF16),
                        pltpu.VMEM((2 * ATT_T, LANES), F32),
                        pltpu.VMEM((2 * ATT_T, 2 * HEAD_DIM), F32),
                        pltpu.VMEM((seq, 2 * HEAD_DIM), BF16)],
        compiler_params=pltpu.CompilerParams(
            dimension_semantics=("arbitrary", "arbitrary", "arbitrary"),
            vmem_limit_bytes=VMEM_LIMIT),
        name="diff_attention",
    )(*args)


def _outproj_kernel(ret_ref, diff_ref, lru_ref, w_ref, g_ref, x_ref, never_ref, o_ref):
    rows = x_ref.shape[0] // OUT_GROUPS
    never = never_ref[0] != 0
    srcs = ((ret_ref, 0, RET_W), (diff_ref, RET_W, RET_W + DIFF_W),
            (lru_ref, RET_W + DIFF_W, D_MIX))

    def partials(i):
        rs = slice(i * rows, (i + 1) * rows)
        return [jnp.dot(ref[rs, :], w_ref[k0:k1, :], preferred_element_type=F32)
                for ref, k0, k1 in srcs]

    def finish(i, y, gates):
        piece = rows // len(gates)
        for j, gate in enumerate(gates):
            yy = y[j * piece:(j + 1) * piece, :]
            if gate is not None:
                yy = jnp.where(never, jnp.tile(gate, (piece // SUBLANES, D_MODEL // LANES)), yy)
            ms = jnp.mean(yy * yy, axis=-1, keepdims=True)
            rs = slice(i * rows + j * piece, i * rows + (j + 1) * piece)
            o_ref[rs, :] = x_ref[rs, :] + (yy * lax.rsqrt(ms + EPS)) * g_ref[...]

    prev = None
    for i in range(OUT_GROUPS):
        parts = partials(i)
        if prev is not None:
            finish(i - 1, prev, [p[rows - SUBLANES:rows, 0:LANES] for p in parts[:OUT_PIECES]])
        prev = parts[0] + parts[1] + parts[2]
    finish(OUT_GROUPS - 1, prev, [None])


def _outproj(ret_o, diff_o, lru_o, w_bf16, g, x2d):
    t = x2d.shape[0]
    tm = OUT_TM
    return pl.pallas_call(
        _outproj_kernel,
        grid=(t // tm,),
        in_specs=[pl.BlockSpec((tm, RET_W), lambda i: (i, 0)),
                  pl.BlockSpec((tm, DIFF_W), lambda i: (i, 0)),
                  pl.BlockSpec((tm, LRU_W), lambda i: (i, 0)),
                  pl.BlockSpec((D_MIX, D_MODEL), lambda i: (0, 0),
                               pipeline_mode=pl.Buffered(1)),
                  pl.BlockSpec((1, D_MODEL), lambda i: (0, 0)),
                  pl.BlockSpec((tm, D_MODEL), lambda i: (i, 0)),
                  pl.BlockSpec(memory_space=pltpu.SMEM)],
        out_specs=pl.BlockSpec((tm, D_MODEL), lambda i: (i, 0)),
        out_shape=jax.ShapeDtypeStruct((t, D_MODEL), F32),
        compiler_params=pltpu.CompilerParams(
            dimension_semantics=("arbitrary",), vmem_limit_bytes=VMEM_LIMIT),
        name="outproj",
    )(ret_o, diff_o, lru_o, w_bf16, g.reshape(1, D_MODEL), x2d, jnp.zeros((1,), jnp.int32))


def kernel(x, positions, pre_norm_g, w_in, diff_lambda_q1, diff_lambda_k1, diff_lambda_q2,
           diff_lambda_k2, diff_subln_g, lru_conv_w, lru_conv_b, lru_wa, lru_ba, lru_wx,
           lru_bx, lru_lambda, w_out, post_norm_g):
    batch, seq, _ = x.shape
    depth = w_in.shape[0]
    assert seq % ATT_T == 0 and seq % RET_ROWS == 0 and seq % INPROJ_TM == 0
    assert ATT_TK == 2 * ATT_T
    bf16_rows = 2 * SUBLANES
    for steps in (batch * seq // ROPE_TM, batch * DIFF_HEADS * (seq // ATT_T)):
        assert D_MODEL % (steps * bf16_rows) == 0 and D_MIX % (steps * bf16_rows) == 0
    x2d = x.reshape(batch * seq, D_MODEL)
    tabs, w_in_b, w_out_b = _rope_tables(positions, w_in, w_out)
    ret_tabs = _retention_tables()
    for l in range(depth):
        ret_p, diff_p, lru_o = _inproj(
            x2d, pre_norm_g[l], w_in_b, tabs, lru_conv_w[l], lru_conv_b[l], lru_wa[l],
            lru_ba[l], lru_wx[l], lru_bx[l], lru_lambda[l], seq)
        ret_o = _retention(ret_p, ret_tabs, batch, seq)
        lam_vecs = jnp.stack([diff_lambda_q1[l], diff_lambda_k1[l],
                              diff_lambda_q2[l], diff_lambda_k2[l]])
        lam_init = 0.8 - 0.6 * math.exp(-0.3 * l)
        w_out_cur = w_out_b
        if l + 1 < depth:
            diff_o, w_in_b, w_out_b = _diff_attention(
                diff_p, lam_vecs, diff_subln_g[l], lam_init, batch, seq, (w_in, w_out, l + 1))
        else:
            diff_o, = _diff_attention(diff_p, lam_vecs, diff_subln_g[l], lam_init, batch, seq)
        x2d = _outproj(ret_o, diff_o, lru_o, w_out_cur, post_norm_g[l], x2d)
    return x2d.reshape(batch, seq, D_MODEL)
```
